```python
import math
import jax, jax.numpy as jnp
from jax import lax
import numpy as np

D_MODEL = 1024
BATCH = 4
SEQ = 4096
DEPTH = 2
DEC_BATCH = 128
DEC_SEQ = 8
PAST_LEN = 2048
PAGE_SIZE = 128

N_EVEN = (DEPTH + 1) // 2
N_ODD = DEPTH // 2
D_MIX = D_MODEL
D_A = D_MIX // 2
CONV_W = 31
H_B = 4
DH_B = (D_MIX // 2) // H_B
D_B = H_B * DH_B
H_C = 4
DH_C = (D_MIX // 2) // (2 * H_C)
D_C = H_C * 2 * DH_C
H_D = 8
DH_D = (D_MIX // 2) // H_D
D_D = H_D * DH_D
D_FF = 2816
PLE_DIM = 256
N_BUCKETS = 32
MAX_DIST = 128
Q_BLOCK = 128
MLSTM_CHUNK = 128
IN_EVEN = 2 * D_A + 4 * D_B + 2 * H_B
IN_ODD = 3 * D_C + 3 * D_D
SPLIT_EVEN = (D_A, 2 * D_A, 2 * D_A + D_B, 2 * D_A + 2 * D_B, 2 * D_A + 3 * D_B, 2 * D_A + 4 * D_B)
SPLIT_ODD = (D_C, 2 * D_C, 3 * D_C, 3 * D_C + D_D, 3 * D_C + 2 * D_D)
NEG_INF = -1e30

kernel_name = 'hybrid_conv_mlstm_diffattn_stickbreak_step'


def rms_norm(x, g, eps=1e-6):
    xf = x.astype(jnp.float32)
    y = xf * lax.rsqrt(jnp.mean(xf * xf, axis=-1, keepdims=True) + eps)
    return (y * g.astype(jnp.float32)).astype(x.dtype)


def swiglu(x, wi, wo):
    g, u = jnp.split(x @ wi, 2, axis=-1)
    return (jax.nn.silu(g) * u) @ wo


def t5_bucket(dist):
    n = jnp.maximum(dist, 0)
    exact = N_BUCKETS // 2
    nf = jnp.maximum(n, 1).astype(jnp.float32)
    large = exact + (jnp.log(nf / exact) / math.log(MAX_DIST / exact) * (N_BUCKETS - exact)).astype(jnp.int32)
    return jnp.where(n < exact, n, jnp.minimum(large, N_BUCKETS - 1))


def mlstm_scan(q, k, v, li, lf, C0, n0, m0):
    B, T, H, _ = q.shape
    L = MLSTM_CHUNK if T % MLSTM_CHUNK == 0 else T
    nc = T // L

    def chunks(a):
        return a.reshape(B, nc, L, H, a.shape[-1]).transpose(1, 0, 3, 2, 4)

    def gchunks(a):
        return a.reshape(B, nc, L, H).transpose(1, 0, 3, 2)

    causal = jnp.tril(jnp.ones((L, L), dtype=bool))

    def step(carry, inp):
        C, n, m = carry
        qc, kc, vc, lic, lfc = inp
        b = jnp.cumsum(lfc, axis=-1)
        dmat = jnp.where(causal, b[..., :, None] - b[..., None, :] + lic[..., None, :], -jnp.inf)
        inter = b + m[..., None]
        mt = jnp.maximum(inter, jnp.max(dmat, axis=-1))
        w_inter = jnp.exp(inter - mt)
        s = jnp.einsum('bhtd,bhsd->bhts', qc, kc) * jnp.exp(dmat - mt[..., None])
        num = w_inter[..., None] * jnp.einsum('bhtd,bhde->bhte', qc, C) + jnp.einsum('bhts,bhse->bhte', s, vc)
        den = w_inter * jnp.einsum('bhtd,bhd->bht', qc, n) + jnp.sum(s, axis=-1)
        h = num / jnp.maximum(jnp.abs(den), jnp.exp(-mt))[..., None]
        m_new = mt[..., -1]
        decay = jnp.exp(b[..., -1] + m - m_new)
        w_s = jnp.exp(b[..., -1:] - b + lic - m_new[..., None])
        C_new = decay[..., None, None] * C + jnp.einsum('bhs,bhsd,bhse->bhde', w_s, kc, vc)
        n_new = decay[..., None] * n + jnp.einsum('bhs,bhsd->bhd', w_s, kc)
        return (C_new, n_new, m_new), h

    carry0 = (C0.astype(jnp.float32), n0.astype(jnp.float32), m0.astype(jnp.float32))
    (C1, n1, m1), hs = lax.scan(step, carry0, (chunks(q), chunks(k), chunks(v), gchunks(li), gchunks(lf)))
    h = hs.transpose(1, 0, 3, 2, 4).reshape(B, T, H, v.shape[-1])
    return h, C1, n1, m1


def diff_attention(q, k, v, q_pos, k_pos, rel_bias, lam):
    dist = q_pos[:, None] - k_pos[None, :]
    bias = jnp.moveaxis(rel_bias.astype(jnp.float32)[t5_bucket(dist)], -1, 0)
    mask = dist >= 0
    scale = DH_C ** -0.5

    def probs(qh, kh):
        s = jnp.einsum('bhtd,bhsd->bhts', qh, kh) * scale + bias
        return jax.nn.softmax(jnp.where(mask, s, NEG_INF), axis=-1)

    a = probs(q[..., :DH_C], k[..., :DH_C]) - lam * probs(q[..., DH_C:], k[..., DH_C:])
    return jnp.einsum('bhts,bhse->bhte', a, v)


def stick_breaking(q, k, v, q_pos, k_pos):
    dist = q_pos[:, None] - k_pos[None, :]
    mask = dist > 0
    z = jnp.einsum('bhtd,bhsd->bhts', q, k) * (q.shape[-1] ** -0.5)
    log_keep = jnp.where(mask, jax.nn.log_sigmoid(-z), 0.0)
    after = lax.cumsum(log_keep, axis=3, reverse=True) - log_keep
    weights = jnp.where(mask, jnp.exp(jax.nn.log_sigmoid(z) + after), 0.0)
    return jnp.einsum('bhts,bhse->bhte', weights, v)


def sweep_query_blocks(fn, q):
    B, H, T, _ = q.shape

    def body(i):
        start = i * Q_BLOCK
        qb = lax.dynamic_slice_in_dim(q, start, Q_BLOCK, axis=2)
        return fn(qb, start + jnp.arange(Q_BLOCK, dtype=jnp.int32))

    out = lax.map(body, jnp.arange(T // Q_BLOCK, dtype=jnp.int32))
    return jnp.moveaxis(out, 0, 2).reshape(B, H, T, out.shape[-1])


def gather_pages(pool, page_table):
    rows = pool[page_table]
    return rows.reshape(page_table.shape[0], -1, pool.shape[-2], pool.shape[-1])


def even_mixer(hn, state, W, j):
    f32 = jnp.float32
    conv_buf, C0, n0, m0 = state
    B, T, _ = hn.shape
    a_val, a_gate, q, k, v, o, gates = jnp.split(hn @ W['ev_w_in'][j], SPLIT_EVEN, axis=-1)
    u = a_val * jax.nn.sigmoid(a_gate)
    full = jnp.concatenate([conv_buf.astype(u.dtype), u], axis=1)
    taps = W['ev_conv_w'][j].astype(u.dtype)[:, None, :]
    c = lax.conv_general_dilated(full, taps, window_strides=(1,), padding='VALID',
                                 dimension_numbers=('NWC', 'WIO', 'NWC'), feature_group_count=D_A)
    c = c.astype(f32) + W['ev_conv_b'][j].astype(f32)
    mu = jnp.mean(c, axis=-1, keepdims=True)
    var = jnp.mean(jnp.square(c - mu), axis=-1, keepdims=True)
    cn = (c - mu) * lax.rsqrt(var + 1e-5) * W['ev_ln_g'][j].astype(f32) + W['ev_ln_b'][j].astype(f32)
    a_out = jax.nn.silu(cn)
    qh = q.reshape(B, T, H_B, DH_B).astype(f32)
    kh = k.reshape(B, T, H_B, DH_B).astype(f32) * (DH_B ** -0.5)
    vh = v.reshape(B, T, H_B, DH_B).astype(f32)
    g = gates.astype(f32) + W['ev_gate_b'][j].astype(f32)
    li = g[..., :H_B]
    lf = jax.nn.log_sigmoid(g[..., H_B:])
    h, C1, n1, m1 = mlstm_scan(qh, kh, vh, li, lf, C0, n0, m0)
    b_out = jax.nn.sigmoid(o.astype(f32)) * h.reshape(B, T, D_B)
    mix = jnp.concatenate([a_out, b_out], axis=-1).astype(hn.dtype) @ W['ev_w_out'][j]
    return mix, (full[:, -(CONV_W - 1):], C1, n1, m1)


def odd_mixer(hn, past, W, j, layer):
    f32 = jnp.float32
    B, T, _ = hn.shape
    cq, ck, cv, sq, sk, sv = jnp.split(hn @ W['od_w_in'][j], SPLIT_ODD, axis=-1)
    cq = cq.reshape(B, T, H_C, 2 * DH_C)
    ck = ck.reshape(B, T, H_C, 2 * DH_C)
    cv = cv.reshape(B, T, H_C, 2 * DH_C)
    sq = sq.reshape(B, T, H_D, DH_D)
    sk = sk.reshape(B, T, H_D, DH_D)
    sv = sv.reshape(B, T, H_D, DH_D)
    new_rows = (ck, cv, sk, sv)
    if past is None:
        keys = new_rows
        q_start = 0
    else:
        keys = tuple(jnp.concatenate([pk.astype(r.dtype), r], axis=1) for pk, r in zip(past, new_rows))
        q_start = past[0].shape[1]
    bhtd = lambda a: jnp.swapaxes(a, 1, 2).astype(f32)
    kc, vc, ks, vs = (bhtd(a) for a in keys)
    k_pos = jnp.arange(kc.shape[2], dtype=jnp.int32)
    lam_init = 0.8 - 0.6 * math.exp(-0.3 * layer)
    lp = W['od_lambda'][j].astype(f32)
    lam = jnp.exp(jnp.sum(lp[0] * lp[1])) - jnp.exp(jnp.sum(lp[2] * lp[3])) + lam_init
    rel_bias = W['rel_bias']
    diff_fn = lambda qb, q_pos: diff_attention(qb, kc, vc, q_pos, k_pos, rel_bias, lam)
    sb_fn = lambda qb, q_pos: stick_breaking(qb, ks, vs, q_pos, k_pos)
    qc_, qs_ = bhtd(cq), bhtd(sq)
    if past is None:
        oc = sweep_query_blocks(diff_fn, qc_)
        os_ = sweep_query_blocks(sb_fn, qs_)
    else:
        q_pos = q_start + jnp.arange(T, dtype=jnp.int32)
        oc = diff_fn(qc_, q_pos)
        os_ = sb_fn(qs_, q_pos)
    hg = W['od_head_g'][j].astype(f32).reshape(H_C, 1, 2 * DH_C)
    oc = oc * lax.rsqrt(jnp.mean(oc * oc, axis=-1, keepdims=True) + 1e-6) * hg * (1.0 - lam_init)
    oc = jnp.swapaxes(oc, 1, 2).reshape(B, T, D_C)
    os_ = jnp.swapaxes(os_, 1, 2).reshape(B, T, D_D)
    mix = jnp.concatenate([oc, os_], axis=-1).astype(hn.dtype) @ W['od_w_out'][j]
    return mix, new_rows


def trunk(x, p, even_states, odd_past, W):
    h = x
    new_even, new_odd = [], []
    for l in range(DEPTH):
        h = h + 0.5 * swiglu(rms_norm(h, W['ffn_norm1'][l]), W['ffn1_wi'][l], W['ffn1_wo'][l])
        hn = rms_norm(h, W['mix_norm'][l])
        if l % 2 == 0:
            mix, st = even_mixer(hn, even_states[l // 2], W, l // 2)
            new_even.append(st)
        else:
            mix, st = odd_mixer(hn, odd_past[l // 2], W, l // 2, l)
            new_odd.append(st)
        h = h + mix
        h = h + 0.5 * swiglu(rms_norm(h, W['ffn_norm2'][l]), W['ffn2_wi'][l], W['ffn2_wo'][l])
        gate = jax.nn.sigmoid(rms_norm(h, W['ple_norm'][l]) @ W['ple_wg'][l])
        h = h + gate * (p[l] @ W['ple_wp'][l])
    return rms_norm(h, W['final_norm']), new_even, new_odd


def setup_inputs(seed: int = 0) -> dict:
    key = jax.random.key(seed)
    ks = iter(jax.random.split(key, 48))
    nrm = lambda shape, scale=1.0: jax.random.normal(next(ks), shape, jnp.float32) * scale
    gain = lambda shape: 1.0 + 0.02 * jax.random.normal(next(ks), shape, jnp.float32)
    n_pages = PAST_LEN // PAGE_SIZE
    n_used = DEC_BATCH * n_pages
    n_pool = n_used + n_used // 4
    page_table = jax.random.permutation(next(ks), n_pool)[:n_used].reshape(DEC_BATCH, n_pages).astype(jnp.int32)
    return {
        'x_prompt': nrm((BATCH, SEQ, D_MODEL)),
        'x_sample': nrm((DEC_BATCH, DEC_SEQ, D_MODEL)),
        'p_prompt': nrm((DEPTH, BATCH, SEQ, PLE_DIM)),
        'p_sample': nrm((DEPTH, DEC_BATCH, DEC_SEQ, PLE_DIM)),
        'state_conv': nrm((N_EVEN, DEC_BATCH, CONV_W - 1, D_A), 0.5),
        'state_mlstm_C': nrm((N_EVEN, DEC_BATCH, H_B, DH_B, DH_B), 0.1),
        'state_mlstm_n': nrm((N_EVEN, DEC_BATCH, H_B, DH_B), 0.1),
        'state_mlstm_m': nrm((N_EVEN, DEC_BATCH, H_B), 0.5),
        'cache_diff_k': nrm((N_ODD, n_pool, PAGE_SIZE, H_C, 2 * DH_C)),
        'cache_diff_v': nrm((N_ODD, n_pool, PAGE_SIZE, H_C, 2 * DH_C)),
        'cache_sb_k': nrm((N_ODD, n_pool, PAGE_SIZE, H_D, DH_D)),
        'cache_sb_v': nrm((N_ODD, n_pool, PAGE_SIZE, H_D, DH_D)),
        'page_table': page_table,
        'ffn_norm1': gain((DEPTH, D_MODEL)),
        'ffn1_wi': nrm((DEPTH, D_MODEL, 2 * D_FF), D_MODEL ** -0.5),
        'ffn1_wo': nrm((DEPTH, D_FF, D_MODEL), D_FF ** -0.5),
        'mix_norm': gain((DEPTH, D_MODEL)),
        'ffn_norm2': gain((DEPTH, D_MODEL)),
        'ffn2_wi': nrm((DEPTH, D_MODEL, 2 * D_FF), D_MODEL ** -0.5),
        'ffn2_wo': nrm((DEPTH, D_FF, D_MODEL), D_FF ** -0.5),
        'ple_norm': gain((DEPTH, D_MODEL)),
        'ple_wg': nrm((DEPTH, D_MODEL, D_MODEL), D_MODEL ** -0.5),
        'ple_wp': nrm((DEPTH, PLE_DIM, D_MODEL), PLE_DIM ** -0.5),
        'ev_w_in': nrm((N_EVEN, D_MODEL, IN_EVEN), D_MODEL ** -0.5),
        'ev_conv_w': nrm((N_EVEN, CONV_W, D_A), CONV_W ** -0.5),
        'ev_conv_b': nrm((N_EVEN, D_A), 0.02),
        'ev_ln_g': gain((N_EVEN, D_A)),
        'ev_ln_b': nrm((N_EVEN, D_A), 0.02),
        'ev_gate_b': jnp.concatenate([nrm((N_EVEN, H_B), 0.1), 3.0 + nrm((N_EVEN, H_B), 0.1)], axis=-1),
        'ev_w_out': nrm((N_EVEN, D_A + D_B, D_MODEL), (D_A + D_B) ** -0.5),
        'od_w_in': nrm((N_ODD, D_MODEL, IN_ODD), D_MODEL ** -0.5),
        'od_lambda': nrm((N_ODD, 4, DH_C), 0.1),
        'od_head_g': gain((N_ODD, D_C)),
        'od_w_out': nrm((N_ODD, D_C + D_D, D_MODEL), (D_C + D_D) ** -0.5),
        'rel_bias': nrm((N_BUCKETS, H_C), 0.3),
        'final_norm': gain((D_MODEL,)),
    }


def reference(x_prompt, x_sample, p_prompt, p_sample, state_conv, state_mlstm_C, state_mlstm_n, state_mlstm_m,
              cache_diff_k, cache_diff_v, cache_sb_k, cache_sb_v, page_table,
              ffn_norm1, ffn1_wi, ffn1_wo, mix_norm, ffn_norm2, ffn2_wi, ffn2_wo, ple_norm, ple_wg, ple_wp,
              ev_w_in, ev_conv_w, ev_conv_b, ev_ln_g, ev_ln_b, ev_gate_b, ev_w_out,
              od_w_in, od_lambda, od_head_g, od_w_out, rel_bias, final_norm):
    W = dict(ffn_norm1=ffn_norm1, ffn1_wi=ffn1_wi, ffn1_wo=ffn1_wo, mix_norm=mix_norm,
             ffn_norm2=ffn_norm2, ffn2_wi=ffn2_wi, ffn2_wo=ffn2_wo,
             ple_norm=ple_norm, ple_wg=ple_wg, ple_wp=ple_wp,
             ev_w_in=ev_w_in, ev_conv_w=ev_conv_w, ev_conv_b=ev_conv_b, ev_ln_g=ev_ln_g, ev_ln_b=ev_ln_b,
             ev_gate_b=ev_gate_b, ev_w_out=ev_w_out,
             od_w_in=od_w_in, od_lambda=od_lambda, od_head_g=od_head_g, od_w_out=od_w_out,
             rel_bias=rel_bias, final_norm=final_norm)
    bp = x_prompt.shape[0]
    even_p = [(jnp.zeros((bp, CONV_W - 1, D_A), x_prompt.dtype),
               jnp.zeros((bp, H_B, DH_B, DH_B), jnp.float32),
               jnp.zeros((bp, H_B, DH_B), jnp.float32),
               jnp.zeros((bp, H_B), jnp.float32)) for _ in range(N_EVEN)]
    y_prompt, ev_p, od_p = trunk(x_prompt, p_prompt, even_p, [None] * N_ODD, W)
    even_s = [(state_conv[j], state_mlstm_C[j], state_mlstm_n[j], state_mlstm_m[j]) for j in range(N_EVEN)]
    odd_s = [tuple(gather_pages(c[j], page_table) for c in (cache_diff_k, cache_diff_v, cache_sb_k, cache_sb_v))
             for j in range(N_ODD)]
    y_sample, ev_s, od_s = trunk(x_sample, p_sample, even_s, odd_s, W)
    stack = lambda states, i: jnp.stack([s[i] for s in states])
    return (y_prompt, y_sample,
            stack(ev_p, 0), stack(ev_s, 0), stack(ev_p, 1), stack(ev_s, 1),
            stack(ev_p, 2), stack(ev_s, 2), stack(ev_p, 3), stack(ev_s, 3),
            stack(od_p, 0), stack(od_s, 0), stack(od_p, 1), stack(od_s, 1),
            stack(od_p, 2), stack(od_s, 2), stack(od_p, 3), stack(od_s, 3))
```

```python
import functools
import math

import jax
import jax.numpy as jnp
from jax import lax
from jax.experimental import pallas as pl
from jax.experimental.pallas import tpu as pltpu

F32 = jnp.float32
BF16 = jnp.bfloat16

LANES = 128
SUBLANES = 8
VMEM_LIMIT_BYTES = 56 * 1024 * 1024

D_MODEL = 1024
D_FF = 2816
HALF = 512
CONV_W = 31
H_B, DH_B = 4, 128
H_C, DH_C = 4, 64
H_D, DH_D = 8, 64
N_BUCKETS = 32
MAX_DIST = 128
MLSTM_CHUNK = 128
NEG_INF = -1e30
SB_DEAD = -104.0

ROW_TILE = 512
FF_TILE = 1408
DIFF_BLK = 256
SB_BLK = 128
CONV_ROWS = 256
CONV_SUB = 32
DEC_PAGES = 4


def _cp(*sem):
    return pltpu.CompilerParams(dimension_semantics=sem, vmem_limit_bytes=VMEM_LIMIT_BYTES)


def _rms(x, g, eps=1e-6):
    return x * lax.rsqrt(jnp.mean(x * x, axis=-1, keepdims=True) + eps) * g


def _dot(a, b):
    return jnp.dot(a, b, preferred_element_type=F32)


def _dot_nt(a, b):
    return lax.dot_general(a, b, (((1,), (1,)), ((), ())), preferred_element_type=F32)


def _dot_tn(a, b):
    return lax.dot_general(a, b, (((0,), (0,)), ((), ())), preferred_element_type=F32)


def _log_sigmoid_neg(z):
    return -(jnp.maximum(z, 0.0) + jnp.log1p(jnp.exp(-jnp.abs(z))))


def _split_bf16(x):
    hi = x.astype(BF16)
    lo = (x - hi.astype(F32)).astype(BF16)
    return hi, lo


def _ffn_kernel(x_ref, g_ref, wig_ref, wiu_ref, wo_ref, o_ref, hn_ref, acc_ref):
    j = pl.program_id(1)

    @pl.when(j == 0)
    def _():
        hn_ref[...] = _rms(x_ref[...], g_ref[...]).astype(BF16)
        acc_ref[...] = jnp.zeros_like(acc_ref)

    hn = hn_ref[...]
    gate = _dot(hn, wig_ref[...])
    up = _dot(hn, wiu_ref[...])
    act = (gate * jax.nn.sigmoid(gate) * up).astype(BF16)
    acc_ref[...] += _dot(act, wo_ref[...])

    @pl.when(j == pl.num_programs(1) - 1)
    def _():
        o_ref[...] = x_ref[...] + 0.5 * acc_ref[...]


def _ffn_half(h, g, wi, wo):
    m, d = h.shape
    tm = min(ROW_TILE, m)
    nj = D_FF // FF_TILE
    return pl.pallas_call(
        _ffn_kernel,
        grid=(m // tm, nj),
        in_specs=[
            pl.BlockSpec((tm, d), lambda i, j: (i, 0)),
            pl.BlockSpec((1, d), lambda i, j: (0, 0)),
            pl.BlockSpec((d, FF_TILE), lambda i, j: (0, j)),
            pl.BlockSpec((d, FF_TILE), lambda i, j: (0, j + D_FF // FF_TILE)),
            pl.BlockSpec((FF_TILE, d), lambda i, j: (j, 0)),
        ],
        out_specs=pl.BlockSpec((tm, d), lambda i, j: (i, 0)),
        out_shape=jax.ShapeDtypeStruct((m, d), F32),
        scratch_shapes=[pltpu.VMEM((tm, d), BF16), pltpu.VMEM((tm, d), F32)],
        compiler_params=_cp("parallel", "arbitrary"),
    )(h, g.reshape(1, d), wi, wi, wo)


def _ple_kernel(x_ref, g_ref, wg_ref, p_ref, wp_ref, fg_ref, o_ref, *, final):
    x = x_ref[...]
    hn = _rms(x, g_ref[...]).astype(BF16)
    gate = jax.nn.sigmoid(_dot(hn, wg_ref[...]))
    h = x + gate * _dot(p_ref[...].astype(BF16), wp_ref[...])
    if final:
        h = _rms(h, fg_ref[...])
    o_ref[...] = h


def _ple(h, g, wg, p, wp, fg, final):
    m, d = h.shape
    tm = min(ROW_TILE, m)
    pd = p.shape[1]
    return pl.pallas_call(
        functools.partial(_ple_kernel, final=final),
        grid=(m // tm,),
        in_specs=[
            pl.BlockSpec((tm, d), lambda i: (i, 0)),
            pl.BlockSpec((1, d), lambda i: (0, 0)),
            pl.BlockSpec((d, d), lambda i: (0, 0)),
            pl.BlockSpec((tm, pd), lambda i: (i, 0)),
            pl.BlockSpec((pd, d), lambda i: (0, 0)),
            pl.BlockSpec((1, d), lambda i: (0, 0)),
        ],
        out_specs=pl.BlockSpec((tm, d), lambda i: (i, 0)),
        out_shape=jax.ShapeDtypeStruct((m, d), F32),
        compiler_params=_cp("parallel"),
    )(h, g.reshape(1, d), wg, p, wp, fg.reshape(1, d))


def _inproj_even_kernel(x_ref, g_ref, w_ref, wgt_ref, u_ref, q_ref, k_ref, v_ref, o_ref, gt_ref):
    hn = _rms(x_ref[...], g_ref[...]).astype(BF16)

    def col(c):
        return _dot(hn, w_ref[:, c * HALF:(c + 1) * HALF])

    u_ref[...] = col(0) * jax.nn.sigmoid(col(1))
    q_ref[...] = col(2).astype(q_ref.dtype)
    k_ref[...] = (col(3) * (DH_B ** -0.5)).astype(k_ref.dtype)
    v_ref[...] = col(4).astype(v_ref.dtype)
    o_ref[...] = col(5)
    gt_ref[...] = _dot_nt(wgt_ref[...], hn)


def _inproj_even(h, g, w, wgt, qkv_dtype):
    m, d = h.shape
    tm = min(ROW_TILE, m)
    row = lambda i: (i, 0)
    out = lambda dt: jax.ShapeDtypeStruct((m, HALF), dt)
    return pl.pallas_call(
        _inproj_even_kernel,
        grid=(m // tm,),
        in_specs=[
            pl.BlockSpec((tm, d), row),
            pl.BlockSpec((1, d), lambda i: (0, 0)),
            pl.BlockSpec((d, 6 * HALF), lambda i: (0, 0)),
            pl.BlockSpec((2 * H_B, d), lambda i: (0, 0)),
        ],
        out_specs=[pl.BlockSpec((tm, HALF), row)] * 5 + [pl.BlockSpec((2 * H_B, tm), lambda i: (0, i))],
        out_shape=[out(F32), out(qkv_dtype), out(qkv_dtype), out(qkv_dtype), out(F32),
                   jax.ShapeDtypeStruct((2 * H_B, m), F32)],
        compiler_params=_cp("parallel"),
    )(h, g.reshape(1, d), w, wgt)


def _inproj_odd_kernel(x_ref, g_ref, w_ref, cq_ref, ck_ref, cv_ref, sq_ref, sk_ref, sv_ref):
    hn = _rms(x_ref[...], g_ref[...]).astype(BF16)

    def col(c):
        return _dot(hn, w_ref[:, c * HALF:(c + 1) * HALF])

    cq_ref[...] = (col(0) * (DH_C ** -0.5)).astype(cq_ref.dtype)
    ck_ref[...] = col(1)
    cv_ref[...] = col(2)
    sq_ref[...] = (col(3) * (DH_D ** -0.5)).astype(sq_ref.dtype)
    sk_ref[...] = col(4)
    sv_ref[...] = col(5)


def _inproj_odd(h, g, w, q_dtype):
    m, d = h.shape
    tm = min(ROW_TILE, m)
    row = lambda i: (i, 0)
    out = lambda dt: jax.ShapeDtypeStruct((m, HALF), dt)
    return pl.pallas_call(
        _inproj_odd_kernel,
        grid=(m // tm,),
        in_specs=[
            pl.BlockSpec((tm, d), row),
            pl.BlockSpec((1, d), lambda i: (0, 0)),
            pl.BlockSpec((d, 6 * HALF), lambda i: (0, 0)),
        ],
        out_specs=[pl.BlockSpec((tm, HALF), row)] * 6,
        out_shape=[out(q_dtype), out(F32), out(F32), out(q_dtype), out(F32), out(F32)],
        compiler_params=_cp("parallel"),
    )(h, g.reshape(1, d), w)


def _outproj_kernel(x_ref, a_ref, b_ref, wa_ref, wb_ref, o_ref):
    o_ref[...] = (x_ref[...] + _dot(a_ref[...].astype(BF16), wa_ref[...])
                  + _dot(b_ref[...].astype(BF16), wb_ref[...]))


def _outproj(h, a, b, w):
    m, d = h.shape
    tm = min(ROW_TILE, m)
    row = lambda i: (i, 0)
    return pl.pallas_call(
        _outproj_kernel,
        grid=(m // tm,),
        in_specs=[
            pl.BlockSpec((tm, d), row),
            pl.BlockSpec((tm, HALF), row),
            pl.BlockSpec((tm, HALF), row),
            pl.BlockSpec((HALF, d), lambda i: (0, 0)),
            pl.BlockSpec((HALF, d), lambda i: (1, 0)),
        ],
        out_specs=pl.BlockSpec((tm, d), row),
        out_shape=jax.ShapeDtypeStruct((m, d), F32),
        compiler_params=_cp("parallel"),
    )(h, a, b, w, w)


_HIST = 32


def _conv_kernel(u_ref, buf_ref, taps_ref, cb_ref, lg_ref, lb_ref, a_ref, st_ref, win_ref, *, tt, sub):
    t = pl.program_id(1)
    pad = _HIST - (CONV_W - 1)

    @pl.when(t == 0)
    def _():
        win_ref[0:SUBLANES, :] = jnp.zeros((SUBLANES, HALF), F32)
        win_ref[pad:_HIST, :] = buf_ref[0]

    win_ref[_HIST:_HIST + tt, :] = u_ref[0]
    for rb in range(tt // sub):
        acc = jnp.zeros((sub, HALF), F32)
        for w in range(CONV_W):
            acc = acc + win_ref[pl.ds(rb * sub + pad + w, sub), :] * taps_ref[w:w + 1, :]
        c = acc + cb_ref[...]
        mu = jnp.mean(c, axis=-1, keepdims=True)
        var = jnp.mean(jnp.square(c - mu), axis=-1, keepdims=True)
        cn = (c - mu) * lax.rsqrt(var + 1e-5) * lg_ref[...] + lb_ref[...]
        a_ref[0, rb * sub:(rb + 1) * sub, :] = (cn * jax.nn.sigmoid(cn)).astype(a_ref.dtype)

    @pl.when(t == pl.num_programs(1) - 1)
    def _():
        st_ref[0] = win_ref[tt + pad:tt + _HIST, :]

    win_ref[0:_HIST, :] = win_ref[tt:tt + _HIST, :]


def _conv_module(u, buf, taps, cb, lg, lb, out_dtype):
    b, t, _ = u.shape
    tt = min(CONV_ROWS, t)
    sub = min(CONV_SUB, tt)
    vec = lambda: pl.BlockSpec((1, HALF), lambda i, j: (0, 0))
    return pl.pallas_call(
        functools.partial(_conv_kernel, tt=tt, sub=sub),
        grid=(b, t // tt),
        in_specs=[
            pl.BlockSpec((1, tt, HALF), lambda i, j: (i, j, 0)),
            pl.BlockSpec((1, CONV_W - 1, HALF), lambda i, j: (i, 0, 0)),
            pl.BlockSpec((CONV_W, HALF), lambda i, j: (0, 0)),
            vec(), vec(), vec(),
        ],
        out_specs=[
            pl.BlockSpec((1, tt, HALF), lambda i, j: (i, j, 0)),
            pl.BlockSpec((1, CONV_W - 1, HALF), lambda i, j: (i, 0, 0)),
        ],
        out_shape=[jax.ShapeDtypeStruct((b, t, HALF), out_dtype),
                   jax.ShapeDtypeStruct((b, CONV_W - 1, HALF), F32)],
        scratch_shapes=[pltpu.VMEM((_HIST + tt, HALF), F32)],
        compiler_params=_cp("parallel", "arbitrary"),
    )(u, buf, taps, cb.reshape(1, HALF), lg.reshape(1, HALF), lb.reshape(1, HALF))


def _cumsum_lanes(x):
    lane = lax.broadcasted_iota(jnp.int32, x.shape, 1)
    sh = 1
    while sh < x.shape[1]:
        x = x + jnp.where(lane >= sh, pltpu.roll(x, sh, 1), 0.0)
        sh *= 2
    return x


def _mlstm_kernel(q_ref, k_ref, v_ref, o_ref, g_ref, gb_ref, c0_ref, n0_ref, m0_ref,
                  h_ref, c_ref, n_ref, m_ref, *, lr):
    L = MLSTM_CHUNK

    @pl.when(pl.program_id(1) == 0)
    def _():
        c_ref[...] = c0_ref[...]
        n_ref[...] = n0_ref[...]
        m_ref[...] = m0_ref[...]

    def rows(ref):
        x = ref[0]
        if lr < L:
            x = jnp.concatenate([x.astype(F32), jnp.zeros((L - lr, HALF), F32)], axis=0)
        return x.astype(BF16)

    q, k, v = rows(q_ref), rows(k_ref), rows(v_ref)
    g = g_ref[0] + gb_ref[...]
    row = lax.broadcasted_iota(jnp.int32, g.shape, 0)
    steps = jnp.where(row < H_B, 0.0, _log_sigmoid_neg(-g))
    bcum = _cumsum_lanes(steps)
    b4 = bcum[H_B:2 * H_B]
    a4 = g[0:H_B] - b4
    packed = jnp.concatenate([a4, b4, jnp.zeros((L - 2 * H_B, L), F32)], axis=0)
    cols = jnp.transpose(packed)
    tpos = lax.broadcasted_iota(jnp.int32, (L, L), 0)
    spos = lax.broadcasted_iota(jnp.int32, (L, L), 1)
    causal = spos <= tpos
    outs = []
    for h in range(H_B):
        sl = slice(h * DH_B, (h + 1) * DH_B)
        qh, kh, vh = q[:, sl], k[:, sl], v[:, sl]
        a_row = a4[h:h + 1, :]
        a_col = cols[:, h:h + 1]
        b_col = cols[:, H_B + h:H_B + h + 1]
        b_last = b4[h:h + 1, L - 1:L]
        m_old = m_ref[0, h:h + 1, 0:1]
        c_old = c_ref[0, h]
        n_old = n_ref[0, h:h + 1, :]
        dmat = jnp.where(causal, b_col + a_row, NEG_INF)
        inter = b_col + m_old
        mt = jnp.maximum(inter, jnp.max(dmat, axis=-1, keepdims=True))
        w_inter = jnp.exp(inter - mt)
        s = _dot_nt(qh, kh) * jnp.exp(dmat - mt)
        num = w_inter * _dot(qh, c_old.astype(BF16)) + _dot(s.astype(BF16), vh)
        den = (w_inter * jnp.sum(qh.astype(F32) * n_old, axis=-1, keepdims=True)
               + jnp.sum(s, axis=-1, keepdims=True))
        outs.append(num / jnp.maximum(jnp.abs(den), jnp.exp(-mt)))
        m_new = mt[L - 1:L, :]
        decay = jnp.exp(b_last + m_old - m_new)
        kw = kh.astype(F32) * jnp.exp(a_col + b_last - m_new)
        c_ref[0, h] = decay * c_old + _dot_tn(kw.astype(BF16), vh)
        n_ref[0, h:h + 1, :] = decay * n_old + jnp.sum(kw, axis=0, keepdims=True)
        m_ref[0, h:h + 1, :] = jnp.broadcast_to(m_new, (1, LANES))
    hs = jnp.concatenate(outs, axis=1)
    h_ref[0] = (jax.nn.sigmoid(o_ref[0]) * hs[0:lr]).astype(h_ref.dtype)


def _mlstm(q, k, v, o, gt, gate_b, c0, n0, m0, out_dtype):
    b, t, _ = q.shape
    L = MLSTM_CHUNK
    lr = min(L, t)
    nc = t // lr
    m0b = jnp.broadcast_to(jnp.pad(m0, ((0, 0), (0, SUBLANES - H_B)))[:, :, None], (b, SUBLANES, LANES))
    gb = jnp.broadcast_to(gate_b.reshape(2 * H_B, 1), (2 * H_B, L))
    blk = pl.BlockSpec((1, lr, HALF), lambda i, j: (i, j, 0))
    st = lambda *s: pl.BlockSpec((1,) + s, lambda i, j: (i,) + (0,) * len(s))
    h, c1, n1, m1 = pl.pallas_call(
        functools.partial(_mlstm_kernel, lr=lr),
        grid=(b, nc),
        in_specs=[blk, blk, blk, blk,
                  pl.BlockSpec((1, 2 * H_B, L), lambda i, j: (i, 0, j)),
                  pl.BlockSpec((2 * H_B, L), lambda i, j: (0, 0)),
                  st(H_B, DH_B, DH_B), st(H_B, DH_B), st(SUBLANES, LANES)],
        out_specs=[blk, st(H_B, DH_B, DH_B), st(H_B, DH_B), st(SUBLANES, LANES)],
        out_shape=[jax.ShapeDtypeStruct((b, t, HALF), out_dtype),
                   jax.ShapeDtypeStruct((b, H_B, DH_B, DH_B), F32),
                   jax.ShapeDtypeStruct((b, H_B, DH_B), F32),
                   jax.ShapeDtypeStruct((b, SUBLANES, LANES), F32)],
        compiler_params=_cp("parallel", "arbitrary"),
    )(q, k, v, o, gt, gb, c0, n0, m0b)
    return h, c1, n1, m1[:, :H_B, 0]


def _t5_bucket(dist):
    n = jnp.maximum(dist, 0)
    exact = N_BUCKETS // 2
    nf = jnp.maximum(n, 1).astype(F32)
    large = exact + (jnp.log(nf / exact) / math.log(MAX_DIST / exact) * (N_BUCKETS - exact)).astype(jnp.int32)
    return jnp.where(n < exact, n, jnp.minimum(large, N_BUCKETS - 1))


def _bias_of_dist(rel_bias, dist):
    b = jnp.moveaxis(rel_bias.astype(F32)[_t5_bucket(dist)], -1, 0)
    return jnp.where(dist >= 0, b, NEG_INF)


def _lambda(lp_ref, lam_init):
    lp = lp_ref[...]
    s1 = jnp.sum(lp[0:1] * lp[1:2], axis=-1, keepdims=True)
    s2 = jnp.sum(lp[2:3] * lp[3:4], axis=-1, keepdims=True)
    return jnp.exp(s1) - jnp.exp(s2) + lam_init


def _head_norm(x, hg, lam_init):
    return x * lax.rsqrt(jnp.mean(x * x, axis=-1, keepdims=True) + 1e-6) * hg * (1.0 - lam_init)


def _diff_attn_kernel(q_ref, k_ref, v_ref, bd_ref, bp_ref, bf_ref, lp_ref, hg_ref, o_ref, kb_ref, vb_ref,
                      *, lam_init):
    blk = DIFF_BLK
    i = pl.program_id(2)

    @pl.when(i == 0)
    def _():
        kb_ref[...] = k_ref[...].astype(BF16)
        vb_ref[...] = v_ref[...].astype(BF16)

    q = q_ref[...]
    lane = lax.broadcasted_iota(jnp.int32, q.shape, 1)
    zero = jnp.zeros_like(q)
    q2 = jnp.concatenate([jnp.where(lane < DH_C, q, zero), jnp.where(lane >= DH_C, q, zero)], axis=0)

    def tile(j, bias):
        off = pl.multiple_of(j * blk, blk)
        s = _dot_nt(q2, kb_ref[pl.ds(off, blk), :]) + bias
        return s, vb_ref[pl.ds(off, blk), :]

    def update(carry, s, vblk):
        m, l, acc = carry
        mn = jnp.maximum(m, jnp.max(s, axis=-1, keepdims=True))
        p = jnp.exp(s - mn)
        al = jnp.exp(m - mn)
        return mn, al * l + jnp.sum(p, axis=-1, keepdims=True), al * acc + _dot(p.astype(BF16), vblk)

    s, vblk = tile(i, bd_ref[0])
    m = jnp.max(s, axis=-1, keepdims=True)
    p = jnp.exp(s - m)
    carry = (m, jnp.sum(p, axis=-1, keepdims=True), _dot(p.astype(BF16), vblk))

    def prev_step(c):
        return update(c, *tile(i - 1, bp_ref[0]))

    carry = lax.cond(i >= 1, prev_step, lambda c: c, carry)

    def far_step(j, c):
        return update(c, *tile(j, bf_ref[0]))

    m, l, acc = lax.fori_loop(0, jnp.maximum(i - 1, 0), far_step, carry)
    o = acc / l
    o = o[0:blk] - _lambda(lp_ref, lam_init) * o[blk:2 * blk]
    o_ref[...] = _head_norm(o, hg_ref[0], lam_init).astype(o_ref.dtype)


def _diff_attention(cq, ck, cv, b, t, rel_bias, lam_p, head_g, lam_init):
    blk = DIFF_BLK
    nq = t // blk
    r = jnp.arange(blk, dtype=jnp.int32)
    d0 = r[:, None] - r[None, :]
    bd = jnp.tile(_bias_of_dist(rel_bias, d0), (1, 2, 1))
    bp = jnp.tile(_bias_of_dist(rel_bias, d0 + blk), (1, 2, 1))
    assert blk + 1 >= MAX_DIST
    bfar = _bias_of_dist(rel_bias, jnp.full((1, blk), 2 * blk, jnp.int32))
    per_head = lambda *s: pl.BlockSpec((1,) + s, lambda bi, h, i: (h, 0, 0))
    return pl.pallas_call(
        functools.partial(_diff_attn_kernel, lam_init=lam_init),
        grid=(b, H_C, nq),
        in_specs=[
            pl.BlockSpec((blk, LANES), lambda bi, h, i: (bi * nq + i, h)),
            pl.BlockSpec((t, LANES), lambda bi, h, i: (bi, h)),
            pl.BlockSpec((t, LANES), lambda bi, h, i: (bi, h)),
            per_head(2 * blk, blk), per_head(2 * blk, blk), per_head(1, blk),
            pl.BlockSpec((4, DH_C), lambda bi, h, i: (0, 0)),
            per_head(1, LANES),
        ],
        out_specs=pl.BlockSpec((blk, LANES), lambda bi, h, i: (bi * nq + i, h)),
        out_shape=jax.ShapeDtypeStruct((b * t, HALF), BF16),
        scratch_shapes=[pltpu.VMEM((t, LANES), BF16), pltpu.VMEM((t, LANES), BF16)],
        compiler_params=_cp("parallel", "parallel", "arbitrary"),
    )(cq, ck, cv, bd, bp, bfar, lam_p, head_g.reshape(H_C, 1, LANES))


def _sb_tile(q2, kblk, vblk, upper, r, mask):
    z = _dot_nt(q2, kblk)
    lk = _log_sigmoid_neg(z)
    if mask is not None:
        lk = jnp.where(mask, lk, 0.0)
    hi, lo = _split_bf16(lk)
    after = _dot(hi, upper) + _dot(lo, upper) + r
    w = jnp.exp(lk + z + after)
    if mask is not None:
        w = jnp.where(mask, w, 0.0)
    return _dot(w.astype(BF16), vblk), r + jnp.sum(lk, axis=-1, keepdims=True)


def _strict_upper(n):
    j = lax.broadcasted_iota(jnp.int32, (n, n), 0)
    s = lax.broadcasted_iota(jnp.int32, (n, n), 1)
    return jnp.where(j > s, 1.0, 0.0).astype(BF16)


def _sb_attn_kernel(q_ref, k_ref, v_ref, o_ref, kb_ref, vb_ref):
    blk = SB_BLK
    i = pl.program_id(2)

    @pl.when(i == 0)
    def _():
        kb_ref[...] = k_ref[...].astype(BF16)
        vb_ref[...] = v_ref[...].astype(BF16)

    q = q_ref[...]
    lane = lax.broadcasted_iota(jnp.int32, q.shape, 1)
    zero = jnp.zeros_like(q)
    q2 = jnp.concatenate([jnp.where(lane < DH_D, q, zero), jnp.where(lane >= DH_D, q, zero)], axis=0)
    upper = _strict_upper(blk)
    tpos = lax.broadcasted_iota(jnp.int32, (2 * blk, blk), 0) % blk
    spos = lax.broadcasted_iota(jnp.int32, (2 * blk, blk), 1)

    def tile(j, r, mask):
        off = pl.multiple_of(j * blk, blk)
        return _sb_tile(q2, kb_ref[pl.ds(off, blk), :], vb_ref[pl.ds(off, blk), :], upper, r, mask)

    acc, r = tile(i, jnp.zeros((2 * blk, 1), F32), spos < tpos)

    def cond(c):
        j, _, r = c
        return jnp.logical_and(j >= 0, jnp.max(r) > SB_DEAD)

    def body(c):
        j, acc, r = c
        pv, r = tile(j, r, None)
        return j - 1, acc + pv, r

    _, acc, _ = lax.while_loop(cond, body, (i - 1, acc, r))
    o_ref[...] = jnp.where(lane < DH_D, acc[0:blk], acc[blk:2 * blk]).astype(o_ref.dtype)


def _sb_attention(sq, sk, sv, b, t):
    blk = SB_BLK
    nq = t // blk
    return pl.pallas_call(
        _sb_attn_kernel,
        grid=(b, HALF // LANES, nq),
        in_specs=[
            pl.BlockSpec((blk, LANES), lambda bi, h, i: (bi * nq + i, h)),
            pl.BlockSpec((t, LANES), lambda bi, h, i: (bi, h)),
            pl.BlockSpec((t, LANES), lambda bi, h, i: (bi, h)),
        ],
        out_specs=pl.BlockSpec((blk, LANES), lambda bi, h, i: (bi * nq + i, h)),
        out_shape=jax.ShapeDtypeStruct((b * t, HALF), BF16),
        scratch_shapes=[pltpu.VMEM((t, LANES), BF16), pltpu.VMEM((t, LANES), BF16)],
        compiler_params=_cp("parallel", "parallel", "arbitrary"),
    )(sq, sk, sv)


_DEC_ROWS = 64


def _dec_attn_kernel(pt_ref, cq_ref, sq_ref, ck_ref, cv_ref, sk_ref, sv_ref, bias_ref, bnew_ref, lp_ref, hg_ref,
                     *rest, ts, lam_init):
    np_ = DEC_PAGES
    pages = rest[:4 * np_]
    oc_ref, os_ref, qd_ref, qs_ref, m_ref, l_ref, r_ref, accd_ref, accs_ref = rest[4 * np_:]
    g = pl.program_id(1)
    psz = pages[0].shape[1]
    upper = _strict_upper(psz)

    def diff_tile(kblk, vblk, bias):
        s = _dot_nt(qd_ref[...], kblk) + bias
        m = m_ref[...]
        mn = jnp.maximum(m, jnp.max(s, axis=-1, keepdims=True))
        p = jnp.exp(s - mn)
        al = jnp.exp(m - mn)
        m_ref[...] = mn
        l_ref[...] = al * l_ref[...] + jnp.sum(p, axis=-1, keepdims=True)
        accd_ref[...] = al * accd_ref[...] + _dot(p.astype(BF16), vblk)

    def sb_tile(kblk, vblk, mask):
        pv, r = _sb_tile(qs_ref[...], kblk, vblk, upper, r_ref[...], mask)
        accs_ref[...] += pv
        r_ref[...] = r

    def pad_rows(x):
        return jnp.concatenate([x, jnp.zeros((psz - ts, HALF), F32)], axis=0).astype(BF16)

    @pl.when(g == 0)
    def _():
        lane = lax.broadcasted_iota(jnp.int32, (ts, HALF), 1)
        cq = cq_ref[0]
        sq = sq_ref[0]
        qd_ref[...] = jnp.concatenate(
            [jnp.where((lane >= c * DH_C) & (lane < (c + 1) * DH_C), cq, 0.0) for c in range(2 * H_C)],
            axis=0).astype(BF16)
        qs_ref[...] = jnp.concatenate(
            [jnp.where((lane >= c * DH_D) & (lane < (c + 1) * DH_D), sq, 0.0) for c in range(H_D)],
            axis=0).astype(BF16)
        m_ref[...] = jnp.full_like(m_ref, NEG_INF)
        l_ref[...] = jnp.zeros_like(l_ref)
        r_ref[...] = jnp.zeros_like(r_ref)
        accd_ref[...] = jnp.zeros_like(accd_ref)
        accs_ref[...] = jnp.zeros_like(accs_ref)
        diff_tile(pad_rows(ck_ref[0]), pad_rows(cv_ref[0]), bnew_ref[...])
        tq = lax.broadcasted_iota(jnp.int32, (_DEC_ROWS, psz), 0) % ts
        sk_pos = lax.broadcasted_iota(jnp.int32, (_DEC_ROWS, psz), 1)
        sb_tile(pad_rows(sk_ref[0]), pad_rows(sv_ref[0]), sk_pos < tq)

    for p in range(np_):
        dk, dv, sk, sv = pages[4 * p:4 * p + 4]
        diff_tile(dk[0].astype(BF16), dv[0].astype(BF16), bias_ref[p])
        sb_tile(sk[0].astype(BF16), sv[0].astype(BF16), None)

    @pl.when(g == pl.num_programs(1) - 1)
    def _():
        lam = _lambda(lp_ref, lam_init)
        o = accd_ref[...] / l_ref[...]
        for h in range(H_C):
            sl = slice(h * 2 * DH_C, (h + 1) * 2 * DH_C)
            r0 = h * 2 * ts
            oh = o[r0:r0 + ts, sl] - lam * o[r0 + ts:r0 + 2 * ts, sl]
            oc_ref[0, :, sl] = _head_norm(oh, hg_ref[h:h + 1, :], lam_init)
        acc = accs_ref[...]
        lane = lax.broadcasted_iota(jnp.int32, (ts, LANES), 1)
        for pr in range(H_D // 2):
            sl = slice(pr * LANES, (pr + 1) * LANES)
            r0 = pr * 2 * ts
            os_ref[0, :, sl] = jnp.where(lane < DH_D, acc[r0:r0 + ts, sl], acc[r0 + ts:r0 + 2 * ts, sl])


def _decode_attention(cq, sq, new_rows, caches, page_table, rel_bias, lam_p, head_g, lam_init):
    b, ts, _ = cq.shape
    n_pages = page_table.shape[1]
    psz = caches[0].shape[1]
    past = n_pages * psz
    np_ = DEC_PAGES
    assert n_pages % np_ == 0 and _DEC_ROWS == 2 * H_C * ts == H_D * ts
    tq = jnp.arange(ts, dtype=jnp.int32)
    kpos = (jnp.arange(n_pages - 1, -1, -1, dtype=jnp.int32)[:, None] * psz
            + jnp.arange(psz, dtype=jnp.int32)[None, :])
    dist = past + tq[None, :, None] - kpos[:, None, :]
    bias = _bias_of_dist(rel_bias, dist)
    bias = jnp.broadcast_to(bias[:, None], (H_C, 2, n_pages, ts, psz))
    bias = jnp.moveaxis(bias, 2, 0).reshape(n_pages, _DEC_ROWS, psz)
    dnew = jnp.where(jnp.arange(psz)[None, :] < ts, tq[:, None] - jnp.arange(psz, dtype=jnp.int32)[None, :], -1)
    bnew = _bias_of_dist(rel_bias, dnew)
    bnew = jnp.broadcast_to(bnew[:, None], (H_C, 2, ts, psz)).reshape(_DEC_ROWS, psz)

    row = pl.BlockSpec((1, ts, HALF), lambda bi, g, pt: (bi, 0, 0))

    def page_spec(p):
        return pl.BlockSpec((1, psz, HALF), lambda bi, g, pt: (pt[bi * n_pages + n_pages - 1 - (g * np_ + p)], 0, 0))

    page_specs, page_args = [], []
    for p in range(np_):
        for c in caches:
            page_specs.append(page_spec(p))
            page_args.append(c)
    oc, os_ = pl.pallas_call(
        functools.partial(_dec_attn_kernel, ts=ts, lam_init=lam_init),
        grid_spec=pltpu.PrefetchScalarGridSpec(
            num_scalar_prefetch=1,
            grid=(b, n_pages // np_),
            in_specs=[row] * 6 + [
                pl.BlockSpec((np_, _DEC_ROWS, psz), lambda bi, g, pt: (g, 0, 0)),
                pl.BlockSpec((_DEC_ROWS, psz), lambda bi, g, pt: (0, 0)),
                pl.BlockSpec((4, DH_C), lambda bi, g, pt: (0, 0)),
                pl.BlockSpec((H_C, LANES), lambda bi, g, pt: (0, 0)),
            ] + page_specs,
            out_specs=[row, row],
            scratch_shapes=[
                pltpu.VMEM((_DEC_ROWS, HALF), BF16), pltpu.VMEM((_DEC_ROWS, HALF), BF16),
                pltpu.VMEM((_DEC_ROWS, 1), F32), pltpu.VMEM((_DEC_ROWS, 1), F32), pltpu.VMEM((_DEC_ROWS, 1), F32),
                pltpu.VMEM((_DEC_ROWS, HALF), F32), pltpu.VMEM((_DEC_ROWS, HALF), F32),
            ],
        ),
        out_shape=[jax.ShapeDtypeStruct((b, ts, HALF), F32)] * 2,
        compiler_params=_cp("parallel", "arbitrary"),
    )(page_table.reshape(-1), cq, sq, *new_rows, bias, bnew, lam_p, head_g.reshape(H_C, LANES), *page_args)
    return oc, os_


def _trunk(x, p, even_states, odd_past, page_table, W):
    b, t, d = x.shape
    m = b * t
    prompt = odd_past is None
    act = BF16 if prompt else F32
    h = x.reshape(m, d)
    depth = p.shape[0]
    new_even, new_odd = [], []
    for l in range(depth):
        j = l // 2
        h = _ffn_half(h, W['ffn_norm1'][l], W['ffn1_wi'][l], W['ffn1_wo'][l])
        if l % 2 == 0:
            buf, c0, n0, m0 = even_states[j]
            u, q, k, v, o, gt = _inproj_even(h, W['mix_norm'][l], W['ev_w_in'][j], W['ev_w_gt'][j], act)
            a_out, buf1 = _conv_module(u.reshape(b, t, HALF), buf, W['ev_conv_w'][j], W['ev_conv_b'][j],
                                       W['ev_ln_g'][j], W['ev_ln_b'][j], act)
            gt = jnp.moveaxis(gt.reshape(2 * H_B, b, t), 1, 0)
            if t < MLSTM_CHUNK:
                padv = jnp.where(jnp.arange(2 * H_B) < H_B, NEG_INF, -NEG_INF).astype(F32)
                gt = jnp.concatenate(
                    [gt, jnp.broadcast_to(padv[None, :, None], (b, 2 * H_B, MLSTM_CHUNK - t))], axis=2)
            r3 = lambda a: a.reshape(b, t, HALF)
            b_out, c1, n1, m1 = _mlstm(r3(q), r3(k), r3(v), r3(o), gt, W['ev_gate_b'][j], c0, n0, m0, act)
            new_even.append((buf1, c1, n1, m1))
            h = _outproj(h, a_out.reshape(m, HALF), b_out.reshape(m, HALF), W['ev_w_out'][j])
        else:
            lam_init = 0.8 - 0.6 * math.exp(-0.3 * l)
            cq, ck, cv, sq, sk, sv = _inproj_odd(h, W['mix_norm'][l], W['od_w_in'][j], act)
            new_odd.append((ck, cv, sk, sv))
            if prompt:
                oc = _diff_attention(cq, ck, cv, b, t, W['rel_bias'], W['od_lambda'][j], W['od_head_g'][j], lam_init)
                os_ = _sb_attention(sq, sk, sv, b, t)
            else:
                r3 = lambda a: a.reshape(b, t, HALF)
                oc, os_ = _decode_attention(r3(cq), r3(sq), [r3(a) for a in (ck, cv, sk, sv)], odd_past[j],
                                            page_table[j], W['rel_bias'], W['od_lambda'][j], W['od_head_g'][j],
                                            lam_init)
            h = _outproj(h, oc.reshape(m, HALF), os_.reshape(m, HALF), W['od_w_out'][j])
        h = _ffn_half(h, W['ffn_norm2'][l], W['ffn2_wi'][l], W['ffn2_wo'][l])
        h = _ple(h, W['ple_norm'][l], W['ple_wg'][l], p[l].reshape(m, -1), W['ple_wp'][l], W['final_norm'],
                 final=(l == depth - 1))
    return h.reshape(b, t, d), new_even, new_odd


def kernel(x_prompt, x_sample, p_prompt, p_sample, state_conv, state_mlstm_C, state_mlstm_n, state_mlstm_m, cache_diff_k, cache_diff_v, cache_sb_k, cache_sb_v, page_table, ffn_norm1, ffn1_wi, ffn1_wo, mix_norm, ffn_norm2, ffn2_wi, ffn2_wo, ple_norm, ple_wg, ple_wp, ev_w_in, ev_conv_w, ev_conv_b, ev_ln_g, ev_ln_b, ev_gate_b, ev_w_out, od_w_in, od_lambda, od_head_g, od_w_out, rel_bias, final_norm):
    bf = lambda a: a.astype(BF16)
    n_even, n_odd = ev_w_in.shape[0], od_w_in.shape[0]
    W = dict(ffn_norm1=ffn_norm1, ffn1_wi=bf(ffn1_wi), ffn1_wo=bf(ffn1_wo), mix_norm=mix_norm,
             ffn_norm2=ffn_norm2, ffn2_wi=bf(ffn2_wi), ffn2_wo=bf(ffn2_wo),
             ple_norm=ple_norm, ple_wg=bf(ple_wg), ple_wp=bf(ple_wp),
             ev_w_in=bf(ev_w_in[:, :, :6 * HALF]), ev_w_gt=bf(jnp.swapaxes(ev_w_in[:, :, 6 * HALF:], 1, 2)),
             ev_conv_w=ev_conv_w, ev_conv_b=ev_conv_b, ev_ln_g=ev_ln_g, ev_ln_b=ev_ln_b,
             ev_gate_b=ev_gate_b, ev_w_out=bf(ev_w_out),
             od_w_in=bf(od_w_in), od_lambda=od_lambda, od_head_g=od_head_g, od_w_out=bf(od_w_out),
             rel_bias=rel_bias, final_norm=final_norm)
    bp, tp = x_prompt.shape[0], x_prompt.shape[1]
    bs, ts = x_sample.shape[0], x_sample.shape[1]
    even_p = [(jnp.zeros((bp, CONV_W - 1, HALF), F32), jnp.zeros((bp, H_B, DH_B, DH_B), F32),
               jnp.zeros((bp, H_B, DH_B), F32), jnp.zeros((bp, H_B), F32)) for _ in range(n_even)]
    y_prompt, ev_p, od_p = _trunk(x_prompt, p_prompt, even_p, None, None, W)
    even_s = [(state_conv[j], state_mlstm_C[j], state_mlstm_n[j], state_mlstm_m[j]) for j in range(n_even)]
    n_pool, psz = cache_diff_k.shape[1], cache_diff_k.shape[2]
    pool = lambda c: c.reshape(n_odd * n_pool, psz, HALF)
    caches = (pool(cache_diff_k), pool(cache_diff_v), pool(cache_sb_k), pool(cache_sb_v))
    tables = [page_table + j * n_pool for j in range(n_odd)]
    y_sample, ev_s, od_s = _trunk(x_sample, p_sample, even_s, [caches] * n_odd, tables, W)
    stack = lambda states, i, shape: jnp.stack([s[i].reshape(shape) for s in states])
    ev = lambda states, i: jnp.stack([s[i] for s in states])
    return (y_prompt, y_sample,
            ev(ev_p, 0), ev(ev_s, 0), ev(ev_p, 1), ev(ev_s, 1),
            ev(ev_p, 2), ev(ev_s, 2), ev(ev_p, 3), ev(ev_s, 3),
            stack(od_p, 0, (bp, tp, H_C, 2 * DH_C)), stack(od_s, 0, (bs, ts, H_C, 2 * DH_C)),
            stack(od_p, 1, (bp, tp, H_C, 2 * DH_C)), stack(od_s, 1, (bs, ts, H_C, 2 * DH_C)),
            stack(od_p, 2, (bp, tp, H_D, DH_D)), stack(od_s, 2, (bs, ts, H_D, DH_D)),
            stack(od_p, 3, (bp, tp, H_D, DH_D)), stack(od_s, 3, (bs, ts, H_D, DH_D)))
```

```python
import functools
import math

import jax
import jax.numpy as jnp
from jax import lax
from jax.experimental import pallas as pl
from jax.experimental.pallas import tpu as pltpu

F32 = jnp.float32
BF16 = jnp.bfloat16

LANES = 128
SUBLANES = 8
VMEM_LIMIT_BYTES = 56 * 1024 * 1024

D_MODEL = 1024
D_FF = 2816
HALF = 512
CONV_W = 31
H_B, DH_B = 4, 128
H_C, DH_C = 4, 64
H_D, DH_D = 8, 64
N_BUCKETS = 32
MAX_DIST = 128
MLSTM_CHUNK = 128
NEG_INF = -1e30
SB_DEAD = -104.0

ROW_TILE = 512
FF_TILE = 1408
DIFF_BLK = 512
SB_BLK = 256
CONV_ROWS = 256
CONV_SUB = 32
DEC_PAGES = 4


def _cp(*sem):
    return pltpu.CompilerParams(dimension_semantics=sem, vmem_limit_bytes=VMEM_LIMIT_BYTES)


def _rms(x, g, eps=1e-6):
    return x * lax.rsqrt(jnp.mean(x * x, axis=-1, keepdims=True) + eps) * g


def _dot(a, b):
    return jnp.dot(a, b, preferred_element_type=F32)


def _dot_nt(a, b):
    return lax.dot_general(a, b, (((1,), (1,)), ((), ())), preferred_element_type=F32)


def _dot_tn(a, b):
    return lax.dot_general(a, b, (((0,), (0,)), ((), ())), preferred_element_type=F32)


def _log_sigmoid_neg(z):
    return -(jnp.maximum(z, 0.0) + jnp.log1p(jnp.exp(-jnp.abs(z))))


def _split_bf16(x):
    hi = x.astype(BF16)
    lo = (x - hi.astype(F32)).astype(BF16)
    return hi, lo


def _ffn_kernel(x_ref, g_ref, wig_ref, wiu_ref, wo_ref, o_ref, hn_ref, acc_ref):
    j = pl.program_id(1)

    @pl.when(j == 0)
    def _():
        hn_ref[...] = _rms(x_ref[...], g_ref[...]).astype(BF16)
        acc_ref[...] = jnp.zeros_like(acc_ref)

    hn = hn_ref[...]
    gate = _dot(hn, wig_ref[...])
    up = _dot(hn, wiu_ref[...])
    act = (gate * jax.nn.sigmoid(gate) * up).astype(BF16)
    acc_ref[...] += _dot(act, wo_ref[...])

    @pl.when(j == pl.num_programs(1) - 1)
    def _():
        o_ref[...] = x_ref[...] + 0.5 * acc_ref[...]


def _ffn_half(h, g, wi, wo):
    m, d = h.shape
    tm = min(ROW_TILE, m)
    nj = D_FF // FF_TILE
    return pl.pallas_call(
        _ffn_kernel,
        grid=(m // tm, nj),
        in_specs=[
            pl.BlockSpec((tm, d), lambda i, j: (i, 0)),
            pl.BlockSpec((1, d), lambda i, j: (0, 0)),
            pl.BlockSpec((d, FF_TILE), lambda i, j: (0, j)),
            pl.BlockSpec((d, FF_TILE), lambda i, j: (0, j + D_FF // FF_TILE)),
            pl.BlockSpec((FF_TILE, d), lambda i, j: (j, 0)),
        ],
        out_specs=pl.BlockSpec((tm, d), lambda i, j: (i, 0)),
        out_shape=jax.ShapeDtypeStruct((m, d), F32),
        scratch_shapes=[pltpu.VMEM((tm, d), BF16), pltpu.VMEM((tm, d), F32)],
        compiler_params=_cp("parallel", "arbitrary"),
    )(h, g.reshape(1, d), wi, wi, wo)


def _ple_kernel(x_ref, g_ref, wg_ref, p_ref, wp_ref, fg_ref, o_ref, *, final):
    x = x_ref[...]
    hn = _rms(x, g_ref[...]).astype(BF16)
    gate = jax.nn.sigmoid(_dot(hn, wg_ref[...]))
    h = x + gate * _dot(p_ref[...].astype(BF16), wp_ref[...])
    if final:
        h = _rms(h, fg_ref[...])
    o_ref[...] = h


def _ple(h, g, wg, p, wp, fg, final):
    m, d = h.shape
    tm = min(ROW_TILE, m)
    pd = p.shape[1]
    return pl.pallas_call(
        functools.partial(_ple_kernel, final=final),
        grid=(m // tm,),
        in_specs=[
            pl.BlockSpec((tm, d), lambda i: (i, 0)),
            pl.BlockSpec((1, d), lambda i: (0, 0)),
            pl.BlockSpec((d, d), lambda i: (0, 0)),
            pl.BlockSpec((tm, pd), lambda i: (i, 0)),
            pl.BlockSpec((pd, d), lambda i: (0, 0)),
            pl.BlockSpec((1, d), lambda i: (0, 0)),
        ],
        out_specs=pl.BlockSpec((tm, d), lambda i: (i, 0)),
        out_shape=jax.ShapeDtypeStruct((m, d), F32),
        compiler_params=_cp("parallel"),
    )(h, g.reshape(1, d), wg, p, wp, fg.reshape(1, d))


def _inproj_even_kernel(x_ref, g_ref, w_ref, wgt_ref, u_ref, q_ref, k_ref, v_ref, o_ref, gt_ref):
    hn = _rms(x_ref[...], g_ref[...]).astype(BF16)

    def col(c):
        return _dot(hn, w_ref[:, c * HALF:(c + 1) * HALF])

    u_ref[...] = col(0) * jax.nn.sigmoid(col(1))
    q_ref[...] = col(2).astype(q_ref.dtype)
    k_ref[...] = (col(3) * (DH_B ** -0.5)).astype(k_ref.dtype)
    v_ref[...] = col(4).astype(v_ref.dtype)
    o_ref[...] = col(5)
    gt_ref[...] = _dot_nt(wgt_ref[...], hn)


def _inproj_even(h, g, w, wgt, qkv_dtype):
    m, d = h.shape
    tm = min(ROW_TILE, m)
    row = lambda i: (i, 0)
    out = lambda dt: jax.ShapeDtypeStruct((m, HALF), dt)
    return pl.pallas_call(
        _inproj_even_kernel,
        grid=(m // tm,),
        in_specs=[
            pl.BlockSpec((tm, d), row),
            pl.BlockSpec((1, d), lambda i: (0, 0)),
            pl.BlockSpec((d, 6 * HALF), lambda i: (0, 0)),
            pl.BlockSpec((2 * H_B, d), lambda i: (0, 0)),
        ],
        out_specs=[pl.BlockSpec((tm, HALF), row)] * 5 + [pl.BlockSpec((2 * H_B, tm), lambda i: (0, i))],
        out_shape=[out(F32), out(qkv_dtype), out(qkv_dtype), out(qkv_dtype), out(F32),
                   jax.ShapeDtypeStruct((2 * H_B, m), F32)],
        compiler_params=_cp("parallel"),
    )(h, g.reshape(1, d), w, wgt)


def _inproj_odd_kernel(x_ref, g_ref, w_ref, cq_ref, ck_ref, cv_ref, sq_ref, sk_ref, sv_ref):
    hn = _rms(x_ref[...], g_ref[...]).astype(BF16)

    def col(c):
        return _dot(hn, w_ref[:, c * HALF:(c + 1) * HALF])

    cq_ref[...] = (col(0) * (DH_C ** -0.5)).astype(cq_ref.dtype)
    ck_ref[...] = col(1)
    cv_ref[...] = col(2)
    sq_ref[...] = (col(3) * (DH_D ** -0.5)).astype(sq_ref.dtype)
    sk_ref[...] = col(4)
    sv_ref[...] = col(5)


def _inproj_odd(h, g, w, q_dtype):
    m, d = h.shape
    tm = min(ROW_TILE, m)
    row = lambda i: (i, 0)
    out = lambda dt: jax.ShapeDtypeStruct((m, HALF), dt)
    return pl.pallas_call(
        _inproj_odd_kernel,
        grid=(m // tm,),
        in_specs=[
            pl.BlockSpec((tm, d), row),
            pl.BlockSpec((1, d), lambda i: (0, 0)),
            pl.BlockSpec((d, 6 * HALF), lambda i: (0, 0)),
        ],
        out_specs=[pl.BlockSpec((tm, HALF), row)] * 6,
        out_shape=[out(q_dtype), out(F32), out(F32), out(q_dtype), out(F32), out(F32)],
        compiler_params=_cp("parallel"),
    )(h, g.reshape(1, d), w)


def _outproj_kernel(x_ref, a_ref, b_ref, wa_ref, wb_ref, o_ref):
    o_ref[...] = (x_ref[...] + _dot(a_ref[...].astype(BF16), wa_ref[...])
                  + _dot(b_ref[...].astype(BF16), wb_ref[...]))


def _outproj(h, a, b, w):
    m, d = h.shape
    tm = min(ROW_TILE, m)
    row = lambda i: (i, 0)
    return pl.pallas_call(
        _outproj_kernel,
        grid=(m // tm,),
        in_specs=[
            pl.BlockSpec((tm, d), row),
            pl.BlockSpec((tm, HALF), row),
            pl.BlockSpec((tm, HALF), row),
            pl.BlockSpec((HALF, d), lambda i: (0, 0)),
            pl.BlockSpec((HALF, d), lambda i: (1, 0)),
        ],
        out_specs=pl.BlockSpec((tm, d), row),
        out_shape=jax.ShapeDtypeStruct((m, d), F32),
        compiler_params=_cp("parallel"),
    )(h, a, b, w, w)


_HIST = 32


def _conv_kernel(u_ref, buf_ref, taps_ref, cb_ref, lg_ref, lb_ref, a_ref, st_ref, win_ref, *, tt, sub):
    t = pl.program_id(1)
    pad = _HIST - (CONV_W - 1)

    @pl.when(t == 0)
    def _():
        win_ref[0:SUBLANES, :] = jnp.zeros((SUBLANES, HALF), F32)
        win_ref[pad:_HIST, :] = buf_ref[0]

    win_ref[_HIST:_HIST + tt, :] = u_ref[0]
    for rb in range(tt // sub):
        acc = jnp.zeros((sub, HALF), F32)
        for w in range(CONV_W):
            acc = acc + win_ref[pl.ds(rb * sub + pad + w, sub), :] * taps_ref[w:w + 1, :]
        c = acc + cb_ref[...]
        mu = jnp.mean(c, axis=-1, keepdims=True)
        var = jnp.mean(jnp.square(c - mu), axis=-1, keepdims=True)
        cn = (c - mu) * lax.rsqrt(var + 1e-5) * lg_ref[...] + lb_ref[...]
        a_ref[0, rb * sub:(rb + 1) * sub, :] = (cn * jax.nn.sigmoid(cn)).astype(a_ref.dtype)

    @pl.when(t == pl.num_programs(1) - 1)
    def _():
        st_ref[0] = win_ref[tt + pad:tt + _HIST, :]

    win_ref[0:_HIST, :] = win_ref[tt:tt + _HIST, :]


def _conv_module(u, buf, taps, cb, lg, lb, out_dtype):
    b, t, _ = u.shape
    tt = min(CONV_ROWS, t)
    sub = min(CONV_SUB, tt)
    vec = lambda: pl.BlockSpec((1, HALF), lambda i, j: (0, 0))
    return pl.pallas_call(
        functools.partial(_conv_kernel, tt=tt, sub=sub),
        grid=(b, t // tt),
        in_specs=[
            pl.BlockSpec((1, tt, HALF), lambda i, j: (i, j, 0)),
            pl.BlockSpec((1, CONV_W - 1, HALF), lambda i, j: (i, 0, 0)),
            pl.BlockSpec((CONV_W, HALF), lambda i, j: (0, 0)),
            vec(), vec(), vec(),
        ],
        out_specs=[
            pl.BlockSpec((1, tt, HALF), lambda i, j: (i, j, 0)),
            pl.BlockSpec((1, CONV_W - 1, HALF), lambda i, j: (i, 0, 0)),
        ],
        out_shape=[jax.ShapeDtypeStruct((b, t, HALF), out_dtype),
                   jax.ShapeDtypeStruct((b, CONV_W - 1, HALF), F32)],
        scratch_shapes=[pltpu.VMEM((_HIST + tt, HALF), F32)],
        compiler_params=_cp("parallel", "arbitrary"),
    )(u, buf, taps, cb.reshape(1, HALF), lg.reshape(1, HALF), lb.reshape(1, HALF))


def _cumsum_lanes(x):
    lane = lax.broadcasted_iota(jnp.int32, x.shape, 1)
    sh = 1
    while sh < x.shape[1]:
        x = x + jnp.where(lane >= sh, pltpu.roll(x, sh, 1), 0.0)
        sh *= 2
    return x


def _mlstm_kernel(q_ref, k_ref, v_ref, o_ref, g_ref, gb_ref, c0_ref, n0_ref, m0_ref,
                  h_ref, c_ref, n_ref, m_ref, *, lr):
    L = MLSTM_CHUNK

    @pl.when(pl.program_id(1) == 0)
    def _():
        c_ref[...] = c0_ref[...]
        n_ref[...] = n0_ref[...]
        m_ref[...] = m0_ref[...]

    def rows(ref):
        x = ref[0]
        if lr < L:
            x = jnp.concatenate([x.astype(F32), jnp.zeros((L - lr, HALF), F32)], axis=0)
        return x.astype(BF16)

    q, k, v = rows(q_ref), rows(k_ref), rows(v_ref)
    g = g_ref[0] + gb_ref[...]
    row = lax.broadcasted_iota(jnp.int32, g.shape, 0)
    steps = jnp.where(row < H_B, 0.0, _log_sigmoid_neg(-g))
    bcum = _cumsum_lanes(steps)
    b4 = bcum[H_B:2 * H_B]
    a4 = g[0:H_B] - b4
    packed = jnp.concatenate([a4, b4, jnp.zeros((L - 2 * H_B, L), F32)], axis=0)
    cols = jnp.transpose(packed)
    tpos = lax.broadcasted_iota(jnp.int32, (L, L), 0)
    spos = lax.broadcasted_iota(jnp.int32, (L, L), 1)
    causal = spos <= tpos
    outs = []
    for h in range(H_B):
        sl = slice(h * DH_B, (h + 1) * DH_B)
        qh, kh, vh = q[:, sl], k[:, sl], v[:, sl]
        a_row = a4[h:h + 1, :]
        a_col = cols[:, h:h + 1]
        b_col = cols[:, H_B + h:H_B + h + 1]
        b_last = b4[h:h + 1, L - 1:L]
        m_old = m_ref[0, h:h + 1, 0:1]
        c_old = c_ref[0, h]
        n_old = n_ref[0, h:h + 1, :]
        dmat = jnp.where(causal, b_col + a_row, NEG_INF)
        inter = b_col + m_old
        mt = jnp.maximum(inter, jnp.max(dmat, axis=-1, keepdims=True))
        w_inter = jnp.exp(inter - mt)
        s = _dot_nt(qh, kh) * jnp.exp(dmat - mt)
        num = w_inter * _dot(qh, c_old.astype(BF16)) + _dot(s.astype(BF16), vh)
        den = (w_inter * jnp.sum(qh.astype(F32) * n_old, axis=-1, keepdims=True)
               + jnp.sum(s, axis=-1, keepdims=True))
        outs.append(num / jnp.maximum(jnp.abs(den), jnp.exp(-mt)))
        m_new = mt[L - 1:L, :]
        decay = jnp.exp(b_last + m_old - m_new)
        kw = kh.astype(F32) * jnp.exp(a_col + b_last - m_new)
        c_ref[0, h] = decay * c_old + _dot_tn(kw.astype(BF16), vh)
        n_ref[0, h:h + 1, :] = decay * n_old + jnp.sum(kw, axis=0, keepdims=True)
        m_ref[0, h:h + 1, :] = jnp.broadcast_to(m_new, (1, LANES))
    hs = jnp.concatenate(outs, axis=1)
    h_ref[0] = (jax.nn.sigmoid(o_ref[0]) * hs[0:lr]).astype(h_ref.dtype)


def _mlstm(q, k, v, o, gt, gate_b, c0, n0, m0, out_dtype):
    b, t, _ = q.shape
    L = MLSTM_CHUNK
    lr = min(L, t)
    nc = t // lr
    m0b = jnp.broadcast_to(jnp.pad(m0, ((0, 0), (0, SUBLANES - H_B)))[:, :, None], (b, SUBLANES, LANES))
    gb = jnp.broadcast_to(gate_b.reshape(2 * H_B, 1), (2 * H_B, L))
    blk = pl.BlockSpec((1, lr, HALF), lambda i, j: (i, j, 0))
    st = lambda *s: pl.BlockSpec((1,) + s, lambda i, j: (i,) + (0,) * len(s))
    h, c1, n1, m1 = pl.pallas_call(
        functools.partial(_mlstm_kernel, lr=lr),
        grid=(b, nc),
        in_specs=[blk, blk, blk, blk,
                  pl.BlockSpec((1, 2 * H_B, L), lambda i, j: (i, 0, j)),
                  pl.BlockSpec((2 * H_B, L), lambda i, j: (0, 0)),
                  st(H_B, DH_B, DH_B), st(H_B, DH_B), st(SUBLANES, LANES)],
        out_specs=[blk, st(H_B, DH_B, DH_B), st(H_B, DH_B), st(SUBLANES, LANES)],
        out_shape=[jax.ShapeDtypeStruct((b, t, HALF), out_dtype),
                   jax.ShapeDtypeStruct((b, H_B, DH_B, DH_B), F32),
                   jax.ShapeDtypeStruct((b, H_B, DH_B), F32),
                   jax.ShapeDtypeStruct((b, SUBLANES, LANES), F32)],
        compiler_params=_cp("parallel", "arbitrary"),
    )(q, k, v, o, gt, gb, c0, n0, m0b)
    return h, c1, n1, m1[:, :H_B, 0]


def _t5_bucket(dist):
    n = jnp.maximum(dist, 0)
    exact = N_BUCKETS // 2
    nf = jnp.maximum(n, 1).astype(F32)
    large = exact + (jnp.log(nf / exact) / math.log(MAX_DIST / exact) * (N_BUCKETS - exact)).astype(jnp.int32)
    return jnp.where(n < exact, n, jnp.minimum(large, N_BUCKETS - 1))


def _bias_of_dist(rel_bias, dist):
    onehot = jax.nn.one_hot(_t5_bucket(dist), N_BUCKETS, dtype=F32)
    b = jnp.einsum('...k,kh->h...', onehot, rel_bias.astype(F32), precision=lax.Precision.HIGHEST)
    return jnp.where(dist >= 0, b, NEG_INF)


def _lambda(lp_ref, lam_init):
    lp = lp_ref[...]
    s1 = jnp.sum(lp[0:1] * lp[1:2], axis=-1, keepdims=True)
    s2 = jnp.sum(lp[2:3] * lp[3:4], axis=-1, keepdims=True)
    return jnp.exp(s1) - jnp.exp(s2) + lam_init


def _head_norm(x, hg, lam_init):
    return x * lax.rsqrt(jnp.mean(x * x, axis=-1, keepdims=True) + 1e-6) * hg * (1.0 - lam_init)


def _diff_attn_kernel(q_ref, k_ref, v_ref, bd_ref, bp_ref, bf_ref, lp_ref, hg_ref, o_ref, kb_ref, vb_ref,
                      *, lam_init):
    blk = DIFF_BLK
    i = pl.program_id(2)

    @pl.when(i == 0)
    def _():
        kb_ref[...] = k_ref[...].astype(BF16)
        vb_ref[...] = v_ref[...].astype(BF16)

    q = q_ref[...]
    lane = lax.broadcasted_iota(jnp.int32, q.shape, 1)
    zero = jnp.zeros_like(q)
    q2 = jnp.concatenate([jnp.where(lane < DH_C, q, zero), jnp.where(lane >= DH_C, q, zero)], axis=0)

    def tile(j, bias):
        off = pl.multiple_of(j * blk, blk)
        s = _dot_nt(q2, kb_ref[pl.ds(off, blk), :]) + bias
        return s, vb_ref[pl.ds(off, blk), :]

    def update(carry, s, vblk):
        m, l, acc = carry
        mn = jnp.maximum(m, jnp.max(s, axis=-1, keepdims=True))
        p = jnp.exp(s - mn)
        al = jnp.exp(m - mn)
        return mn, al * l + jnp.sum(p, axis=-1, keepdims=True), al * acc + _dot(p.astype(BF16), vblk)

    s, vblk = tile(i, bd_ref[0])
    m = jnp.max(s, axis=-1, keepdims=True)
    p = jnp.exp(s - m)
    carry = (m, jnp.sum(p, axis=-1, keepdims=True), _dot(p.astype(BF16), vblk))

    def prev_step(c):
        return update(c, *tile(i - 1, bp_ref[0]))

    carry = lax.cond(i >= 1, prev_step, lambda c: c, carry)

    def far_step(j, c):
        return update(c, *tile(j, bf_ref[0]))

    m, l, acc = lax.fori_loop(0, jnp.maximum(i - 1, 0), far_step, carry)
    o = acc / l
    o = o[0:blk] - _lambda(lp_ref, lam_init) * o[blk:2 * blk]
    o_ref[...] = _head_norm(o, hg_ref[0], lam_init).astype(o_ref.dtype)


def _diff_attention(cq, ck, cv, b, t, rel_bias, lam_p, head_g, lam_init):
    blk = DIFF_BLK
    nq = t // blk
    r = jnp.arange(blk, dtype=jnp.int32)
    d0 = r[:, None] - r[None, :]
    bd = jnp.tile(_bias_of_dist(rel_bias, d0), (1, 2, 1))
    bp = jnp.tile(_bias_of_dist(rel_bias, d0 + blk), (1, 2, 1))
    assert blk + 1 >= MAX_DIST
    bfar = _bias_of_dist(rel_bias, jnp.full((1, blk), 2 * blk, jnp.int32))
    per_head = lambda *s: pl.BlockSpec((1,) + s, lambda bi, h, i: (h, 0, 0))
    return pl.pallas_call(
        functools.partial(_diff_attn_kernel, lam_init=lam_init),
        grid=(b, H_C, nq),
        in_specs=[
            pl.BlockSpec((blk, LANES), lambda bi, h, i: (bi * nq + i, h)),
            pl.BlockSpec((t, LANES), lambda bi, h, i: (bi, h)),
            pl.BlockSpec((t, LANES), lambda bi, h, i: (bi, h)),
            per_head(2 * blk, blk), per_head(2 * blk, blk), per_head(1, blk),
            pl.BlockSpec((4, DH_C), lambda bi, h, i: (0, 0)),
            per_head(1, LANES),
        ],
        out_specs=pl.BlockSpec((blk, LANES), lambda bi, h, i: (bi * nq + i, h)),
        out_shape=jax.ShapeDtypeStruct((b * t, HALF), BF16),
        scratch_shapes=[pltpu.VMEM((t, LANES), BF16), pltpu.VMEM((t, LANES), BF16)],
        compiler_params=_cp("parallel", "parallel", "arbitrary"),
    )(cq, ck, cv, bd, bp, bfar, lam_p, head_g.reshape(H_C, 1, LANES))


def _sb_tile(q2, kblk, vblk, upper, r, mask):
    z = _dot_nt(q2, kblk)
    lk = _log_sigmoid_neg(z)
    if mask is not None:
        lk = jnp.where(mask, lk, 0.0)
    hi, lo = _split_bf16(lk)
    after = _dot(hi, upper) + _dot(lo, upper) + r
    w = jnp.exp(lk + z + after)
    if mask is not None:
        w = jnp.where(mask, w, 0.0)
    return _dot(w.astype(BF16), vblk), r + jnp.sum(lk, axis=-1, keepdims=True)


def _strict_upper(n):
    j = lax.broadcasted_iota(jnp.int32, (n, n), 0)
    s = lax.broadcasted_iota(jnp.int32, (n, n), 1)
    return jnp.where(j > s, 1.0, 0.0).astype(BF16)


def _sb_attn_kernel(q_ref, k_ref, v_ref, o_ref, kb_ref, vb_ref):
    blk = SB_BLK
    i = pl.program_id(2)

    @pl.when(i == 0)
    def _():
        kb_ref[...] = k_ref[...].astype(BF16)
        vb_ref[...] = v_ref[...].astype(BF16)

    q = q_ref[...]
    lane = lax.broadcasted_iota(jnp.int32, q.shape, 1)
    zero = jnp.zeros_like(q)
    q2 = jnp.concatenate([jnp.where(lane < DH_D, q, zero), jnp.where(lane >= DH_D, q, zero)], axis=0)
    upper = _strict_upper(blk)
    tpos = lax.broadcasted_iota(jnp.int32, (2 * blk, blk), 0) % blk
    spos = lax.broadcasted_iota(jnp.int32, (2 * blk, blk), 1)

    def tile(j, r, mask):
        off = pl.multiple_of(j * blk, blk)
        return _sb_tile(q2, kb_ref[pl.ds(off, blk), :], vb_ref[pl.ds(off, blk), :], upper, r, mask)

    acc, r = tile(i, jnp.zeros((2 * blk, 1), F32), spos < tpos)

    def cond(c):
        j, _, r = c
        return jnp.logical_and(j >= 0, jnp.max(r) > SB_DEAD)

    def body(c):
        j, acc, r = c
        pv, r = tile(j, r, None)
        return j - 1, acc + pv, r

    _, acc, _ = lax.while_loop(cond, body, (i - 1, acc, r))
    o_ref[...] = jnp.where(lane < DH_D, acc[0:blk], acc[blk:2 * blk]).astype(o_ref.dtype)


def _sb_attention(sq, sk, sv, b, t):
    blk = SB_BLK
    nq = t // blk
    return pl.pallas_call(
        _sb_attn_kernel,
        grid=(b, HALF // LANES, nq),
        in_specs=[
            pl.BlockSpec((blk, LANES), lambda bi, h, i: (bi * nq + i, h)),
            pl.BlockSpec((t, LANES), lambda bi, h, i: (bi, h)),
            pl.BlockSpec((t, LANES), lambda bi, h, i: (bi, h)),
        ],
        out_specs=pl.BlockSpec((blk, LANES), lambda bi, h, i: (bi * nq + i, h)),
        out_shape=jax.ShapeDtypeStruct((b * t, HALF), BF16),
        scratch_shapes=[pltpu.VMEM((t, LANES), BF16), pltpu.VMEM((t, LANES), BF16)],
        compiler_params=_cp("parallel", "parallel", "arbitrary"),
    )(sq, sk, sv)


_DEC_ROWS = 64


def _dec_attn_kernel(pt_ref, cq_ref, sq_ref, ckf_ref, cvf_ref, sk_ref, sv_ref, bias_ref, bnew_ref, lp_ref, hg_ref,
                     *rest, ts, lam_init):
    np_ = DEC_PAGES
    pages = rest[:4 * np_]
    oc_ref, os_ref, qa_ref, qs_ref, m_ref, l_ref, r_ref, accd_ref, accs_ref = rest[4 * np_:]
    g = pl.program_id(1)
    psz = pages[2].shape[2]
    upper = _strict_upper(psz)

    def diff_update(tiles):
        m = m_ref[...]
        mn = m
        for s, _ in tiles:
            mn = jnp.maximum(mn, jnp.max(s, axis=-1, keepdims=True))
        al = jnp.exp(m - mn)
        l = al * l_ref[...]
        acc = al * accd_ref[...]
        for s, v in tiles:
            p = jnp.exp(s - mn)
            l = l + jnp.sum(p, axis=-1, keepdims=True)
            acc = acc + _dot(p.astype(BF16), v)
        m_ref[...] = mn
        l_ref[...] = l
        accd_ref[...] = acc

    def sb_update(tiles):
        r = r_ref[...]
        acc = accs_ref[...]
        for z, pv, mask in tiles:
            lk = _log_sigmoid_neg(z)
            if mask is not None:
                lk = jnp.where(mask, lk, 0.0)
            hi, lo = _split_bf16(lk)
            w = jnp.exp(lk + z + _dot(hi, upper) + _dot(lo, upper) + r)
            if mask is not None:
                w = jnp.where(mask, w, 0.0)
            acc = acc + pv(w.astype(BF16))
            r = r + jnp.sum(lk, axis=-1, keepdims=True)
        r_ref[...] = r
        accs_ref[...] = acc

    def pad_rows(x):
        return jnp.concatenate([x, jnp.zeros((psz - x.shape[0], x.shape[1]), F32)], axis=0).astype(BF16)

    @pl.when(g == 0)
    def _():
        cq = cq_ref[0]
        sq = sq_ref[0]
        half = lax.broadcasted_iota(jnp.int32, (ts, LANES), 1) >= DH_C
        qa_ref[...] = jnp.concatenate(
            [jnp.where(half if c % 2 else jnp.logical_not(half), cq[:, (c // 2) * LANES:(c // 2 + 1) * LANES], 0.0)
             for c in range(2 * H_C)], axis=0).astype(BF16)
        lane = lax.broadcasted_iota(jnp.int32, (ts, HALF), 1)
        qs_ref[...] = jnp.concatenate(
            [jnp.where((lane >= c * DH_D) & (lane < (c + 1) * DH_D), sq, 0.0) for c in range(H_D)],
            axis=0).astype(BF16)
        m_ref[...] = jnp.full_like(m_ref, NEG_INF)
        l_ref[...] = jnp.zeros_like(l_ref)
        r_ref[...] = jnp.zeros_like(r_ref)
        accd_ref[...] = jnp.zeros_like(accd_ref)
        accs_ref[...] = jnp.zeros_like(accs_ref)
        diff_update([(_dot_nt(qa_ref[...], pad_rows(ckf_ref[0])) + bnew_ref[...], pad_rows(cvf_ref[0]))])
        tq = lax.broadcasted_iota(jnp.int32, (_DEC_ROWS, psz), 0) % ts
        kpos = lax.broadcasted_iota(jnp.int32, (_DEC_ROWS, psz), 1)
        sv_new = pad_rows(sv_ref[0])
        sb_update([(_dot_nt(qs_ref[...], pad_rows(sk_ref[0])), lambda w: _dot(w, sv_new), kpos < tq)])

    qa = qa_ref[...]
    qs = qs_ref[...]
    dtiles, stiles = [], []
    for p in range(np_):
        dk, dv, skt, svt = pages[4 * p:4 * p + 4]
        dtiles.append((_dot_nt(qa, dk[0].astype(BF16)) + bias_ref[p], dv[0].astype(BF16)))
        stiles.append((_dot(qs, skt[0].astype(BF16)),
                       functools.partial(_dot_nt, b=svt[0].astype(BF16)), None))
    diff_update(dtiles)
    sb_update(stiles)

    @pl.when(g == pl.num_programs(1) - 1)
    def _():
        lam = _lambda(lp_ref, lam_init)
        o = accd_ref[...] / l_ref[...]
        for h in range(H_C):
            r0 = h * 2 * ts
            oh = o[r0:r0 + ts] - lam * o[r0 + ts:r0 + 2 * ts]
            oc_ref[0, :, h * LANES:(h + 1) * LANES] = _head_norm(oh, hg_ref[h:h + 1, :], lam_init)
        acc = accs_ref[...]
        lane = lax.broadcasted_iota(jnp.int32, (ts, LANES), 1)
        for pr in range(H_D // 2):
            sl = slice(pr * LANES, (pr + 1) * LANES)
            r0 = pr * 2 * ts
            os_ref[0, :, sl] = jnp.where(lane < DH_D, acc[r0:r0 + ts, sl], acc[r0 + ts:r0 + 2 * ts, sl])


def _decode_attention(cq, sq, new_rows, caches, page_table, rel_bias, lam_p, head_g, lam_init):
    b, ts, _ = cq.shape
    n_pages = page_table.shape[1]
    psz = caches[2].shape[2]
    past = n_pages * psz
    np_ = DEC_PAGES
    nkn = LANES // H_C
    assert n_pages % np_ == 0 and _DEC_ROWS == 2 * H_C * ts == H_D * ts and ts <= nkn
    ck, cv, sk, sv = new_rows
    ckf = ck.reshape(b, ts * H_C, LANES)
    cvf = cv.reshape(b, ts * H_C, LANES)

    def table(base):
        base = jnp.moveaxis(base, 0, -3)
        own = jnp.arange(H_C)[:, None, None, None] == jnp.arange(H_C)[None, None, None, :]
        tab = jnp.where(own, base[..., None], NEG_INF)
        tab = jnp.broadcast_to(tab[..., :, None, :, :, :], tab.shape[:-3] + (2,) + tab.shape[-3:])
        return tab.reshape(tab.shape[:-5] + (_DEC_ROWS, tab.shape[-2] * H_C))

    tq = jnp.arange(ts, dtype=jnp.int32)
    kpos = (jnp.arange(n_pages - 1, -1, -1, dtype=jnp.int32)[:, None] * psz
            + jnp.arange(psz, dtype=jnp.int32)[None, :])
    dist = past + tq[None, :, None] - kpos[:, None, :]
    bias = table(_bias_of_dist(rel_bias, dist))
    knew = jnp.arange(nkn, dtype=jnp.int32)
    dnew = jnp.where(knew[None, :] < ts, tq[:, None] - knew[None, :], -1)
    bnew = table(_bias_of_dist(rel_bias, dnew))

    row = pl.BlockSpec((1, ts, HALF), lambda bi, g, pt: (bi, 0, 0))
    rowf = pl.BlockSpec((1, ts * H_C, LANES), lambda bi, g, pt: (bi, 0, 0))

    def page_spec(p):
        return pl.BlockSpec((1, HALF, LANES),
                            lambda bi, g, pt: (pt[bi * n_pages + n_pages - 1 - (g * np_ + p)], 0, 0))

    page_specs, page_args = [], []
    for p in range(np_):
        for c in caches:
            page_specs.append(page_spec(p))
            page_args.append(c)
    oc, os_ = pl.pallas_call(
        functools.partial(_dec_attn_kernel, ts=ts, lam_init=lam_init),
        grid_spec=pltpu.PrefetchScalarGridSpec(
            num_scalar_prefetch=1,
            grid=(b, n_pages // np_),
            in_specs=[row, row, rowf, rowf, row, row,
                      pl.BlockSpec((np_, _DEC_ROWS, psz * H_C), lambda bi, g, pt: (g, 0, 0)),
                      pl.BlockSpec((_DEC_ROWS, LANES), lambda bi, g, pt: (0, 0)),
                      pl.BlockSpec((4, DH_C), lambda bi, g, pt: (0, 0)),
                      pl.BlockSpec((H_C, LANES), lambda bi, g, pt: (0, 0)),
                      ] + page_specs,
            out_specs=[row, row],
            scratch_shapes=[
                pltpu.VMEM((_DEC_ROWS, LANES), BF16), pltpu.VMEM((_DEC_ROWS, HALF), BF16),
                pltpu.VMEM((_DEC_ROWS, 1), F32), pltpu.VMEM((_DEC_ROWS, 1), F32), pltpu.VMEM((_DEC_ROWS, 1), F32),
                pltpu.VMEM((_DEC_ROWS, LANES), F32), pltpu.VMEM((_DEC_ROWS, HALF), F32),
            ],
        ),
        out_shape=[jax.ShapeDtypeStruct((b, ts, HALF), F32)] * 2,
        compiler_params=_cp("parallel", "arbitrary"),
    )(page_table.reshape(-1), cq, sq, ckf, cvf, sk, sv, bias, bnew, lam_p, head_g.reshape(H_C, LANES), *page_args)
    return oc, os_


def _trunk(x, p, even_states, odd_past, page_table, W):
    b, t, d = x.shape
    m = b * t
    prompt = odd_past is None
    act = BF16 if prompt else F32
    h = x.reshape(m, d)
    depth = p.shape[0]
    new_even, new_odd = [], []
    for l in range(depth):
        j = l // 2
        h = _ffn_half(h, W['ffn_norm1'][l], W['ffn1_wi'][l], W['ffn1_wo'][l])
        if l % 2 == 0:
            buf, c0, n0, m0 = even_states[j]
            u, q, k, v, o, gt = _inproj_even(h, W['mix_norm'][l], W['ev_w_in'][j], W['ev_w_gt'][j], act)
            a_out, buf1 = _conv_module(u.reshape(b, t, HALF), buf, W['ev_conv_w'][j], W['ev_conv_b'][j],
                                       W['ev_ln_g'][j], W['ev_ln_b'][j], act)
            gt = jnp.moveaxis(gt.reshape(2 * H_B, b, t), 1, 0)
            if t < MLSTM_CHUNK:
                padv = jnp.where(jnp.arange(2 * H_B) < H_B, NEG_INF, -NEG_INF).astype(F32)
                gt = jnp.concatenate(
                    [gt, jnp.broadcast_to(padv[None, :, None], (b, 2 * H_B, MLSTM_CHUNK - t))], axis=2)
            r3 = lambda a: a.reshape(b, t, HALF)
            b_out, c1, n1, m1 = _mlstm(r3(q), r3(k), r3(v), r3(o), gt, W['ev_gate_b'][j], c0, n0, m0, act)
            new_even.append((buf1, c1, n1, m1))
            h = _outproj(h, a_out.reshape(m, HALF), b_out.reshape(m, HALF), W['ev_w_out'][j])
        else:
            lam_init = 0.8 - 0.6 * math.exp(-0.3 * l)
            cq, ck, cv, sq, sk, sv = _inproj_odd(h, W['mix_norm'][l], W['od_w_in'][j], act)
            new_odd.append((ck, cv, sk, sv))
            if prompt:
                oc = _diff_attention(cq, ck, cv, b, t, W['rel_bias'], W['od_lambda'][j], W['od_head_g'][j], lam_init)
                os_ = _sb_attention(sq, sk, sv, b, t)
            else:
                r3 = lambda a: a.reshape(b, t, HALF)
                oc, os_ = _decode_attention(r3(cq), r3(sq), [r3(a) for a in (ck, cv, sk, sv)], odd_past[j],
                                            page_table[j], W['rel_bias'], W['od_lambda'][j], W['od_head_g'][j],
                                            lam_init)
            h = _outproj(h, oc.reshape(m, HALF), os_.reshape(m, HALF), W['od_w_out'][j])
        h = _ffn_half(h, W['ffn_norm2'][l], W['ffn2_wi'][l], W['ffn2_wo'][l])
        h = _ple(h, W['ple_norm'][l], W['ple_wg'][l], p[l].reshape(m, -1), W['ple_wp'][l], W['final_norm'],
                 final=(l == depth - 1))
    return h.reshape(b, t, d), new_even, new_odd


def kernel(x_prompt, x_sample, p_prompt, p_sample, state_conv, state_mlstm_C, state_mlstm_n, state_mlstm_m, cache_diff_k, cache_diff_v, cache_sb_k, cache_sb_v, page_table, ffn_norm1, ffn1_wi, ffn1_wo, mix_norm, ffn_norm2, ffn2_wi, ffn2_wo, ple_norm, ple_wg, ple_wp, ev_w_in, ev_conv_w, ev_conv_b, ev_ln_g, ev_ln_b, ev_gate_b, ev_w_out, od_w_in, od_lambda, od_head_g, od_w_out, rel_bias, final_norm):
    bf = lambda a: a.astype(BF16)
    n_even, n_odd = ev_w_in.shape[0], od_w_in.shape[0]
    W = dict(ffn_norm1=ffn_norm1, ffn1_wi=bf(ffn1_wi), ffn1_wo=bf(ffn1_wo), mix_norm=mix_norm,
             ffn_norm2=ffn_norm2, ffn2_wi=bf(ffn2_wi), ffn2_wo=bf(ffn2_wo),
             ple_norm=ple_norm, ple_wg=bf(ple_wg), ple_wp=bf(ple_wp),
             ev_w_in=bf(ev_w_in[:, :, :6 * HALF]), ev_w_gt=bf(jnp.swapaxes(ev_w_in[:, :, 6 * HALF:], 1, 2)),
             ev_conv_w=ev_conv_w, ev_conv_b=ev_conv_b, ev_ln_g=ev_ln_g, ev_ln_b=ev_ln_b,
             ev_gate_b=ev_gate_b, ev_w_out=bf(ev_w_out),
             od_w_in=bf(od_w_in), od_lambda=od_lambda, od_head_g=od_head_g, od_w_out=bf(od_w_out),
             rel_bias=rel_bias, final_norm=final_norm)
    bp, tp = x_prompt.shape[0], x_prompt.shape[1]
    bs, ts = x_sample.shape[0], x_sample.shape[1]
    even_p = [(jnp.zeros((bp, CONV_W - 1, HALF), F32), jnp.zeros((bp, H_B, DH_B, DH_B), F32),
               jnp.zeros((bp, H_B, DH_B), F32), jnp.zeros((bp, H_B), F32)) for _ in range(n_even)]
    y_prompt, ev_p, od_p = _trunk(x_prompt, p_prompt, even_p, None, None, W)
    even_s = [(state_conv[j], state_mlstm_C[j], state_mlstm_n[j], state_mlstm_m[j]) for j in range(n_even)]
    n_pool, psz = cache_diff_k.shape[1], cache_diff_k.shape[2]
    pool_d = lambda c: c.reshape(n_odd * n_pool, psz * H_C, 2 * DH_C)
    pool_s = lambda c: jnp.transpose(c, (0, 1, 3, 4, 2)).reshape(n_odd * n_pool, H_D * DH_D, psz)
    caches = (pool_d(cache_diff_k), pool_d(cache_diff_v), pool_s(cache_sb_k), pool_s(cache_sb_v))
    tables = [page_table + j * n_pool for j in range(n_odd)]
    y_sample, ev_s, od_s = _trunk(x_sample, p_sample, even_s, [caches] * n_odd, tables, W)
    stack = lambda states, i, shape: jnp.stack([s[i].reshape(shape) for s in states])
    ev = lambda states, i: jnp.stack([s[i] for s in states])
    return (y_prompt, y_sample,
            ev(ev_p, 0), ev(ev_s, 0), ev(ev_p, 1), ev(ev_s, 1),
            ev(ev_p, 2), ev(ev_s, 2), ev(ev_p, 3), ev(ev_s, 3),
            stack(od_p, 0, (bp, tp, H_C, 2 * DH_C)), stack(od_s, 0, (bs, ts, H_C, 2 * DH_C)),
            stack(od_p, 1, (bp, tp, H_C, 2 * DH_C)), stack(od_s, 1, (bs, ts, H_C, 2 * DH_C)),
            stack(od_p, 2, (bp, tp, H_D, DH_D)), stack(od_s, 2, (bs, ts, H_D, DH_D)),
            stack(od_p, 3, (bp, tp, H_D, DH_D)), stack(od_s, 3, (bs, ts, H_D, DH_D)))
```

```python
import functools
import math

import jax
import jax.numpy as jnp
from jax import lax
from jax.experimental import pallas as pl
from jax.experimental.pallas import tpu as pltpu

F32 = jnp.float32
BF16 = jnp.bfloat16

LANES = 128
SUBLANES = 8
VMEM_LIMIT_BYTES = 56 * 1024 * 1024

D_MODEL = 1024
D_FF = 2816
HALF = 512
CONV_W = 31
H_B, DH_B = 4, 128
H_C, DH_C = 4, 64
H_D, DH_D = 8, 64
N_BUCKETS = 32
MAX_DIST = 128
MLSTM_CHUNK = 128
NEG_INF = -1e30
SB_DEAD = -104.0

ROW_TILE = 512
FF_CHUNK = 256
DIFF_BLK = 512
SB_BLK = 256
CONV_ROWS = 256
CONV_SUB = 32
CONV_SEQS = 16
MLSTM_SEQS = 4
DEC_PAGES = 4


def _cp(*sem):
    return pltpu.CompilerParams(dimension_semantics=sem, vmem_limit_bytes=VMEM_LIMIT_BYTES)


def _rms(x, g, eps=1e-6):
    return x * lax.rsqrt(jnp.mean(x * x, axis=-1, keepdims=True) + eps) * g


def _dot(a, b):
    return jnp.dot(a, b, preferred_element_type=F32)


def _dot_nt(a, b):
    return lax.dot_general(a, b, (((1,), (1,)), ((), ())), preferred_element_type=F32)


def _dot_tn(a, b):
    return lax.dot_general(a, b, (((0,), (0,)), ((), ())), preferred_element_type=F32)


def _log_sigmoid_neg(z):
    return -(jnp.maximum(z, 0.0) + jnp.log1p(jnp.exp(-jnp.abs(z))))


def _split_bf16(x):
    hi = x.astype(BF16)
    lo = (x - hi.astype(F32)).astype(BF16)
    return hi, lo


def _ffn_kernel(x_ref, g_ref, wi_ref, wo_ref, o_ref, act_ref):
    x = x_ref[...]
    hn = _rms(x, g_ref[...]).astype(BF16)
    for c in range(D_FF // FF_CHUNK):
        lo = c * FF_CHUNK
        gate = _dot(hn, wi_ref[:, lo:lo + FF_CHUNK])
        up = _dot(hn, wi_ref[:, D_FF + lo:D_FF + lo + FF_CHUNK])
        act_ref[:, lo:lo + FF_CHUNK] = (gate * jax.nn.sigmoid(gate) * up).astype(BF16)
    o_ref[...] = x + 0.5 * _dot(act_ref[...], wo_ref[...])


def _ffn_half(h, g, wi, wo):
    m, d = h.shape
    tm = min(ROW_TILE, m)
    resident = lambda shape: pl.BlockSpec(shape, lambda i: (0, 0), pipeline_mode=pl.Buffered(1))
    return pl.pallas_call(
        _ffn_kernel,
        grid=(m // tm,),
        in_specs=[
            pl.BlockSpec((tm, d), lambda i: (i, 0)),
            pl.BlockSpec((1, d), lambda i: (0, 0)),
            resident((d, 2 * D_FF)),
            resident((D_FF, d)),
        ],
        out_specs=pl.BlockSpec((tm, d), lambda i: (i, 0)),
        out_shape=jax.ShapeDtypeStruct((m, d), F32),
        scratch_shapes=[pltpu.VMEM((tm, D_FF), BF16)],
        compiler_params=_cp("parallel"),
    )(h, g.reshape(1, d), wi, wo)


def _ple_kernel(x_ref, g_ref, wg_ref, p_ref, wp_ref, fg_ref, o_ref, *, final):
    x = x_ref[...]
    hn = _rms(x, g_ref[...]).astype(BF16)
    gate = jax.nn.sigmoid(_dot(hn, wg_ref[...]))
    h = x + gate * _dot(p_ref[...].astype(BF16), wp_ref[...])
    if final:
        h = _rms(h, fg_ref[...])
    o_ref[...] = h


def _ple(h, g, wg, p, wp, fg, final):
    m, d = h.shape
    tm = min(ROW_TILE, m)
    pd = p.shape[1]
    return pl.pallas_call(
        functools.partial(_ple_kernel, final=final),
        grid=(m // tm,),
        in_specs=[
            pl.BlockSpec((tm, d), lambda i: (i, 0)),
            pl.BlockSpec((1, d), lambda i: (0, 0)),
            pl.BlockSpec((d, d), lambda i: (0, 0)),
            pl.BlockSpec((tm, pd), lambda i: (i, 0)),
            pl.BlockSpec((pd, d), lambda i: (0, 0)),
            pl.BlockSpec((1, d), lambda i: (0, 0)),
        ],
        out_specs=pl.BlockSpec((tm, d), lambda i: (i, 0)),
        out_shape=jax.ShapeDtypeStruct((m, d), F32),
        compiler_params=_cp("parallel"),
    )(h, g.reshape(1, d), wg, p, wp, fg.reshape(1, d))


def _inproj_even_kernel(x_ref, g_ref, w_ref, wgt_ref, u_ref, q_ref, k_ref, v_ref, o_ref, gt_ref):
    hn = _rms(x_ref[...], g_ref[...]).astype(BF16)

    def col(c):
        return _dot(hn, w_ref[:, c * HALF:(c + 1) * HALF])

    u_ref[...] = col(0) * jax.nn.sigmoid(col(1))
    q_ref[...] = col(2).astype(q_ref.dtype)
    k_ref[...] = (col(3) * (DH_B ** -0.5)).astype(k_ref.dtype)
    v_ref[...] = col(4).astype(v_ref.dtype)
    o_ref[...] = col(5)
    gt_ref[...] = _dot_nt(wgt_ref[...], hn)


def _inproj_even(h, g, w, wgt, qkv_dtype):
    m, d = h.shape
    tm = min(ROW_TILE, m)
    row = lambda i: (i, 0)
    out = lambda dt: jax.ShapeDtypeStruct((m, HALF), dt)
    return pl.pallas_call(
        _inproj_even_kernel,
        grid=(m // tm,),
        in_specs=[
            pl.BlockSpec((tm, d), row),
            pl.BlockSpec((1, d), lambda i: (0, 0)),
            pl.BlockSpec((d, 6 * HALF), lambda i: (0, 0)),
            pl.BlockSpec((2 * H_B, d), lambda i: (0, 0)),
        ],
        out_specs=[pl.BlockSpec((tm, HALF), row)] * 5 + [pl.BlockSpec((2 * H_B, tm), lambda i: (0, i))],
        out_shape=[out(F32), out(qkv_dtype), out(qkv_dtype), out(qkv_dtype), out(F32),
                   jax.ShapeDtypeStruct((2 * H_B, m), F32)],
        compiler_params=_cp("parallel"),
    )(h, g.reshape(1, d), w, wgt)


def _inproj_odd_kernel(x_ref, g_ref, w_ref, cq_ref, ck_ref, cv_ref, sq_ref, sk_ref, sv_ref):
    hn = _rms(x_ref[...], g_ref[...]).astype(BF16)

    def col(c):
        return _dot(hn, w_ref[:, c * HALF:(c + 1) * HALF])

    cq_ref[...] = (col(0) * (DH_C ** -0.5)).astype(cq_ref.dtype)
    ck_ref[...] = col(1)
    cv_ref[...] = col(2)
    sq_ref[...] = (col(3) * (DH_D ** -0.5)).astype(sq_ref.dtype)
    sk_ref[...] = col(4)
    sv_ref[...] = col(5)


def _inproj_odd(h, g, w, q_dtype):
    m, d = h.shape
    tm = min(ROW_TILE, m)
    row = lambda i: (i, 0)
    out = lambda dt: jax.ShapeDtypeStruct((m, HALF), dt)
    return pl.pallas_call(
        _inproj_odd_kernel,
        grid=(m // tm,),
        in_specs=[
            pl.BlockSpec((tm, d), row),
            pl.BlockSpec((1, d), lambda i: (0, 0)),
            pl.BlockSpec((d, 6 * HALF), lambda i: (0, 0)),
        ],
        out_specs=[pl.BlockSpec((tm, HALF), row)] * 6,
        out_shape=[out(q_dtype), out(F32), out(F32), out(q_dtype), out(F32), out(F32)],
        compiler_params=_cp("parallel"),
    )(h, g.reshape(1, d), w)


def _outproj_kernel(x_ref, a_ref, b_ref, wa_ref, wb_ref, o_ref):
    o_ref[...] = (x_ref[...] + _dot(a_ref[...].astype(BF16), wa_ref[...])
                  + _dot(b_ref[...].astype(BF16), wb_ref[...]))


def _outproj(h, a, b, w):
    m, d = h.shape
    tm = min(ROW_TILE, m)
    row = lambda i: (i, 0)
    return pl.pallas_call(
        _outproj_kernel,
        grid=(m // tm,),
        in_specs=[
            pl.BlockSpec((tm, d), row),
            pl.BlockSpec((tm, HALF), row),
            pl.BlockSpec((tm, HALF), row),
            pl.BlockSpec((HALF, d), lambda i: (0, 0)),
            pl.BlockSpec((HALF, d), lambda i: (1, 0)),
        ],
        out_specs=pl.BlockSpec((tm, d), row),
        out_shape=jax.ShapeDtypeStruct((m, d), F32),
        compiler_params=_cp("parallel"),
    )(h, a, b, w, w)


_HIST = 32


def _conv_kernel(u_ref, buf_ref, taps_ref, cb_ref, lg_ref, lb_ref, a_ref, st_ref, win_ref, sh_ref, *, tt, sub, bb):
    t = pl.program_id(1)
    pad = _HIST - (CONV_W - 1)
    span = tt + _HIST - SUBLANES

    for s in range(bb):
        @pl.when(t == 0)
        def _(s=s):
            win_ref[s, 0:SUBLANES, :] = jnp.zeros((SUBLANES, HALF), F32)
            win_ref[s, pad:_HIST, :] = buf_ref[s]

        win_ref[s, _HIST:_HIST + tt, :] = u_ref[s]
        for r in range(1, SUBLANES):
            sh_ref[s, r - 1] = win_ref[s, pl.ds(r, span), :]
        for rb in range(tt // sub):
            acc = jnp.zeros((sub, HALF), F32)
            for w in range(CONV_W):
                a8, r = divmod(pad + w, SUBLANES)
                lo = rb * sub + a8 * SUBLANES
                src = win_ref[s, lo:lo + sub, :] if r == 0 else sh_ref[s, r - 1, lo:lo + sub, :]
                acc = acc + src * taps_ref[w:w + 1, :]
            c = acc + cb_ref[...]
            mu = jnp.mean(c, axis=-1, keepdims=True)
            var = jnp.mean(jnp.square(c - mu), axis=-1, keepdims=True)
            cn = (c - mu) * lax.rsqrt(var + 1e-5) * lg_ref[...] + lb_ref[...]
            a_ref[s, rb * sub:(rb + 1) * sub, :] = (cn * jax.nn.sigmoid(cn)).astype(a_ref.dtype)

        @pl.when(t == pl.num_programs(1) - 1)
        def _(s=s):
            st_ref[s] = win_ref[s, tt + pad:tt + _HIST, :]

        win_ref[s, 0:_HIST, :] = win_ref[s, tt:tt + _HIST, :]


def _conv_module(u, buf, taps, cb, lg, lb, out_dtype):
    b, t, _ = u.shape
    tt = min(CONV_ROWS, t)
    sub = min(CONV_SUB, tt)
    bb = 1 if t > tt else math.gcd(b, CONV_SEQS)
    vec = lambda: pl.BlockSpec((1, HALF), lambda i, j: (0, 0))
    return pl.pallas_call(
        functools.partial(_conv_kernel, tt=tt, sub=sub, bb=bb),
        grid=(b // bb, t // tt),
        in_specs=[
            pl.BlockSpec((bb, tt, HALF), lambda i, j: (i, j, 0)),
            pl.BlockSpec((bb, CONV_W - 1, HALF), lambda i, j: (i, 0, 0)),
            pl.BlockSpec((CONV_W, HALF), lambda i, j: (0, 0)),
            vec(), vec(), vec(),
        ],
        out_specs=[
            pl.BlockSpec((bb, tt, HALF), lambda i, j: (i, j, 0)),
            pl.BlockSpec((bb, CONV_W - 1, HALF), lambda i, j: (i, 0, 0)),
        ],
        out_shape=[jax.ShapeDtypeStruct((b, t, HALF), out_dtype),
                   jax.ShapeDtypeStruct((b, CONV_W - 1, HALF), F32)],
        scratch_shapes=[pltpu.VMEM((bb, _HIST + tt, HALF), F32),
                        pltpu.VMEM((bb, SUBLANES - 1, tt + _HIST - SUBLANES, HALF), F32)],
        compiler_params=_cp("parallel", "arbitrary"),
    )(u, buf, taps, cb.reshape(1, HALF), lg.reshape(1, HALF), lb.reshape(1, HALF))


def _cumsum_lanes(x):
    lane = lax.broadcasted_iota(jnp.int32, x.shape, 1)
    sh = 1
    while sh < x.shape[1]:
        x = x + jnp.where(lane >= sh, pltpu.roll(x, sh, 1), 0.0)
        sh *= 2
    return x


def _mlstm_kernel(q_ref, k_ref, v_ref, o_ref, g_ref, gb_ref, c0_ref, n0_ref, m0_ref,
                  h_ref, c_ref, n_ref, m_ref, *, lr, bb):
    L = MLSTM_CHUNK
    ng = 2 * H_B

    @pl.when(pl.program_id(1) == 0)
    def _():
        c_ref[...] = c0_ref[...]
        n_ref[...] = n0_ref[...]
        m_ref[...] = m0_ref[...]

    def rows(ref, s):
        x = ref[s]
        if lr < L:
            x = jnp.concatenate([x.astype(F32), jnp.zeros((L - lr, HALF), F32)], axis=0)
        return x.astype(BF16)

    g = g_ref[...].reshape(bb * ng, L) + gb_ref[...]
    row = lax.broadcasted_iota(jnp.int32, g.shape, 0)
    is_li = row % ng < H_B
    bcum = _cumsum_lanes(jnp.where(is_li, 0.0, _log_sigmoid_neg(-g)))
    ab = jnp.where(is_li, g - pltpu.roll(bcum, (bb * ng) - H_B, 0), bcum)
    cols = jnp.transpose(jnp.concatenate([ab, jnp.zeros((L - bb * ng, L), F32)], axis=0))
    tpos = lax.broadcasted_iota(jnp.int32, (L, L), 0)
    spos = lax.broadcasted_iota(jnp.int32, (L, L), 1)
    causal = spos <= tpos
    for s in range(bb):
        q, k, v = rows(q_ref, s), rows(k_ref, s), rows(v_ref, s)
        outs = []
        for h in range(H_B):
            sl = slice(h * DH_B, (h + 1) * DH_B)
            qh, kh, vh = q[:, sl], k[:, sl], v[:, sl]
            ia, ib = s * ng + h, s * ng + H_B + h
            a_row = ab[ia:ia + 1, :]
            a_col = cols[:, ia:ia + 1]
            b_col = cols[:, ib:ib + 1]
            b_last = ab[ib:ib + 1, L - 1:L]
            m_old = m_ref[s, h:h + 1, 0:1]
            c_old = c_ref[s, h]
            n_old = n_ref[s, h:h + 1, :]
            dmat = jnp.where(causal, b_col + a_row, NEG_INF)
            inter = b_col + m_old
            mt = jnp.maximum(inter, jnp.max(dmat, axis=-1, keepdims=True))
            w_inter = jnp.exp(inter - mt)
            sc = _dot_nt(qh, kh) * jnp.exp(dmat - mt)
            num = w_inter * _dot(qh, c_old.astype(BF16)) + _dot(sc.astype(BF16), vh)
            den = (w_inter * jnp.sum(qh.astype(F32) * n_old, axis=-1, keepdims=True)
                   + jnp.sum(sc, axis=-1, keepdims=True))
            outs.append(num / jnp.maximum(jnp.abs(den), jnp.exp(-mt)))
            m_new = mt[L - 1:L, :]
            decay = jnp.exp(b_last + m_old - m_new)
            kw = kh.astype(F32) * jnp.exp(a_col + b_last - m_new)
            c_ref[s, h] = decay * c_old + _dot_tn(kw.astype(BF16), vh)
            n_ref[s, h:h + 1, :] = decay * n_old + jnp.sum(kw, axis=0, keepdims=True)
            m_ref[s, h:h + 1, :] = jnp.broadcast_to(m_new, (1, LANES))
        hs = jnp.concatenate(outs, axis=1)
        h_ref[s] = (jax.nn.sigmoid(o_ref[s]) * hs[0:lr]).astype(h_ref.dtype)


def _mlstm(q, k, v, o, gt, gate_b, c0, n0, m0, out_dtype):
    b, t, _ = q.shape
    L = MLSTM_CHUNK
    lr = min(L, t)
    nc = t // lr
    m0b = jnp.broadcast_to(jnp.pad(m0, ((0, 0), (0, SUBLANES - H_B)))[:, :, None], (b, SUBLANES, LANES))
    bb = math.gcd(b, MLSTM_SEQS)
    gb = jnp.broadcast_to(jnp.tile(gate_b, bb).reshape(bb * 2 * H_B, 1), (bb * 2 * H_B, L))
    blk = pl.BlockSpec((bb, lr, HALF), lambda i, j: (i, j, 0))
    st = lambda *s: pl.BlockSpec((bb,) + s, lambda i, j: (i,) + (0,) * len(s))
    h, c1, n1, m1 = pl.pallas_call(
        functools.partial(_mlstm_kernel, lr=lr, bb=bb),
        grid=(b // bb, nc),
        in_specs=[blk, blk, blk, blk,
                  pl.BlockSpec((bb, 2 * H_B, L), lambda i, j: (i, 0, j)),
                  pl.BlockSpec((bb * 2 * H_B, L), lambda i, j: (0, 0)),
                  st(H_B, DH_B, DH_B), st(H_B, DH_B), st(SUBLANES, LANES)],
        out_specs=[blk, st(H_B, DH_B, DH_B), st(H_B, DH_B), st(SUBLANES, LANES)],
        out_shape=[jax.ShapeDtypeStruct((b, t, HALF), out_dtype),
                   jax.ShapeDtypeStruct((b, H_B, DH_B, DH_B), F32),
                   jax.ShapeDtypeStruct((b, H_B, DH_B), F32),
                   jax.ShapeDtypeStruct((b, SUBLANES, LANES), F32)],
        compiler_params=_cp("parallel", "arbitrary"),
    )(q, k, v, o, gt, gb, c0, n0, m0b)
    return h, c1, n1, m1[:, :H_B, 0]


def _t5_bucket(dist):
    n = jnp.maximum(dist, 0)
    exact = N_BUCKETS // 2
    nf = jnp.maximum(n, 1).astype(F32)
    large = exact + (jnp.log(nf / exact) / math.log(MAX_DIST / exact) * (N_BUCKETS - exact)).astype(jnp.int32)
    return jnp.where(n < exact, n, jnp.minimum(large, N_BUCKETS - 1))


def _bias_of_dist(rel_bias, dist):
    onehot = jax.nn.one_hot(_t5_bucket(dist), N_BUCKETS, dtype=F32)
    b = jnp.einsum('...k,kh->h...', onehot, rel_bias.astype(F32), precision=lax.Precision.HIGHEST)
    return jnp.where(dist >= 0, b, NEG_INF)


def _lambda(lp_ref, lam_init):
    lp = lp_ref[...]
    s1 = jnp.sum(lp[0:1] * lp[1:2], axis=-1, keepdims=True)
    s2 = jnp.sum(lp[2:3] * lp[3:4], axis=-1, keepdims=True)
    return jnp.exp(s1) - jnp.exp(s2) + lam_init


def _head_norm(x, hg, lam_init):
    return x * lax.rsqrt(jnp.mean(x * x, axis=-1, keepdims=True) + 1e-6) * hg * (1.0 - lam_init)


def _diff_attn_kernel(q_ref, k_ref, v_ref, bd_ref, bp_ref, bf_ref, lp_ref, hg_ref, o_ref, kb_ref, vb_ref,
                      *, lam_init):
    blk = DIFF_BLK
    i = pl.program_id(2)

    @pl.when(i == 0)
    def _():
        kb_ref[...] = k_ref[...].astype(BF16)
        vb_ref[...] = v_ref[...].astype(BF16)

    q = q_ref[...]
    lane = lax.broadcasted_iota(jnp.int32, q.shape, 1)
    zero = jnp.zeros_like(q)
    q2 = jnp.concatenate([jnp.where(lane < DH_C, q, zero), jnp.where(lane >= DH_C, q, zero)], axis=0)

    def tile(j, bias):
        off = pl.multiple_of(j * blk, blk)
        s = _dot_nt(q2, kb_ref[pl.ds(off, blk), :]) + bias
        return s, vb_ref[pl.ds(off, blk), :]

    def update(carry, s, vblk):
        m, l, acc = carry
        mn = jnp.maximum(m, jnp.max(s, axis=-1, keepdims=True))
        p = jnp.exp(s - mn)
        al = jnp.exp(m - mn)
        return mn, al * l + jnp.sum(p, axis=-1, keepdims=True), al * acc + _dot(p.astype(BF16), vblk)

    s, vblk = tile(i, bd_ref[0])
    m = jnp.max(s, axis=-1, keepdims=True)
    p = jnp.exp(s - m)
    carry = (m, jnp.sum(p, axis=-1, keepdims=True), _dot(p.astype(BF16), vblk))

    def prev_step(c):
        return update(c, *tile(i - 1, bp_ref[0]))

    carry = lax.cond(i >= 1, prev_step, lambda c: c, carry)

    def far_step(j, c):
        return update(c, *tile(j, bf_ref[0]))

    m, l, acc = lax.fori_loop(0, jnp.maximum(i - 1, 0), far_step, carry)
    o = acc / l
    o = o[0:blk] - _lambda(lp_ref, lam_init) * o[blk:2 * blk]
    o_ref[...] = _head_norm(o, hg_ref[0], lam_init).astype(o_ref.dtype)


def _diff_attention(cq, ck, cv, b, t, rel_bias, lam_p, head_g, lam_init):
    blk = DIFF_BLK
    nq = t // blk
    r = jnp.arange(blk, dtype=jnp.int32)
    d0 = r[:, None] - r[None, :]
    bd = jnp.tile(_bias_of_dist(rel_bias, d0), (1, 2, 1))
    bp = jnp.tile(_bias_of_dist(rel_bias, d0 + blk), (1, 2, 1))
    assert blk + 1 >= MAX_DIST
    bfar = _bias_of_dist(rel_bias, jnp.full((1, blk), 2 * blk, jnp.int32))
    per_head = lambda *s: pl.BlockSpec((1,) + s, lambda bi, h, i: (h, 0, 0))
    return pl.pallas_call(
        functools.partial(_diff_attn_kernel, lam_init=lam_init),
        grid=(b, H_C, nq),
        in_specs=[
            pl.BlockSpec((blk, LANES), lambda bi, h, i: (bi * nq + i, h)),
            pl.BlockSpec((t, LANES), lambda bi, h, i: (bi, h)),
            pl.BlockSpec((t, LANES), lambda bi, h, i: (bi, h)),
            per_head(2 * blk, blk), per_head(2 * blk, blk), per_head(1, blk),
            pl.BlockSpec((4, DH_C), lambda bi, h, i: (0, 0)),
            per_head(1, LANES),
        ],
        out_specs=pl.BlockSpec((blk, LANES), lambda bi, h, i: (bi * nq + i, h)),
        out_shape=jax.ShapeDtypeStruct((b * t, HALF), BF16),
        scratch_shapes=[pltpu.VMEM((t, LANES), BF16), pltpu.VMEM((t, LANES), BF16)],
        compiler_params=_cp("parallel", "parallel", "arbitrary"),
    )(cq, ck, cv, bd, bp, bfar, lam_p, head_g.reshape(H_C, 1, LANES))


def _sb_tile(q2, kblk, vblk, upper, r, mask):
    z = _dot_nt(q2, kblk)
    lk = _log_sigmoid_neg(z)
    if mask is not None:
        lk = jnp.where(mask, lk, 0.0)
    hi, lo = _split_bf16(lk)
    after = _dot(hi, upper) + _dot(lo, upper) + r
    w = jnp.exp(lk + z + after)
    if mask is not None:
        w = jnp.where(mask, w, 0.0)
    return _dot(w.astype(BF16), vblk), r + jnp.sum(lk, axis=-1, keepdims=True)


def _strict_upper(n):
    j = lax.broadcasted_iota(jnp.int32, (n, n), 0)
    s = lax.broadcasted_iota(jnp.int32, (n, n), 1)
    return jnp.where(j > s, 1.0, 0.0).astype(BF16)


def _sb_attn_kernel(q_ref, k_ref, v_ref, o_ref, kb_ref, vb_ref):
    blk = SB_BLK
    i = pl.program_id(2)

    @pl.when(i == 0)
    def _():
        kb_ref[...] = k_ref[...].astype(BF16)
        vb_ref[...] = v_ref[...].astype(BF16)

    q = q_ref[...]
    lane = lax.broadcasted_iota(jnp.int32, q.shape, 1)
    zero = jnp.zeros_like(q)
    q2 = jnp.concatenate([jnp.where(lane < DH_D, q, zero), jnp.where(lane >= DH_D, q, zero)], axis=0)
    upper = _strict_upper(blk)
    tpos = lax.broadcasted_iota(jnp.int32, (2 * blk, blk), 0) % blk
    spos = lax.broadcasted_iota(jnp.int32, (2 * blk, blk), 1)

    def tile(j, r, mask):
        off = pl.multiple_of(j * blk, blk)
        return _sb_tile(q2, kb_ref[pl.ds(off, blk), :], vb_ref[pl.ds(off, blk), :], upper, r, mask)

    acc, r = tile(i, jnp.zeros((2 * blk, 1), F32), spos < tpos)

    def cond(c):
        j, _, r = c
        return jnp.logical_and(j >= 0, jnp.max(r) > SB_DEAD)

    def body(c):
        j, acc, r = c
        pv, r = tile(j, r, None)
        return j - 1, acc + pv, r

    _, acc, _ = lax.while_loop(cond, body, (i - 1, acc, r))
    o_ref[...] = jnp.where(lane < DH_D, acc[0:blk], acc[blk:2 * blk]).astype(o_ref.dtype)


def _sb_attention(sq, sk, sv, b, t):
    blk = SB_BLK
    nq = t // blk
    return pl.pallas_call(
        _sb_attn_kernel,
        grid=(b, HALF // LANES, nq),
        in_specs=[
            pl.BlockSpec((blk, LANES), lambda bi, h, i: (bi * nq + i, h)),
            pl.BlockSpec((t, LANES), lambda bi, h, i: (bi, h)),
            pl.BlockSpec((t, LANES), lambda bi, h, i: (bi, h)),
        ],
        out_specs=pl.BlockSpec((blk, LANES), lambda bi, h, i: (bi * nq + i, h)),
        out_shape=jax.ShapeDtypeStruct((b * t, HALF), BF16),
        scratch_shapes=[pltpu.VMEM((t, LANES), BF16), pltpu.VMEM((t, LANES), BF16)],
        compiler_params=_cp("parallel", "parallel", "arbitrary"),
    )(sq, sk, sv)


_DEC_ROWS = 64


def _dec_attn_kernel(pt_ref, cq_ref, sq_ref, ckf_ref, cvf_ref, sk_ref, sv_ref, bias_ref, bnew_ref, lp_ref, hg_ref,
                     *rest, ts, lam_init):
    np_ = DEC_PAGES
    pages = rest[:4 * np_]
    oc_ref, os_ref, qa_ref, qs_ref, m_ref, l_ref, r_ref, accd_ref, accs_ref = rest[4 * np_:]
    g = pl.program_id(1)
    psz = pages[2].shape[2]
    upper = _strict_upper(psz)

    def diff_update(tiles):
        m = m_ref[...]
        mn = m
        for s, _ in tiles:
            mn = jnp.maximum(mn, jnp.max(s, axis=-1, keepdims=True))
        al = jnp.exp(m - mn)
        l = al * l_ref[...]
        acc = al * accd_ref[...]
        for s, v in tiles:
            p = jnp.exp(s - mn)
            l = l + jnp.sum(p, axis=-1, keepdims=True)
            acc = acc + _dot(p.astype(BF16), v)
        m_ref[...] = mn
        l_ref[...] = l
        accd_ref[...] = acc

    def sb_update(tiles):
        r = r_ref[...]
        acc = accs_ref[...]
        for z, pv, mask in tiles:
            lk = _log_sigmoid_neg(z)
            if mask is not None:
                lk = jnp.where(mask, lk, 0.0)
            hi, lo = _split_bf16(lk)
            w = jnp.exp(lk + z + _dot(hi, upper) + _dot(lo, upper) + r)
            if mask is not None:
                w = jnp.where(mask, w, 0.0)
            acc = acc + pv(w.astype(BF16))
            r = r + jnp.sum(lk, axis=-1, keepdims=True)
        r_ref[...] = r
        accs_ref[...] = acc

    def pad_rows(x):
        return jnp.concatenate([x, jnp.zeros((psz - x.shape[0], x.shape[1]), F32)], axis=0).astype(BF16)

    @pl.when(g == 0)
    def _():
        cq = cq_ref[0]
        sq = sq_ref[0]
        half = lax.broadcasted_iota(jnp.int32, (ts, LANES), 1) >= DH_C
        qa_ref[...] = jnp.concatenate(
            [jnp.where(half if c % 2 else jnp.logical_not(half), cq[:, (c // 2) * LANES:(c // 2 + 1) * LANES], 0.0)
             for c in range(2 * H_C)], axis=0).astype(BF16)
        lane = lax.broadcasted_iota(jnp.int32, (ts, HALF), 1)
        qs_ref[...] = jnp.concatenate(
            [jnp.where((lane >= c * DH_D) & (lane < (c + 1) * DH_D), sq, 0.0) for c in range(H_D)],
            axis=0).astype(BF16)
        m_ref[...] = jnp.full_like(m_ref, NEG_INF)
        l_ref[...] = jnp.zeros_like(l_ref)
        r_ref[...] = jnp.zeros_like(r_ref)
        accd_ref[...] = jnp.zeros_like(accd_ref)
        accs_ref[...] = jnp.zeros_like(accs_ref)
        diff_update([(_dot_nt(qa_ref[...], pad_rows(ckf_ref[0])) + bnew_ref[...], pad_rows(cvf_ref[0]))])
        tq = lax.broadcasted_iota(jnp.int32, (_DEC_ROWS, psz), 0) % ts
        kpos = lax.broadcasted_iota(jnp.int32, (_DEC_ROWS, psz), 1)
        sv_new = pad_rows(sv_ref[0])
        sb_update([(_dot_nt(qs_ref[...], pad_rows(sk_ref[0])), lambda w: _dot(w, sv_new), kpos < tq)])

    qa = qa_ref[...]
    diff_update([(_dot_nt(qa, pages[4 * p][0].astype(BF16)) + bias_ref[p], pages[4 * p + 1][0].astype(BF16))
                 for p in range(np_)])

    @pl.when(jnp.max(r_ref[...]) > SB_DEAD)
    def _():
        qs = qs_ref[...]
        sb_update([(_dot(qs, pages[4 * p + 2][0].astype(BF16)),
                    functools.partial(_dot_nt, b=pages[4 * p + 3][0].astype(BF16)), None)
                   for p in range(np_)])

    @pl.when(g == pl.num_programs(1) - 1)
    def _():
        lam = _lambda(lp_ref, lam_init)
        o = accd_ref[...] / l_ref[...]
        for h in range(H_C):
            r0 = h * 2 * ts
            oh = o[r0:r0 + ts] - lam * o[r0 + ts:r0 + 2 * ts]
            oc_ref[0, :, h * LANES:(h + 1) * LANES] = _head_norm(oh, hg_ref[h:h + 1, :], lam_init)
        acc = accs_ref[...]
        lane = lax.broadcasted_iota(jnp.int32, (ts, LANES), 1)
        for pr in range(H_D // 2):
            sl = slice(pr * LANES, (pr + 1) * LANES)
            r0 = pr * 2 * ts
            os_ref[0, :, sl] = jnp.where(lane < DH_D, acc[r0:r0 + ts, sl], acc[r0 + ts:r0 + 2 * ts, sl])


def _decode_attention(cq, sq, new_rows, caches, page_table, rel_bias, lam_p, head_g, lam_init):
    b, ts, _ = cq.shape
    n_pages = page_table.shape[1]
    psz = caches[2].shape[2]
    past = n_pages * psz
    np_ = DEC_PAGES
    nkn = LANES // H_C
    assert n_pages % np_ == 0 and _DEC_ROWS == 2 * H_C * ts == H_D * ts and ts <= nkn
    ck, cv, sk, sv = new_rows
    ckf = ck.reshape(b, ts * H_C, LANES)
    cvf = cv.reshape(b, ts * H_C, LANES)

    def table(base):
        base = jnp.moveaxis(base, 0, -3)
        own = jnp.arange(H_C)[:, None, None, None] == jnp.arange(H_C)[None, None, None, :]
        tab = jnp.where(own, base[..., None], NEG_INF)
        tab = jnp.broadcast_to(tab[..., :, None, :, :, :], tab.shape[:-3] + (2,) + tab.shape[-3:])
        return tab.reshape(tab.shape[:-5] + (_DEC_ROWS, tab.shape[-2] * H_C))

    tq = jnp.arange(ts, dtype=jnp.int32)
    kpos = (jnp.arange(n_pages - 1, -1, -1, dtype=jnp.int32)[:, None] * psz
            + jnp.arange(psz, dtype=jnp.int32)[None, :])
    dist = past + tq[None, :, None] - kpos[:, None, :]
    bias = table(_bias_of_dist(rel_bias, dist))
    knew = jnp.arange(nkn, dtype=jnp.int32)
    dnew = jnp.where(knew[None, :] < ts, tq[:, None] - knew[None, :], -1)
    bnew = table(_bias_of_dist(rel_bias, dnew))

    row = pl.BlockSpec((1, ts, HALF), lambda bi, g, pt: (bi, 0, 0))
    rowf = pl.BlockSpec((1, ts * H_C, LANES), lambda bi, g, pt: (bi, 0, 0))

    def page_spec(p):
        return pl.BlockSpec((1, HALF, LANES),
                            lambda bi, g, pt: (pt[bi * n_pages + n_pages - 1 - (g * np_ + p)], 0, 0))

    page_specs, page_args = [], []
    for p in range(np_):
        for c in caches:
            page_specs.append(page_spec(p))
            page_args.append(c)
    oc, os_ = pl.pallas_call(
        functools.partial(_dec_attn_kernel, ts=ts, lam_init=lam_init),
        grid_spec=pltpu.PrefetchScalarGridSpec(
            num_scalar_prefetch=1,
            grid=(b, n_pages // np_),
            in_specs=[row, row, rowf, rowf, row, row,
                      pl.BlockSpec((np_, _DEC_ROWS, psz * H_C), lambda bi, g, pt: (g, 0, 0)),
                      pl.BlockSpec((_DEC_ROWS, LANES), lambda bi, g, pt: (0, 0)),
                      pl.BlockSpec((4, DH_C), lambda bi, g, pt: (0, 0)),
                      pl.BlockSpec((H_C, LANES), lambda bi, g, pt: (0, 0)),
                      ] + page_specs,
            out_specs=[row, row],
            scratch_shapes=[
                pltpu.VMEM((_DEC_ROWS, LANES), BF16), pltpu.VMEM((_DEC_ROWS, HALF), BF16),
                pltpu.VMEM((_DEC_ROWS, 1), F32), pltpu.VMEM((_DEC_ROWS, 1), F32), pltpu.VMEM((_DEC_ROWS, 1), F32),
                pltpu.VMEM((_DEC_ROWS, LANES), F32), pltpu.VMEM((_DEC_ROWS, HALF), F32),
            ],
        ),
        out_shape=[jax.ShapeDtypeStruct((b, ts, HALF), F32)] * 2,
        compiler_params=_cp("parallel", "arbitrary"),
    )(page_table.reshape(-1), cq, sq, ckf, cvf, sk, sv, bias, bnew, lam_p, head_g.reshape(H_C, LANES), *page_args)
    return oc, os_


def _trunk(x, p, even_states, odd_past, page_table, W):
    b, t, d = x.shape
    m = b * t
    prompt = odd_past is None
    act = BF16 if prompt else F32
    h = x.reshape(m, d)
    depth = p.shape[0]
    new_even, new_odd = [], []
    for l in range(depth):
        j = l // 2
        h = _ffn_half(h, W['ffn_norm1'][l], W['ffn1_wi'][l], W['ffn1_wo'][l])
        if l % 2 == 0:
            buf, c0, n0, m0 = even_states[j]
            u, q, k, v, o, gt = _inproj_even(h, W['mix_norm'][l], W['ev_w_in'][j], W['ev_w_gt'][j], act)
            a_out, buf1 = _conv_module(u.reshape(b, t, HALF), buf, W['ev_conv_w'][j], W['ev_conv_b'][j],
                                       W['ev_ln_g'][j], W['ev_ln_b'][j], act)
            gt = jnp.moveaxis(gt.reshape(2 * H_B, b, t), 1, 0)
            if t < MLSTM_CHUNK:
                padv = jnp.where(jnp.arange(2 * H_B) < H_B, NEG_INF, -NEG_INF).astype(F32)
                gt = jnp.concatenate(
                    [gt, jnp.broadcast_to(padv[None, :, None], (b, 2 * H_B, MLSTM_CHUNK - t))], axis=2)
            r3 = lambda a: a.reshape(b, t, HALF)
            b_out, c1, n1, m1 = _mlstm(r3(q), r3(k), r3(v), r3(o), gt, W['ev_gate_b'][j], c0, n0, m0, act)
            new_even.append((buf1, c1, n1, m1))
            h = _outproj(h, a_out.reshape(m, HALF), b_out.reshape(m, HALF), W['ev_w_out'][j])
        else:
            lam_init = 0.8 - 0.6 * math.exp(-0.3 * l)
            cq, ck, cv, sq, sk, sv = _inproj_odd(h, W['mix_norm'][l], W['od_w_in'][j], act)
            new_odd.append((ck, cv, sk, sv))
            if prompt:
                oc = _diff_attention(cq, ck, cv, b, t, W['rel_bias'], W['od_lambda'][j], W['od_head_g'][j], lam_init)
                os_ = _sb_attention(sq, sk, sv, b, t)
            else:
                r3 = lambda a: a.reshape(b, t, HALF)
                oc, os_ = _decode_attention(r3(cq), r3(sq), [r3(a) for a in (ck, cv, sk, sv)], odd_past[j],
                                            page_table[j], W['rel_bias'], W['od_lambda'][j], W['od_head_g'][j],
                                            lam_init)
            h = _outproj(h, oc.reshape(m, HALF), os_.reshape(m, HALF), W['od_w_out'][j])
        h = _ffn_half(h, W['ffn_norm2'][l], W['ffn2_wi'][l], W['ffn2_wo'][l])
        h = _ple(h, W['ple_norm'][l], W['ple_wg'][l], p[l].reshape(m, -1), W['ple_wp'][l], W['final_norm'],
                 final=(l == depth - 1))
    return h.reshape(b, t, d), new_even, new_odd


def kernel(x_prompt, x_sample, p_prompt, p_sample, state_conv, state_mlstm_C, state_mlstm_n, state_mlstm_m, cache_diff_k, cache_diff_v, cache_sb_k, cache_sb_v, page_table, ffn_norm1, ffn1_wi, ffn1_wo, mix_norm, ffn_norm2, ffn2_wi, ffn2_wo, ple_norm, ple_wg, ple_wp, ev_w_in, ev_conv_w, ev_conv_b, ev_ln_g, ev_ln_b, ev_gate_b, ev_w_out, od_w_in, od_lambda, od_head_g, od_w_out, rel_bias, final_norm):
    bf = lambda a: a.astype(BF16)
    n_even, n_odd = ev_w_in.shape[0], od_w_in.shape[0]
    W = dict(ffn_norm1=ffn_norm1, ffn1_wi=bf(ffn1_wi), ffn1_wo=bf(ffn1_wo), mix_norm=mix_norm,
             ffn_norm2=ffn_norm2, ffn2_wi=bf(ffn2_wi), ffn2_wo=bf(ffn2_wo),
             ple_norm=ple_norm, ple_wg=bf(ple_wg), ple_wp=bf(ple_wp),
             ev_w_in=bf(ev_w_in[:, :, :6 * HALF]), ev_w_gt=bf(jnp.swapaxes(ev_w_in[:, :, 6 * HALF:], 1, 2)),
             ev_conv_w=ev_conv_w, ev_conv_b=ev_conv_b, ev_ln_g=ev_ln_g, ev_ln_b=ev_ln_b,
             ev_gate_b=ev_gate_b, ev_w_out=bf(ev_w_out),
             od_w_in=bf(od_w_in), od_lambda=od_lambda, od_head_g=od_head_g, od_w_out=bf(od_w_out),
             rel_bias=rel_bias, final_norm=final_norm)
    bp, tp = x_prompt.shape[0], x_prompt.shape[1]
    bs, ts = x_sample.shape[0], x_sample.shape[1]
    even_p = [(jnp.zeros((bp, CONV_W - 1, HALF), F32), jnp.zeros((bp, H_B, DH_B, DH_B), F32),
               jnp.zeros((bp, H_B, DH_B), F32), jnp.zeros((bp, H_B), F32)) for _ in range(n_even)]
    y_prompt, ev_p, od_p = _trunk(x_prompt, p_prompt, even_p, None, None, W)
    even_s = [(state_conv[j], state_mlstm_C[j], state_mlstm_n[j], state_mlstm_m[j]) for j in range(n_even)]
    n_pool, psz = cache_diff_k.shape[1], cache_diff_k.shape[2]
    pool_d = lambda c: c.reshape(n_odd * n_pool, psz * H_C, 2 * DH_C)
    pool_s = lambda c: jnp.transpose(c, (0, 1, 3, 4, 2)).reshape(n_odd * n_pool, H_D * DH_D, psz)
    caches = (pool_d(cache_diff_k), pool_d(cache_diff_v), pool_s(cache_sb_k), pool_s(cache_sb_v))
    tables = [page_table + j * n_pool for j in range(n_odd)]
    y_sample, ev_s, od_s = _trunk(x_sample, p_sample, even_s, [caches] * n_odd, tables, W)
    stack = lambda states, i, shape: jnp.stack([s[i].reshape(shape) for s in states])
    ev = lambda states, i: jnp.stack([s[i] for s in states])
    return (y_prompt, y_sample,
            ev(ev_p, 0), ev(ev_s, 0), ev(ev_p, 1), ev(ev_s, 1),
            ev(ev_p, 2), ev(ev_s, 2), ev(ev_p, 3), ev(ev_s, 3),
            stack(od_p, 0, (bp, tp, H_C, 2 * DH_C)), stack(od_s, 0, (bs, ts, H_C, 2 * DH_C)),
            stack(od_p, 1, (bp, tp, H_C, 2 * DH_C)), stack(od_s, 1, (bs, ts, H_C, 2 * DH_C)),
            stack(od_p, 2, (bp, tp, H_D, DH_D)), stack(od_s, 2, (bs, ts, H_D, DH_D)),
            stack(od_p, 3, (bp, tp, H_D, DH_D)), stack(od_s, 3, (bs, ts, H_D, DH_D)))
```

```python
import functools
import math

import jax
import jax.numpy as jnp
from jax import lax
from jax.experimental import pallas as pl
from jax.experimental.pallas import tpu as pltpu

F32 = jnp.float32
BF16 = jnp.bfloat16

LANES = 128
SUBLANES = 8
VMEM_LIMIT_BYTES = 56 * 1024 * 1024

D_MODEL = 1024
D_FF = 2816
HALF = 512
CONV_W = 31
H_B, DH_B = 4, 128
H_C, DH_C = 4, 64
H_D, DH_D = 8, 64
N_BUCKETS = 32
MAX_DIST = 128
MLSTM_CHUNK = 128
NEG_INF = -1e30
SB_DEAD = -104.0

ROW_TILE = 512
FF_CHUNK = 256
DIFF_BLK = 512
SB_BLK = 256
CONV_ROWS = 256
CONV_SUB = 32
CONV_SEQS = 16
MLSTM_SEQS = 4
DEC_PAGES = 4
DEC_DIFF_PAGES = 8


def _cp(*sem):
    return pltpu.CompilerParams(dimension_semantics=sem, vmem_limit_bytes=VMEM_LIMIT_BYTES)


def _rms(x, g, eps=1e-6):
    return x * lax.rsqrt(jnp.mean(x * x, axis=-1, keepdims=True) + eps) * g


def _dot(a, b):
    return jnp.dot(a, b, preferred_element_type=F32)


def _dot_nt(a, b):
    return lax.dot_general(a, b, (((1,), (1,)), ((), ())), preferred_element_type=F32)


def _dot_tn(a, b):
    return lax.dot_general(a, b, (((0,), (0,)), ((), ())), preferred_element_type=F32)


def _log_sigmoid_neg(z):
    return -(jnp.maximum(z, 0.0) + jnp.log1p(jnp.exp(-jnp.abs(z))))


def _split_bf16(x):
    hi = x.astype(BF16)
    lo = (x - hi.astype(F32)).astype(BF16)
    return hi, lo


def _ffn_kernel(x_ref, g_ref, wi_ref, wo_ref, o_ref, act_ref):
    x = x_ref[...]
    hn = _rms(x, g_ref[...]).astype(BF16)
    for c in range(D_FF // FF_CHUNK):
        lo = c * FF_CHUNK
        gate = _dot(hn, wi_ref[:, lo:lo + FF_CHUNK])
        up = _dot(hn, wi_ref[:, D_FF + lo:D_FF + lo + FF_CHUNK])
        act_ref[:, lo:lo + FF_CHUNK] = (gate * jax.nn.sigmoid(gate) * up).astype(BF16)
    o_ref[...] = x + 0.5 * _dot(act_ref[...], wo_ref[...])


def _ffn_half(h, g, wi, wo):
    m, d = h.shape
    tm = min(ROW_TILE, m)
    resident = lambda shape: pl.BlockSpec(shape, lambda i: (0, 0), pipeline_mode=pl.Buffered(1))
    return pl.pallas_call(
        _ffn_kernel,
        grid=(m // tm,),
        in_specs=[
            pl.BlockSpec((tm, d), lambda i: (i, 0)),
            pl.BlockSpec((1, d), lambda i: (0, 0)),
            resident((d, 2 * D_FF)),
            resident((D_FF, d)),
        ],
        out_specs=pl.BlockSpec((tm, d), lambda i: (i, 0)),
        out_shape=jax.ShapeDtypeStruct((m, d), F32),
        scratch_shapes=[pltpu.VMEM((tm, D_FF), BF16)],
        compiler_params=_cp("parallel"),
    )(h, g.reshape(1, d), wi, wo)


def _ple_kernel(x_ref, g_ref, wg_ref, p_ref, wp_ref, fg_ref, o_ref, *, final):
    x = x_ref[...]
    hn = _rms(x, g_ref[...]).astype(BF16)
    gate = jax.nn.sigmoid(_dot(hn, wg_ref[...]))
    h = x + gate * _dot(p_ref[...].astype(BF16), wp_ref[...])
    if final:
        h = _rms(h, fg_ref[...])
    o_ref[...] = h


def _ple(h, g, wg, p, wp, fg, final):
    m, d = h.shape
    tm = min(ROW_TILE, m)
    pd = p.shape[1]
    return pl.pallas_call(
        functools.partial(_ple_kernel, final=final),
        grid=(m // tm,),
        in_specs=[
            pl.BlockSpec((tm, d), lambda i: (i, 0)),
            pl.BlockSpec((1, d), lambda i: (0, 0)),
            pl.BlockSpec((d, d), lambda i: (0, 0)),
            pl.BlockSpec((tm, pd), lambda i: (i, 0)),
            pl.BlockSpec((pd, d), lambda i: (0, 0)),
            pl.BlockSpec((1, d), lambda i: (0, 0)),
        ],
        out_specs=pl.BlockSpec((tm, d), lambda i: (i, 0)),
        out_shape=jax.ShapeDtypeStruct((m, d), F32),
        compiler_params=_cp("parallel"),
    )(h, g.reshape(1, d), wg, p, wp, fg.reshape(1, d))


def _inproj_even_kernel(x_ref, g_ref, w_ref, wgt_ref, u_ref, q_ref, k_ref, v_ref, o_ref, gt_ref):
    hn = _rms(x_ref[...], g_ref[...]).astype(BF16)

    def col(c):
        return _dot(hn, w_ref[:, c * HALF:(c + 1) * HALF])

    u_ref[...] = col(0) * jax.nn.sigmoid(col(1))
    q_ref[...] = col(2).astype(q_ref.dtype)
    k_ref[...] = (col(3) * (DH_B ** -0.5)).astype(k_ref.dtype)
    v_ref[...] = col(4).astype(v_ref.dtype)
    o_ref[...] = col(5)
    gt_ref[...] = _dot_nt(wgt_ref[...], hn)


def _inproj_even(h, g, w, wgt, qkv_dtype):
    m, d = h.shape
    tm = min(ROW_TILE, m)
    row = lambda i: (i, 0)
    out = lambda dt: jax.ShapeDtypeStruct((m, HALF), dt)
    return pl.pallas_call(
        _inproj_even_kernel,
        grid=(m // tm,),
        in_specs=[
            pl.BlockSpec((tm, d), row),
            pl.BlockSpec((1, d), lambda i: (0, 0)),
            pl.BlockSpec((d, 6 * HALF), lambda i: (0, 0)),
            pl.BlockSpec((2 * H_B, d), lambda i: (0, 0)),
        ],
        out_specs=[pl.BlockSpec((tm, HALF), row)] * 5 + [pl.BlockSpec((2 * H_B, tm), lambda i: (0, i))],
        out_shape=[out(F32), out(qkv_dtype), out(qkv_dtype), out(qkv_dtype), out(F32),
                   jax.ShapeDtypeStruct((2 * H_B, m), F32)],
        compiler_params=_cp("parallel"),
    )(h, g.reshape(1, d), w, wgt)


def _inproj_odd_kernel(x_ref, g_ref, w_ref, cq_ref, ck_ref, cv_ref, sq_ref, sk_ref, sv_ref):
    hn = _rms(x_ref[...], g_ref[...]).astype(BF16)

    def col(c):
        return _dot(hn, w_ref[:, c * HALF:(c + 1) * HALF])

    cq_ref[...] = (col(0) * (DH_C ** -0.5)).astype(cq_ref.dtype)
    ck_ref[...] = col(1)
    cv_ref[...] = col(2)
    sq_ref[...] = (col(3) * (DH_D ** -0.5)).astype(sq_ref.dtype)
    sk_ref[...] = col(4)
    sv_ref[...] = col(5)


def _inproj_odd(h, g, w, q_dtype):
    m, d = h.shape
    tm = min(ROW_TILE, m)
    row = lambda i: (i, 0)
    out = lambda dt: jax.ShapeDtypeStruct((m, HALF), dt)
    return pl.pallas_call(
        _inproj_odd_kernel,
        grid=(m // tm,),
        in_specs=[
            pl.BlockSpec((tm, d), row),
            pl.BlockSpec((1, d), lambda i: (0, 0)),
            pl.BlockSpec((d, 6 * HALF), lambda i: (0, 0)),
        ],
        out_specs=[pl.BlockSpec((tm, HALF), row)] * 6,
        out_shape=[out(q_dtype), out(F32), out(F32), out(q_dtype), out(F32), out(F32)],
        compiler_params=_cp("parallel"),
    )(h, g.reshape(1, d), w)


def _outproj_kernel(x_ref, a_ref, b_ref, wa_ref, wb_ref, o_ref):
    o_ref[...] = (x_ref[...] + _dot(a_ref[...].astype(BF16), wa_ref[...])
                  + _dot(b_ref[...].astype(BF16), wb_ref[...]))


def _outproj(h, a, b, w):
    m, d = h.shape
    tm = min(ROW_TILE, m)
    row = lambda i: (i, 0)
    return pl.pallas_call(
        _outproj_kernel,
        grid=(m // tm,),
        in_specs=[
            pl.BlockSpec((tm, d), row),
            pl.BlockSpec((tm, HALF), row),
            pl.BlockSpec((tm, HALF), row),
            pl.BlockSpec((HALF, d), lambda i: (0, 0)),
            pl.BlockSpec((HALF, d), lambda i: (1, 0)),
        ],
        out_specs=pl.BlockSpec((tm, d), row),
        out_shape=jax.ShapeDtypeStruct((m, d), F32),
        compiler_params=_cp("parallel"),
    )(h, a, b, w, w)


_HIST = 32


def _conv_kernel(u_ref, buf_ref, taps_ref, cb_ref, lg_ref, lb_ref, a_ref, st_ref, win_ref, sh_ref, *, tt, sub, bb):
    t = pl.program_id(1)
    pad = _HIST - (CONV_W - 1)
    span = tt + _HIST - SUBLANES

    for s in range(bb):
        @pl.when(t == 0)
        def _(s=s):
            win_ref[s, 0:SUBLANES, :] = jnp.zeros((SUBLANES, HALF), F32)
            win_ref[s, pad:_HIST, :] = buf_ref[s]

        win_ref[s, _HIST:_HIST + tt, :] = u_ref[s]
        for r in range(1, SUBLANES):
            sh_ref[s, r - 1] = win_ref[s, pl.ds(r, span), :]
        for rb in range(tt // sub):
            acc = jnp.zeros((sub, HALF), F32)
            for w in range(CONV_W):
                a8, r = divmod(pad + w, SUBLANES)
                lo = rb * sub + a8 * SUBLANES
                src = win_ref[s, lo:lo + sub, :] if r == 0 else sh_ref[s, r - 1, lo:lo + sub, :]
                acc = acc + src * taps_ref[w:w + 1, :]
            c = acc + cb_ref[...]
            mu = jnp.mean(c, axis=-1, keepdims=True)
            var = jnp.mean(jnp.square(c - mu), axis=-1, keepdims=True)
            cn = (c - mu) * lax.rsqrt(var + 1e-5) * lg_ref[...] + lb_ref[...]
            a_ref[s, rb * sub:(rb + 1) * sub, :] = (cn * jax.nn.sigmoid(cn)).astype(a_ref.dtype)

        @pl.when(t == pl.num_programs(1) - 1)
        def _(s=s):
            st_ref[s] = win_ref[s, tt + pad:tt + _HIST, :]

        win_ref[s, 0:_HIST, :] = win_ref[s, tt:tt + _HIST, :]


def _conv_module(u, buf, taps, cb, lg, lb, out_dtype):
    b, t, _ = u.shape
    tt = min(CONV_ROWS, t)
    sub = min(CONV_SUB, tt)
    bb = 1 if t > tt else math.gcd(b, CONV_SEQS)
    vec = lambda: pl.BlockSpec((1, HALF), lambda i, j: (0, 0))
    return pl.pallas_call(
        functools.partial(_conv_kernel, tt=tt, sub=sub, bb=bb),
        grid=(b // bb, t // tt),
        in_specs=[
            pl.BlockSpec((bb, tt, HALF), lambda i, j: (i, j, 0)),
            pl.BlockSpec((bb, CONV_W - 1, HALF), lambda i, j: (i, 0, 0)),
            pl.BlockSpec((CONV_W, HALF), lambda i, j: (0, 0)),
            vec(), vec(), vec(),
        ],
        out_specs=[
            pl.BlockSpec((bb, tt, HALF), lambda i, j: (i, j, 0)),
            pl.BlockSpec((bb, CONV_W - 1, HALF), lambda i, j: (i, 0, 0)),
        ],
        out_shape=[jax.ShapeDtypeStruct((b, t, HALF), out_dtype),
                   jax.ShapeDtypeStruct((b, CONV_W - 1, HALF), F32)],
        scratch_shapes=[pltpu.VMEM((bb, _HIST + tt, HALF), F32),
                        pltpu.VMEM((bb, SUBLANES - 1, tt + _HIST - SUBLANES, HALF), F32)],
        compiler_params=_cp("parallel", "arbitrary"),
    )(u, buf, taps, cb.reshape(1, HALF), lg.reshape(1, HALF), lb.reshape(1, HALF))


def _cumsum_lanes(x):
    lane = lax.broadcasted_iota(jnp.int32, x.shape, 1)
    sh = 1
    while sh < x.shape[1]:
        x = x + jnp.where(lane >= sh, pltpu.roll(x, sh, 1), 0.0)
        sh *= 2
    return x


def _mlstm_kernel(q_ref, k_ref, v_ref, o_ref, g_ref, gb_ref, c0_ref, n0_ref, m0_ref,
                  h_ref, c_ref, n_ref, m_ref, *, lr, bb):
    L = MLSTM_CHUNK
    ng = 2 * H_B

    @pl.when(pl.program_id(1) == 0)
    def _():
        c_ref[...] = c0_ref[...]
        n_ref[...] = n0_ref[...]
        m_ref[...] = m0_ref[...]

    def rows(ref, s):
        x = ref[s]
        if lr < L:
            x = jnp.concatenate([x.astype(F32), jnp.zeros((L - lr, HALF), F32)], axis=0)
        return x.astype(BF16)

    g = g_ref[...].reshape(bb * ng, L) + gb_ref[...]
    row = lax.broadcasted_iota(jnp.int32, g.shape, 0)
    is_li = row % ng < H_B
    bcum = _cumsum_lanes(jnp.where(is_li, 0.0, _log_sigmoid_neg(-g)))
    ab = jnp.where(is_li, g - pltpu.roll(bcum, (bb * ng) - H_B, 0), bcum)
    cols = jnp.transpose(jnp.concatenate([ab, jnp.zeros((L - bb * ng, L), F32)], axis=0))
    tpos = lax.broadcasted_iota(jnp.int32, (L, L), 0)
    spos = lax.broadcasted_iota(jnp.int32, (L, L), 1)
    causal = spos <= tpos
    for s in range(bb):
        q, k, v = rows(q_ref, s), rows(k_ref, s), rows(v_ref, s)
        outs = []
        for h in range(H_B):
            sl = slice(h * DH_B, (h + 1) * DH_B)
            qh, kh, vh = q[:, sl], k[:, sl], v[:, sl]
            ia, ib = s * ng + h, s * ng + H_B + h
            a_row = ab[ia:ia + 1, :]
            a_col = cols[:, ia:ia + 1]
            b_col = cols[:, ib:ib + 1]
            b_last = ab[ib:ib + 1, L - 1:L]
            m_old = m_ref[s, h:h + 1, 0:1]
            c_old = c_ref[s, h]
            n_old = n_ref[s, h:h + 1, :]
            dmat = jnp.where(causal, b_col + a_row, NEG_INF)
            inter = b_col + m_old
            mt = jnp.maximum(inter, jnp.max(dmat, axis=-1, keepdims=True))
            w_inter = jnp.exp(inter - mt)
            sc = _dot_nt(qh, kh) * jnp.exp(dmat - mt)
            num = w_inter * _dot(qh, c_old.astype(BF16)) + _dot(sc.astype(BF16), vh)
            den = (w_inter * jnp.sum(qh.astype(F32) * n_old, axis=-1, keepdims=True)
                   + jnp.sum(sc, axis=-1, keepdims=True))
            outs.append(num / jnp.maximum(jnp.abs(den), jnp.exp(-mt)))
            m_new = mt[L - 1:L, :]
            decay = jnp.exp(b_last + m_old - m_new)
            kw = kh.astype(F32) * jnp.exp(a_col + b_last - m_new)
            c_ref[s, h] = decay * c_old + _dot_tn(kw.astype(BF16), vh)
            n_ref[s, h:h + 1, :] = decay * n_old + jnp.sum(kw, axis=0, keepdims=True)
            m_ref[s, h:h + 1, :] = jnp.broadcast_to(m_new, (1, LANES))
        hs = jnp.concatenate(outs, axis=1)
        h_ref[s] = (jax.nn.sigmoid(o_ref[s]) * hs[0:lr]).astype(h_ref.dtype)


def _mlstm(q, k, v, o, gt, gate_b, c0, n0, m0, out_dtype):
    b, t, _ = q.shape
    L = MLSTM_CHUNK
    lr = min(L, t)
    nc = t // lr
    m0b = jnp.broadcast_to(jnp.pad(m0, ((0, 0), (0, SUBLANES - H_B)))[:, :, None], (b, SUBLANES, LANES))
    bb = math.gcd(b, MLSTM_SEQS)
    gb = jnp.broadcast_to(jnp.tile(gate_b, bb).reshape(bb * 2 * H_B, 1), (bb * 2 * H_B, L))
    blk = pl.BlockSpec((bb, lr, HALF), lambda i, j: (i, j, 0))
    st = lambda *s: pl.BlockSpec((bb,) + s, lambda i, j: (i,) + (0,) * len(s))
    h, c1, n1, m1 = pl.pallas_call(
        functools.partial(_mlstm_kernel, lr=lr, bb=bb),
        grid=(b // bb, nc),
        in_specs=[blk, blk, blk, blk,
                  pl.BlockSpec((bb, 2 * H_B, L), lambda i, j: (i, 0, j)),
                  pl.BlockSpec((bb * 2 * H_B, L), lambda i, j: (0, 0)),
                  st(H_B, DH_B, DH_B), st(H_B, DH_B), st(SUBLANES, LANES)],
        out_specs=[blk, st(H_B, DH_B, DH_B), st(H_B, DH_B), st(SUBLANES, LANES)],
        out_shape=[jax.ShapeDtypeStruct((b, t, HALF), out_dtype),
                   jax.ShapeDtypeStruct((b, H_B, DH_B, DH_B), F32),
                   jax.ShapeDtypeStruct((b, H_B, DH_B), F32),
                   jax.ShapeDtypeStruct((b, SUBLANES, LANES), F32)],
        compiler_params=_cp("parallel", "arbitrary"),
    )(q, k, v, o, gt, gb, c0, n0, m0b)
    return h, c1, n1, m1[:, :H_B, 0]


def _t5_bucket(dist):
    n = jnp.maximum(dist, 0)
    exact = N_BUCKETS // 2
    nf = jnp.maximum(n, 1).astype(F32)
    large = exact + (jnp.log(nf / exact) / math.log(MAX_DIST / exact) * (N_BUCKETS - exact)).astype(jnp.int32)
    return jnp.where(n < exact, n, jnp.minimum(large, N_BUCKETS - 1))


def _bias_of_dist(rel_bias, dist):
    onehot = jax.nn.one_hot(_t5_bucket(dist), N_BUCKETS, dtype=F32)
    b = jnp.einsum('...k,kh->h...', onehot, rel_bias.astype(F32), precision=lax.Precision.HIGHEST)
    return jnp.where(dist >= 0, b, NEG_INF)


def _lambda(lp_ref, lam_init):
    lp = lp_ref[...]
    s1 = jnp.sum(lp[0:1] * lp[1:2], axis=-1, keepdims=True)
    s2 = jnp.sum(lp[2:3] * lp[3:4], axis=-1, keepdims=True)
    return jnp.exp(s1) - jnp.exp(s2) + lam_init


def _head_norm(x, hg, lam_init):
    return x * lax.rsqrt(jnp.mean(x * x, axis=-1, keepdims=True) + 1e-6) * hg * (1.0 - lam_init)


def _diff_attn_kernel(q_ref, k_ref, v_ref, bd_ref, bp_ref, lp_ref, hg_ref, o_ref, kb_ref, vt_ref, *, lam_init):
    blk = DIFF_BLK
    i = pl.program_id(2)

    @pl.when(i == 0)
    def _():
        kb_ref[...] = k_ref[...].astype(BF16)
        for c in range(vt_ref.shape[0]):
            vt_ref[c] = jnp.transpose(v_ref[c * blk:(c + 1) * blk, :]).astype(BF16)

    q = q_ref[...]
    lane = lax.broadcasted_iota(jnp.int32, q.shape, 1)
    zero = jnp.zeros_like(q)
    q2 = jnp.concatenate([jnp.where(lane < DH_C, q, zero), jnp.where(lane >= DH_C, q, zero)], axis=0)

    def scores(j):
        off = pl.multiple_of(j * blk, blk)
        return _dot_nt(kb_ref[pl.ds(off, blk), :], q2)

    def update(carry, s, vt):
        m, l, acc = carry
        mn = jnp.maximum(m, jnp.max(s, axis=0, keepdims=True))
        p = jnp.exp(s - mn)
        al = jnp.exp(m - mn)
        return mn, al * l + jnp.sum(p, axis=0, keepdims=True), al * acc + _dot(vt, p.astype(BF16))

    s = scores(i) + bd_ref[0]
    m = jnp.max(s, axis=0, keepdims=True)
    p = jnp.exp(s - m)
    carry = (m, jnp.sum(p, axis=0, keepdims=True), _dot(vt_ref[i], p.astype(BF16)))

    def prev_step(c):
        return update(c, scores(i - 1) + bp_ref[0], vt_ref[i - 1])

    carry = lax.cond(i >= 1, prev_step, lambda c: c, carry)

    def far_step(j, c):
        return update(c, scores(j), vt_ref[j])

    m, l, acc = lax.fori_loop(0, jnp.maximum(i - 1, 0), far_step, carry)
    o = acc / l
    o = jnp.transpose(o[:, 0:blk] - _lambda(lp_ref, lam_init) * o[:, blk:2 * blk])
    o_ref[...] = _head_norm(o, hg_ref[0], lam_init).astype(o_ref.dtype)


def _diff_attention(cq, ck, cv, b, t, rel_bias, lam_p, head_g, lam_init):
    blk = DIFF_BLK
    nq = t // blk
    r = jnp.arange(blk, dtype=jnp.int32)
    d0 = r[None, :] - r[:, None]
    assert blk + 1 >= MAX_DIST
    far = _bias_of_dist(rel_bias, jnp.full((1, 1), 2 * blk, jnp.int32))
    rel = lambda d: jnp.tile(jnp.where(d >= 0, _bias_of_dist(rel_bias, d) - far, NEG_INF), (1, 1, 2))
    bd = rel(d0)
    bp = rel(d0 + blk)
    per_head = lambda *s: pl.BlockSpec((1,) + s, lambda bi, h, i: (h, 0, 0))
    return pl.pallas_call(
        functools.partial(_diff_attn_kernel, lam_init=lam_init),
        grid=(b, H_C, nq),
        in_specs=[
            pl.BlockSpec((blk, LANES), lambda bi, h, i: (bi * nq + i, h)),
            pl.BlockSpec((t, LANES), lambda bi, h, i: (bi, h)),
            pl.BlockSpec((t, LANES), lambda bi, h, i: (bi, h)),
            per_head(blk, 2 * blk), per_head(blk, 2 * blk),
            pl.BlockSpec((4, DH_C), lambda bi, h, i: (0, 0)),
            per_head(1, LANES),
        ],
        out_specs=pl.BlockSpec((blk, LANES), lambda bi, h, i: (bi * nq + i, h)),
        out_shape=jax.ShapeDtypeStruct((b * t, HALF), BF16),
        scratch_shapes=[pltpu.VMEM((t, LANES), BF16), pltpu.VMEM((nq, LANES, blk), BF16)],
        compiler_params=_cp("parallel", "parallel", "arbitrary"),
    )(cq, ck, cv, bd, bp, lam_p, head_g.reshape(H_C, 1, LANES))


def _sb_tile(q2, kblk, vblk, upper, r, mask):
    z = _dot_nt(q2, kblk)
    lk = _log_sigmoid_neg(z)
    if mask is not None:
        lk = jnp.where(mask, lk, 0.0)
    hi, lo = _split_bf16(lk)
    after = _dot(hi, upper) + _dot(lo, upper) + r
    w = jnp.exp(lk + z + after)
    if mask is not None:
        w = jnp.where(mask, w, 0.0)
    return _dot(w.astype(BF16), vblk), r + jnp.sum(lk, axis=-1, keepdims=True)


def _strict_upper(n):
    j = lax.broadcasted_iota(jnp.int32, (n, n), 0)
    s = lax.broadcasted_iota(jnp.int32, (n, n), 1)
    return jnp.where(j > s, 1.0, 0.0).astype(BF16)


def _sb_attn_kernel(q_ref, k_ref, v_ref, o_ref, kb_ref, vb_ref):
    blk = SB_BLK
    i = pl.program_id(2)

    @pl.when(i == 0)
    def _():
        kb_ref[...] = k_ref[...].astype(BF16)
        vb_ref[...] = v_ref[...].astype(BF16)

    q = q_ref[...]
    lane = lax.broadcasted_iota(jnp.int32, q.shape, 1)
    zero = jnp.zeros_like(q)
    q2 = jnp.concatenate([jnp.where(lane < DH_D, q, zero), jnp.where(lane >= DH_D, q, zero)], axis=0)
    upper = _strict_upper(blk)
    tpos = lax.broadcasted_iota(jnp.int32, (2 * blk, blk), 0) % blk
    spos = lax.broadcasted_iota(jnp.int32, (2 * blk, blk), 1)

    def tile(j, r, mask):
        off = pl.multiple_of(j * blk, blk)
        return _sb_tile(q2, kb_ref[pl.ds(off, blk), :], vb_ref[pl.ds(off, blk), :], upper, r, mask)

    acc, r = tile(i, jnp.zeros((2 * blk, 1), F32), spos < tpos)

    def cond(c):
        j, _, r = c
        return jnp.logical_and(j >= 0, jnp.max(r) > SB_DEAD)

    def body(c):
        j, acc, r = c
        pv, r = tile(j, r, None)
        return j - 1, acc + pv, r

    _, acc, _ = lax.while_loop(cond, body, (i - 1, acc, r))
    o_ref[...] = jnp.where(lane < DH_D, acc[0:blk], acc[blk:2 * blk]).astype(o_ref.dtype)


def _sb_attention(sq, sk, sv, b, t):
    blk = SB_BLK
    nq = t // blk
    return pl.pallas_call(
        _sb_attn_kernel,
        grid=(b, HALF // LANES, nq),
        in_specs=[
            pl.BlockSpec((blk, LANES), lambda bi, h, i: (bi * nq + i, h)),
            pl.BlockSpec((t, LANES), lambda bi, h, i: (bi, h)),
            pl.BlockSpec((t, LANES), lambda bi, h, i: (bi, h)),
        ],
        out_specs=pl.BlockSpec((blk, LANES), lambda bi, h, i: (bi * nq + i, h)),
        out_shape=jax.ShapeDtypeStruct((b * t, HALF), BF16),
        scratch_shapes=[pltpu.VMEM((t, LANES), BF16), pltpu.VMEM((t, LANES), BF16)],
        compiler_params=_cp("parallel", "parallel", "arbitrary"),
    )(sq, sk, sv)


_DEC_ROWS = 64


def _pad_rows(x, n):
    return jnp.concatenate([x, jnp.zeros((n - x.shape[0], x.shape[1]), F32)], axis=0).astype(BF16)


def _dec_diff_kernel(pt_ref, cq_ref, ckf_ref, cvf_ref, bias_ref, bnew_ref, lp_ref, hg_ref, *rest,
                     ts, lam_init, npg):
    pages = rest[:2 * npg]
    oc_ref, qa_ref, m_ref, l_ref, acc_ref = rest[2 * npg:]
    g = pl.program_id(1)

    def update(tiles):
        m = m_ref[...]
        mn = m
        for s, _ in tiles:
            mn = jnp.maximum(mn, jnp.max(s, axis=-1, keepdims=True))
        al = jnp.exp(m - mn)
        l = al * l_ref[...]
        acc = al * acc_ref[...]
        for s, v in tiles:
            p = jnp.exp(s - mn)
            l = l + jnp.sum(p, axis=-1, keepdims=True)
            acc = acc + _dot(p.astype(BF16), v)
        m_ref[...] = mn
        l_ref[...] = l
        acc_ref[...] = acc

    @pl.when(g == 0)
    def _():
        cq = cq_ref[0]
        half = lax.broadcasted_iota(jnp.int32, (ts, LANES), 1) >= DH_C
        qa_ref[...] = jnp.concatenate(
            [jnp.where(half if c % 2 else jnp.logical_not(half), cq[:, (c // 2) * LANES:(c // 2 + 1) * LANES], 0.0)
             for c in range(2 * H_C)], axis=0).astype(BF16)
        m_ref[...] = jnp.full_like(m_ref, NEG_INF)
        l_ref[...] = jnp.zeros_like(l_ref)
        acc_ref[...] = jnp.zeros_like(acc_ref)
        update([(_dot_nt(qa_ref[...], _pad_rows(ckf_ref[0], LANES)) + bnew_ref[...], _pad_rows(cvf_ref[0], LANES))])

    qa = qa_ref[...]
    update([(_dot_nt(qa, pages[2 * p][0].astype(BF16)) + bias_ref[g * npg + p], pages[2 * p + 1][0].astype(BF16))
            for p in range(npg)])

    @pl.when(g == pl.num_programs(1) - 1)
    def _():
        lam = _lambda(lp_ref, lam_init)
        o = acc_ref[...] / l_ref[...]
        for h in range(H_C):
            r0 = h * 2 * ts
            oh = o[r0:r0 + ts] - lam * o[r0 + ts:r0 + 2 * ts]
            oc_ref[0, :, h * LANES:(h + 1) * LANES] = _head_norm(oh, hg_ref[h:h + 1, :], lam_init)


def _sb_queries(sq):
    lane = lax.broadcasted_iota(jnp.int32, sq.shape, 1)
    return jnp.concatenate(
        [jnp.where((lane >= c * DH_D) & (lane < (c + 1) * DH_D), sq, 0.0) for c in range(H_D)], axis=0).astype(BF16)


def _sb_fold(tiles, upper, r, acc):
    for z, pv, mask in tiles:
        lk = _log_sigmoid_neg(z)
        if mask is not None:
            lk = jnp.where(mask, lk, 0.0)
        hi, lo = _split_bf16(lk)
        w = jnp.exp(lk + z + _dot(hi, upper) + _dot(lo, upper) + r)
        if mask is not None:
            w = jnp.where(mask, w, 0.0)
        acc = acc + pv(w.astype(BF16))
        r = r + jnp.sum(lk, axis=-1, keepdims=True)
    return r, acc


def _sb_page_tiles(qs, pages):
    return [(_dot(qs, pages[2 * p][0].astype(BF16)), functools.partial(_dot_nt, b=pages[2 * p + 1][0].astype(BF16)),
             None) for p in range(len(pages) // 2)]


def _dec_sb_first_kernel(pt_ref, sq_ref, sk_ref, sv_ref, *rest, ts, npg):
    pages = rest[:2 * npg]
    acc_ref, r_ref, alive_ref = rest[2 * npg:]
    psz = pages[0].shape[2]
    upper = _strict_upper(psz)
    qs = _sb_queries(sq_ref[0])
    tq = lax.broadcasted_iota(jnp.int32, (_DEC_ROWS, psz), 0) % ts
    kpos = lax.broadcasted_iota(jnp.int32, (_DEC_ROWS, psz), 1)
    sv_new = _pad_rows(sv_ref[0], psz)
    tiles = [(_dot_nt(qs, _pad_rows(sk_ref[0], psz)), lambda w: _dot(w, sv_new), kpos < tq)]
    r, acc = _sb_fold(tiles + _sb_page_tiles(qs, pages), upper,
                      jnp.zeros((_DEC_ROWS, 1), F32), jnp.zeros((_DEC_ROWS, HALF), F32))
    acc_ref[0] = acc
    r_ref[0] = jnp.broadcast_to(r, (_DEC_ROWS, LANES))
    alive = jnp.max(r, axis=0, keepdims=True) > SB_DEAD
    alive_ref[0] = jnp.broadcast_to(jnp.where(alive, 1, 0), (SUBLANES, LANES)).astype(jnp.int32)


def _dec_sb_rest_kernel(pt_ref, al_ref, sq_ref, acc_in_ref, r_in_ref, *rest, ts, npg, nrest):
    pages = rest[:2 * nrest]
    os_ref, acc_ref, r_ref = rest[2 * nrest:]
    acc_ref[...] = acc_in_ref[0]
    r_ref[...] = r_in_ref[0][:, 0:1]

    @pl.when(al_ref[pl.program_id(0)] == 1)
    def _():
        qs = _sb_queries(sq_ref[0])
        upper = _strict_upper(pages[0].shape[2])
        for grp in range(nrest // npg):
            @pl.when(jnp.max(r_ref[...]) > SB_DEAD)
            def _(grp=grp):
                tiles = _sb_page_tiles(qs, pages[2 * npg * grp:2 * npg * (grp + 1)])
                r, acc = _sb_fold(tiles, upper, r_ref[...], acc_ref[...])
                r_ref[...] = r
                acc_ref[...] = acc

    acc = acc_ref[...]
    lane = lax.broadcasted_iota(jnp.int32, (ts, LANES), 1)
    for pr in range(H_D // 2):
        sl = slice(pr * LANES, (pr + 1) * LANES)
        r0 = pr * 2 * ts
        os_ref[0, :, sl] = jnp.where(lane < DH_D, acc[r0:r0 + ts, sl], acc[r0 + ts:r0 + 2 * ts, sl])


def _decode_attention(cq, sq, new_rows, caches, page_table, rel_bias, lam_p, head_g, lam_init):
    b, ts, _ = cq.shape
    n_pages = page_table.shape[1]
    psz = caches[2].shape[2]
    past = n_pages * psz
    npd = math.gcd(n_pages, DEC_DIFF_PAGES)
    nps = math.gcd(n_pages, DEC_PAGES)
    nrest = n_pages - nps
    nkn = LANES // H_C
    assert _DEC_ROWS == 2 * H_C * ts == H_D * ts and ts <= nkn
    ck, cv, sk, sv = new_rows
    ckf = ck.reshape(b, ts * H_C, LANES)
    cvf = cv.reshape(b, ts * H_C, LANES)
    dk, dv, skt, svt = caches
    pt = page_table.reshape(-1)

    def table(base):
        base = jnp.moveaxis(base, 0, -3)
        own = jnp.arange(H_C)[:, None, None, None] == jnp.arange(H_C)[None, None, None, :]
        tab = jnp.where(own, base[..., None], NEG_INF)
        tab = jnp.broadcast_to(tab[..., :, None, :, :, :], tab.shape[:-3] + (2,) + tab.shape[-3:])
        return tab.reshape(tab.shape[:-5] + (_DEC_ROWS, tab.shape[-2] * H_C))

    tq = jnp.arange(ts, dtype=jnp.int32)
    kpos = (jnp.arange(n_pages - 1, -1, -1, dtype=jnp.int32)[:, None] * psz
            + jnp.arange(psz, dtype=jnp.int32)[None, :])
    dist = past + tq[None, :, None] - kpos[:, None, :]
    bias = table(_bias_of_dist(rel_bias, dist))
    knew = jnp.arange(nkn, dtype=jnp.int32)
    dnew = jnp.where(knew[None, :] < ts, tq[:, None] - knew[None, :], -1)
    bnew = table(_bias_of_dist(rel_bias, dnew))

    page = lambda idx: pl.BlockSpec((1, HALF, LANES), idx)
    const = lambda shape: pl.BlockSpec(shape, lambda *_: (0,) * len(shape))

    row = pl.BlockSpec((1, ts, HALF), lambda bi, g, pt: (bi, 0, 0))
    rowf = pl.BlockSpec((1, ts * H_C, LANES), lambda bi, g, pt: (bi, 0, 0))
    dspecs, dargs = [], []
    for p in range(npd):
        for c in (dk, dv):
            dspecs.append(page(lambda bi, g, pt, p=p: (pt[bi * n_pages + n_pages - 1 - (g * npd + p)], 0, 0)))
            dargs.append(c)
    oc = pl.pallas_call(
        functools.partial(_dec_diff_kernel, ts=ts, lam_init=lam_init, npg=npd),
        grid_spec=pltpu.PrefetchScalarGridSpec(
            num_scalar_prefetch=1,
            grid=(b, n_pages // npd),
            in_specs=[row, rowf, rowf, const((n_pages, _DEC_ROWS, psz * H_C)), const((_DEC_ROWS, LANES)),
                      const((4, DH_C)), const((H_C, LANES))] + dspecs,
            out_specs=row,
            scratch_shapes=[pltpu.VMEM((_DEC_ROWS, LANES), BF16), pltpu.VMEM((_DEC_ROWS, 1), F32),
                            pltpu.VMEM((_DEC_ROWS, 1), F32), pltpu.VMEM((_DEC_ROWS, LANES), F32)],
        ),
        out_shape=jax.ShapeDtypeStruct((b, ts, HALF), F32),
        compiler_params=_cp("parallel", "arbitrary"),
    )(pt, cq, ckf, cvf, bias, bnew, lam_p, head_g.reshape(H_C, LANES), *dargs)

    row1 = pl.BlockSpec((1, ts, HALF), lambda bi, pt: (bi, 0, 0))
    per_seq = lambda *s: pl.BlockSpec((1,) + s, lambda bi, *_: (bi,) + (0,) * len(s))
    sspecs, sargs = [], []
    for p in range(nps):
        for c in (skt, svt):
            sspecs.append(page(lambda bi, pt, p=p: (pt[bi * n_pages + n_pages - 1 - p], 0, 0)))
            sargs.append(c)
    acc, r, alive = pl.pallas_call(
        functools.partial(_dec_sb_first_kernel, ts=ts, npg=nps),
        grid_spec=pltpu.PrefetchScalarGridSpec(
            num_scalar_prefetch=1,
            grid=(b,),
            in_specs=[row1, row1, row1] + sspecs,
            out_specs=[per_seq(_DEC_ROWS, HALF), per_seq(_DEC_ROWS, LANES), per_seq(SUBLANES, LANES)],
        ),
        out_shape=[jax.ShapeDtypeStruct((b, _DEC_ROWS, HALF), F32), jax.ShapeDtypeStruct((b, _DEC_ROWS, LANES), F32),
                   jax.ShapeDtypeStruct((b, SUBLANES, LANES), jnp.int32)],
        compiler_params=_cp("parallel"),
    )(pt, sq, sk, sv, *sargs)

    alive = alive[:, 0, 0]
    rspecs, rargs = [], []
    for p in range(nrest):
        for c in (skt, svt):
            rspecs.append(page(lambda bi, pt, al, p=p: (
                jnp.where(al[bi] == 1, pt[bi * n_pages + n_pages - 1 - nps - p], pt[0]), 0, 0)))
            rargs.append(c)
    os_ = pl.pallas_call(
        functools.partial(_dec_sb_rest_kernel, ts=ts, npg=nps, nrest=nrest),
        grid_spec=pltpu.PrefetchScalarGridSpec(
            num_scalar_prefetch=2,
            grid=(b,),
            in_specs=[pl.BlockSpec((1, ts, HALF), lambda bi, pt, al: (bi, 0, 0)),
                      per_seq(_DEC_ROWS, HALF), per_seq(_DEC_ROWS, LANES)] + rspecs,
            out_specs=pl.BlockSpec((1, ts, HALF), lambda bi, pt, al: (bi, 0, 0)),
            scratch_shapes=[pltpu.VMEM((_DEC_ROWS, HALF), F32), pltpu.VMEM((_DEC_ROWS, 1), F32)],
        ),
        out_shape=jax.ShapeDtypeStruct((b, ts, HALF), F32),
        compiler_params=_cp("arbitrary"),
    )(pt, alive, sq, acc, r, *rargs)
    return oc, os_


def _trunk(x, p, even_states, odd_past, page_table, W):
    b, t, d = x.shape
    m = b * t
    prompt = odd_past is None
    act = BF16 if prompt else F32
    h = x.reshape(m, d)
    depth = p.shape[0]
    new_even, new_odd = [], []
    for l in range(depth):
        j = l // 2
        h = _ffn_half(h, W['ffn_norm1'][l], W['ffn1_wi'][l], W['ffn1_wo'][l])
        if l % 2 == 0:
            buf, c0, n0, m0 = even_states[j]
            u, q, k, v, o, gt = _inproj_even(h, W['mix_norm'][l], W['ev_w_in'][j], W['ev_w_gt'][j], act)
            a_out, buf1 = _conv_module(u.reshape(b, t, HALF), buf, W['ev_conv_w'][j], W['ev_conv_b'][j],
                                       W['ev_ln_g'][j], W['ev_ln_b'][j], act)
            gt = jnp.moveaxis(gt.reshape(2 * H_B, b, t), 1, 0)
            if t < MLSTM_CHUNK:
                padv = jnp.where(jnp.arange(2 * H_B) < H_B, NEG_INF, -NEG_INF).astype(F32)
                gt = jnp.concatenate(
                    [gt, jnp.broadcast_to(padv[None, :, None], (b, 2 * H_B, MLSTM_CHUNK - t))], axis=2)
            r3 = lambda a: a.reshape(b, t, HALF)
            b_out, c1, n1, m1 = _mlstm(r3(q), r3(k), r3(v), r3(o), gt, W['ev_gate_b'][j], c0, n0, m0, act)
            new_even.append((buf1, c1, n1, m1))
            h = _outproj(h, a_out.reshape(m, HALF), b_out.reshape(m, HALF), W['ev_w_out'][j])
        else:
            lam_init = 0.8 - 0.6 * math.exp(-0.3 * l)
            cq, ck, cv, sq, sk, sv = _inproj_odd(h, W['mix_norm'][l], W['od_w_in'][j], act)
            new_odd.append((ck, cv, sk, sv))
            if prompt:
                oc = _diff_attention(cq, ck, cv, b, t, W['rel_bias'], W['od_lambda'][j], W['od_head_g'][j], lam_init)
                os_ = _sb_attention(sq, sk, sv, b, t)
            else:
                r3 = lambda a: a.reshape(b, t, HALF)
                oc, os_ = _decode_attention(r3(cq), r3(sq), [r3(a) for a in (ck, cv, sk, sv)], odd_past[j],
                                            page_table[j], W['rel_bias'], W['od_lambda'][j], W['od_head_g'][j],
                                            lam_init)
            h = _outproj(h, oc.reshape(m, HALF), os_.reshape(m, HALF), W['od_w_out'][j])
        h = _ffn_half(h, W['ffn_norm2'][l], W['ffn2_wi'][l], W['ffn2_wo'][l])
        h = _ple(h, W['ple_norm'][l], W['ple_wg'][l], p[l].reshape(m, -1), W['ple_wp'][l], W['final_norm'],
                 final=(l == depth - 1))
    return h.reshape(b, t, d), new_even, new_odd


def kernel(x_prompt, x_sample, p_prompt, p_sample, state_conv, state_mlstm_C, state_mlstm_n, state_mlstm_m, cache_diff_k, cache_diff_v, cache_sb_k, cache_sb_v, page_table, ffn_norm1, ffn1_wi, ffn1_wo, mix_norm, ffn_norm2, ffn2_wi, ffn2_wo, ple_norm, ple_wg, ple_wp, ev_w_in, ev_conv_w, ev_conv_b, ev_ln_g, ev_ln_b, ev_gate_b, ev_w_out, od_w_in, od_lambda, od_head_g, od_w_out, rel_bias, final_norm):
    bf = lambda a: a.astype(BF16)
    n_even, n_odd = ev_w_in.shape[0], od_w_in.shape[0]
    W = dict(ffn_norm1=ffn_norm1, ffn1_wi=bf(ffn1_wi), ffn1_wo=bf(ffn1_wo), mix_norm=mix_norm,
             ffn_norm2=ffn_norm2, ffn2_wi=bf(ffn2_wi), ffn2_wo=bf(ffn2_wo),
             ple_norm=ple_norm, ple_wg=bf(ple_wg), ple_wp=bf(ple_wp),
             ev_w_in=bf(ev_w_in[:, :, :6 * HALF]), ev_w_gt=bf(jnp.swapaxes(ev_w_in[:, :, 6 * HALF:], 1, 2)),
             ev_conv_w=ev_conv_w, ev_conv_b=ev_conv_b, ev_ln_g=ev_ln_g, ev_ln_b=ev_ln_b,
             ev_gate_b=ev_gate_b, ev_w_out=bf(ev_w_out),
             od_w_in=bf(od_w_in), od_lambda=od_lambda, od_head_g=od_head_g, od_w_out=bf(od_w_out),
             rel_bias=rel_bias, final_norm=final_norm)
    bp, tp = x_prompt.shape[0], x_prompt.shape[1]
    bs, ts = x_sample.shape[0], x_sample.shape[1]
    even_p = [(jnp.zeros((bp, CONV_W - 1, HALF), F32), jnp.zeros((bp, H_B, DH_B, DH_B), F32),
               jnp.zeros((bp, H_B, DH_B), F32), jnp.zeros((bp, H_B), F32)) for _ in range(n_even)]
    y_prompt, ev_p, od_p = _trunk(x_prompt, p_prompt, even_p, None, None, W)
    even_s = [(state_conv[j], state_mlstm_C[j], state_mlstm_n[j], state_mlstm_m[j]) for j in range(n_even)]
    n_pool, psz = cache_diff_k.shape[1], cache_diff_k.shape[2]
    pool_d = lambda c: c.reshape(n_odd * n_pool, psz * H_C, 2 * DH_C)
    pool_s = lambda c: jnp.transpose(c, (0, 1, 3, 4, 2)).reshape(n_odd * n_pool, H_D * DH_D, psz)
    caches = (pool_d(cache_diff_k), pool_d(cache_diff_v), pool_s(cache_sb_k), pool_s(cache_sb_v))
    tables = [page_table + j * n_pool for j in range(n_odd)]
    y_sample, ev_s, od_s = _trunk(x_sample, p_sample, even_s, [caches] * n_odd, tables, W)
    stack = lambda states, i, shape: jnp.stack([s[i].reshape(shape) for s in states])
    ev = lambda states, i: jnp.stack([s[i] for s in states])
    return (y_prompt, y_sample,
            ev(ev_p, 0), ev(ev_s, 0), ev(ev_p, 1), ev(ev_s, 1),
            ev(ev_p, 2), ev(ev_s, 2), ev(ev_p, 3), ev(ev_s, 3),
            stack(od_p, 0, (bp, tp, H_C, 2 * DH_C)), stack(od_s, 0, (bs, ts, H_C, 2 * DH_C)),
            stack(od_p, 1, (bp, tp, H_C, 2 * DH_C)), stack(od_s, 1, (bs, ts, H_C, 2 * DH_C)),
            stack(od_p, 2, (bp, tp, H_D, DH_D)), stack(od_s, 2, (bs, ts, H_D, DH_D)),
            stack(od_p, 3, (bp, tp, H_D, DH_D)), stack(od_s, 3, (bs, ts, H_D, DH_D)))
```

```python
import functools
import math

import jax
import jax.numpy as jnp
from jax import lax
from jax.experimental import pallas as pl
from jax.experimental.pallas import tpu as pltpu

F32 = jnp.float32
BF16 = jnp.bfloat16

LANES = 128
SUBLANES = 8
VMEM_LIMIT_BYTES = 56 * 1024 * 1024

D_MODEL = 1024
D_FF = 2816
HALF = 512
CONV_W = 31
H_B, DH_B = 4, 128
H_C, DH_C = 4, 64
H_D, DH_D = 8, 64
N_BUCKETS = 32
MAX_DIST = 128
MLSTM_CHUNK = 128
NEG_INF = -1e30
SB_DEAD = -104.0

ROW_TILE = 512
FF_CHUNK = 256
DIFF_BLK = 512
SB_BLK = 256
CONV_ROWS = 256
CONV_SUB = 32
CONV_SEQS = 16
MLSTM_SEQS = 4
DEC_PAGES = 4
DEC_DIFF_PAGES = 8
DEC_SB_SEQS = 2


def _cp(*sem):
    return pltpu.CompilerParams(dimension_semantics=sem, vmem_limit_bytes=VMEM_LIMIT_BYTES)


def _rms(x, g, eps=1e-6):
    return x * lax.rsqrt(jnp.mean(x * x, axis=-1, keepdims=True) + eps) * g


def _dot(a, b):
    return jnp.dot(a, b, preferred_element_type=F32)


def _dot_nt(a, b):
    return lax.dot_general(a, b, (((1,), (1,)), ((), ())), preferred_element_type=F32)


def _dot_tn(a, b):
    return lax.dot_general(a, b, (((0,), (0,)), ((), ())), preferred_element_type=F32)


def _log_sigmoid_neg(z):
    return -(jnp.maximum(z, 0.0) + jnp.log1p(jnp.exp(-jnp.abs(z))))


def _split_bf16(x):
    hi = x.astype(BF16)
    lo = (x - hi.astype(F32)).astype(BF16)
    return hi, lo


def _ffn_kernel(x_ref, g_ref, wi_ref, wo_ref, o_ref, act_ref):
    x = x_ref[...]
    hn = _rms(x, g_ref[...]).astype(BF16)
    for c in range(D_FF // FF_CHUNK):
        lo = c * FF_CHUNK
        gate = _dot(hn, wi_ref[:, lo:lo + FF_CHUNK])
        up = _dot(hn, wi_ref[:, D_FF + lo:D_FF + lo + FF_CHUNK])
        act_ref[:, lo:lo + FF_CHUNK] = (gate * jax.nn.sigmoid(gate) * up).astype(BF16)
    o_ref[...] = x + 0.5 * _dot(act_ref[...], wo_ref[...])


def _ffn_half(h, g, wi, wo):
    m, d = h.shape
    tm = min(ROW_TILE, m)
    resident = lambda shape: pl.BlockSpec(shape, lambda i: (0, 0), pipeline_mode=pl.Buffered(1))
    return pl.pallas_call(
        _ffn_kernel,
        grid=(m // tm,),
        in_specs=[
            pl.BlockSpec((tm, d), lambda i: (i, 0)),
            pl.BlockSpec((1, d), lambda i: (0, 0)),
            resident((d, 2 * D_FF)),
            resident((D_FF, d)),
        ],
        out_specs=pl.BlockSpec((tm, d), lambda i: (i, 0)),
        out_shape=jax.ShapeDtypeStruct((m, d), F32),
        scratch_shapes=[pltpu.VMEM((tm, D_FF), BF16)],
        compiler_params=_cp("parallel"),
    )(h, g.reshape(1, d), wi, wo)


def _ple_kernel(x_ref, g_ref, wg_ref, p_ref, wp_ref, fg_ref, o_ref, *, final):
    x = x_ref[...]
    hn = _rms(x, g_ref[...]).astype(BF16)
    gate = jax.nn.sigmoid(_dot(hn, wg_ref[...]))
    h = x + gate * _dot(p_ref[...].astype(BF16), wp_ref[...])
    if final:
        h = _rms(h, fg_ref[...])
    o_ref[...] = h


def _ple(h, g, wg, p, wp, fg, final):
    m, d = h.shape
    tm = min(ROW_TILE, m)
    pd = p.shape[1]
    return pl.pallas_call(
        functools.partial(_ple_kernel, final=final),
        grid=(m // tm,),
        in_specs=[
            pl.BlockSpec((tm, d), lambda i: (i, 0)),
            pl.BlockSpec((1, d), lambda i: (0, 0)),
            pl.BlockSpec((d, d), lambda i: (0, 0)),
            pl.BlockSpec((tm, pd), lambda i: (i, 0)),
            pl.BlockSpec((pd, d), lambda i: (0, 0)),
            pl.BlockSpec((1, d), lambda i: (0, 0)),
        ],
        out_specs=pl.BlockSpec((tm, d), lambda i: (i, 0)),
        out_shape=jax.ShapeDtypeStruct((m, d), F32),
        compiler_params=_cp("parallel"),
    )(h, g.reshape(1, d), wg, p, wp, fg.reshape(1, d))


def _inproj_even_kernel(x_ref, g_ref, w_ref, wgt_ref, u_ref, q_ref, k_ref, v_ref, o_ref, gt_ref):
    hn = _rms(x_ref[...], g_ref[...]).astype(BF16)

    def col(c):
        return _dot(hn, w_ref[:, c * HALF:(c + 1) * HALF])

    u_ref[...] = col(0) * jax.nn.sigmoid(col(1))
    q_ref[...] = col(2).astype(q_ref.dtype)
    k_ref[...] = (col(3) * (DH_B ** -0.5)).astype(k_ref.dtype)
    v_ref[...] = col(4).astype(v_ref.dtype)
    o_ref[...] = col(5)
    gt_ref[...] = _dot_nt(wgt_ref[...], hn)


def _inproj_even(h, g, w, wgt, qkv_dtype):
    m, d = h.shape
    tm = min(ROW_TILE, m)
    row = lambda i: (i, 0)
    out = lambda dt: jax.ShapeDtypeStruct((m, HALF), dt)
    return pl.pallas_call(
        _inproj_even_kernel,
        grid=(m // tm,),
        in_specs=[
            pl.BlockSpec((tm, d), row),
            pl.BlockSpec((1, d), lambda i: (0, 0)),
            pl.BlockSpec((d, 6 * HALF), lambda i: (0, 0)),
            pl.BlockSpec((2 * H_B, d), lambda i: (0, 0)),
        ],
        out_specs=[pl.BlockSpec((tm, HALF), row)] * 5 + [pl.BlockSpec((2 * H_B, tm), lambda i: (0, i))],
        out_shape=[out(F32), out(qkv_dtype), out(qkv_dtype), out(qkv_dtype), out(F32),
                   jax.ShapeDtypeStruct((2 * H_B, m), F32)],
        compiler_params=_cp("parallel"),
    )(h, g.reshape(1, d), w, wgt)


def _inproj_odd_kernel(x_ref, g_ref, w_ref, cq_ref, ck_ref, cv_ref, sq_ref, sk_ref, sv_ref):
    hn = _rms(x_ref[...], g_ref[...]).astype(BF16)

    def col(c):
        return _dot(hn, w_ref[:, c * HALF:(c + 1) * HALF])

    cq_ref[...] = (col(0) * (DH_C ** -0.5)).astype(cq_ref.dtype)
    ck_ref[...] = col(1)
    cv_ref[...] = col(2)
    sq_ref[...] = (col(3) * (DH_D ** -0.5)).astype(sq_ref.dtype)
    sk_ref[...] = col(4)
    sv_ref[...] = col(5)


def _inproj_odd(h, g, w, q_dtype):
    m, d = h.shape
    tm = min(ROW_TILE, m)
    row = lambda i: (i, 0)
    out = lambda dt: jax.ShapeDtypeStruct((m, HALF), dt)
    return pl.pallas_call(
        _inproj_odd_kernel,
        grid=(m // tm,),
        in_specs=[
            pl.BlockSpec((tm, d), row),
            pl.BlockSpec((1, d), lambda i: (0, 0)),
            pl.BlockSpec((d, 6 * HALF), lambda i: (0, 0)),
        ],
        out_specs=[pl.BlockSpec((tm, HALF), row)] * 6,
        out_shape=[out(q_dtype), out(F32), out(F32), out(q_dtype), out(F32), out(F32)],
        compiler_params=_cp("parallel"),
    )(h, g.reshape(1, d), w)


def _outproj_kernel(x_ref, a_ref, b_ref, wa_ref, wb_ref, o_ref):
    o_ref[...] = (x_ref[...] + _dot(a_ref[...].astype(BF16), wa_ref[...])
                  + _dot(b_ref[...].astype(BF16), wb_ref[...]))


def _outproj(h, a, b, w):
    m, d = h.shape
    tm = min(ROW_TILE, m)
    row = lambda i: (i, 0)
    return pl.pallas_call(
        _outproj_kernel,
        grid=(m // tm,),
        in_specs=[
            pl.BlockSpec((tm, d), row),
            pl.BlockSpec((tm, HALF), row),
            pl.BlockSpec((tm, HALF), row),
            pl.BlockSpec((HALF, d), lambda i: (0, 0)),
            pl.BlockSpec((HALF, d), lambda i: (1, 0)),
        ],
        out_specs=pl.BlockSpec((tm, d), row),
        out_shape=jax.ShapeDtypeStruct((m, d), F32),
        compiler_params=_cp("parallel"),
    )(h, a, b, w, w)


_HIST = 32


def _conv_kernel(u_ref, buf_ref, taps_ref, cb_ref, lg_ref, lb_ref, a_ref, st_ref, win_ref, sh_ref, *, tt, sub, bb):
    t = pl.program_id(1)
    pad = _HIST - (CONV_W - 1)
    span = tt + _HIST - SUBLANES

    for s in range(bb):
        @pl.when(t == 0)
        def _(s=s):
            win_ref[s, 0:SUBLANES, :] = jnp.zeros((SUBLANES, HALF), F32)
            win_ref[s, pad:_HIST, :] = buf_ref[s]

        win_ref[s, _HIST:_HIST + tt, :] = u_ref[s]
        for r in range(1, SUBLANES):
            sh_ref[s, r - 1] = win_ref[s, pl.ds(r, span), :]
        for rb in range(tt // sub):
            acc = jnp.zeros((sub, HALF), F32)
            for w in range(CONV_W):
                a8, r = divmod(pad + w, SUBLANES)
                lo = rb * sub + a8 * SUBLANES
                src = win_ref[s, lo:lo + sub, :] if r == 0 else sh_ref[s, r - 1, lo:lo + sub, :]
                acc = acc + src * taps_ref[w:w + 1, :]
            c = acc + cb_ref[...]
            mu = jnp.mean(c, axis=-1, keepdims=True)
            var = jnp.mean(jnp.square(c - mu), axis=-1, keepdims=True)
            cn = (c - mu) * lax.rsqrt(var + 1e-5) * lg_ref[...] + lb_ref[...]
            a_ref[s, rb * sub:(rb + 1) * sub, :] = (cn * jax.nn.sigmoid(cn)).astype(a_ref.dtype)

        @pl.when(t == pl.num_programs(1) - 1)
        def _(s=s):
            st_ref[s] = win_ref[s, tt + pad:tt + _HIST, :]

        win_ref[s, 0:_HIST, :] = win_ref[s, tt:tt + _HIST, :]


def _conv_module(u, buf, taps, cb, lg, lb, out_dtype):
    b, t, _ = u.shape
    tt = min(CONV_ROWS, t)
    sub = min(CONV_SUB, tt)
    bb = 1 if t > tt else math.gcd(b, CONV_SEQS)
    vec = lambda: pl.BlockSpec((1, HALF), lambda i, j: (0, 0))
    return pl.pallas_call(
        functools.partial(_conv_kernel, tt=tt, sub=sub, bb=bb),
        grid=(b // bb, t // tt),
        in_specs=[
            pl.BlockSpec((bb, tt, HALF), lambda i, j: (i, j, 0)),
            pl.BlockSpec((bb, CONV_W - 1, HALF), lambda i, j: (i, 0, 0)),
            pl.BlockSpec((CONV_W, HALF), lambda i, j: (0, 0)),
            vec(), vec(), vec(),
        ],
        out_specs=[
            pl.BlockSpec((bb, tt, HALF), lambda i, j: (i, j, 0)),
            pl.BlockSpec((bb, CONV_W - 1, HALF), lambda i, j: (i, 0, 0)),
        ],
        out_shape=[jax.ShapeDtypeStruct((b, t, HALF), out_dtype),
                   jax.ShapeDtypeStruct((b, CONV_W - 1, HALF), F32)],
        scratch_shapes=[pltpu.VMEM((bb, _HIST + tt, HALF), F32),
                        pltpu.VMEM((bb, SUBLANES - 1, tt + _HIST - SUBLANES, HALF), F32)],
        compiler_params=_cp("parallel", "arbitrary"),
    )(u, buf, taps, cb.reshape(1, HALF), lg.reshape(1, HALF), lb.reshape(1, HALF))


def _scan_lanes(x, op, fill):
    lane = lax.broadcasted_iota(jnp.int32, x.shape, 1)
    sh = 1
    while sh < x.shape[1]:
        x = op(x, jnp.where(lane >= sh, pltpu.roll(x, sh, 1), fill))
        sh *= 2
    return x


def _mlstm_kernel(q_ref, k_ref, v_ref, o_ref, g_ref, gb_ref, c0_ref, n0_ref, m0_ref,
                  h_ref, c_ref, n_ref, m_ref, cx_ref, *, lr, bb):
    L = MLSTM_CHUNK
    ng = 2 * H_B
    nr = bb * ng

    @pl.when(pl.program_id(1) == 0)
    def _():
        m_ref[...] = m0_ref[...]
        for s in range(bb):
            for h in range(H_B):
                cx_ref[s, h, :, 0:DH_B] = c0_ref[s, h]
                cx_ref[s, h, :, DH_B:2 * DH_B] = jnp.transpose(jnp.broadcast_to(n0_ref[s, h:h + 1, :], (DH_B, DH_B)))

    def rows(ref, s):
        x = ref[s]
        if lr < L:
            x = jnp.concatenate([x.astype(F32), jnp.zeros((L - lr, HALF), F32)], axis=0)
        return x.astype(BF16)

    g = g_ref[...].reshape(nr, L) + gb_ref[...]
    row = lax.broadcasted_iota(jnp.int32, g.shape, 0)
    is_li = row % ng < H_B
    bcum = _scan_lanes(jnp.where(is_li, 0.0, _log_sigmoid_neg(-g)), jnp.add, 0.0)
    b = pltpu.roll(bcum, nr - H_B, 0)
    a = jnp.where(is_li, g - b, 0.0)
    m_old = m_ref[...].reshape(nr, LANES)
    mx = jnp.maximum(m_old, _scan_lanes(a, jnp.maximum, -3e38))
    b_last = b[:, L - 1:L]
    m_new = b_last + mx[:, L - 1:L]
    w_inter = jnp.exp(m_old - mx)
    inv_floor = jnp.exp(-(b + mx))
    w_key = jnp.exp(a + b_last - m_new)
    decay = jnp.exp(b_last + m_old - m_new)
    m_ref[...] = jnp.where(is_li, jnp.broadcast_to(m_new, (nr, LANES)), 0.0).reshape(bb, ng, LANES)
    packed = jnp.concatenate([mx, w_inter, inv_floor, w_key] + [jnp.zeros((L - 4 * nr, L), F32)] * (4 * nr < L), axis=0)
    cols = jnp.transpose(packed)
    tpos = lax.broadcasted_iota(jnp.int32, (L, L), 0)
    spos = lax.broadcasted_iota(jnp.int32, (L, L), 1)
    causal = spos <= tpos
    ones = jnp.ones((L, DH_B), BF16)
    for s in range(bb):
        q, k, v = rows(q_ref, s), rows(k_ref, s), rows(v_ref, s)
        outs = []
        for h in range(H_B):
            sl = slice(h * DH_B, (h + 1) * DH_B)
            qh, kh = q[:, sl], k[:, sl]
            v1 = jnp.concatenate([v[:, sl], ones], axis=1)
            i = s * ng + h
            col = lambda vec: cols[:, vec * nr + i:vec * nr + i + 1]
            gate = jnp.where(causal, jnp.exp(a[i:i + 1, :] - col(0)), 0.0)
            sc = _dot_nt(qh, kh) * gate
            cx = cx_ref[s, h]
            mix = col(1) * _dot(qh, cx.astype(BF16)) + _dot(sc.astype(BF16), v1)
            outs.append(mix[:, 0:DH_B] / jnp.maximum(jnp.abs(mix[:, DH_B:2 * DH_B]), col(2)))
            kw = (kh.astype(F32) * col(3)).astype(BF16)
            cx_ref[s, h] = decay[i:i + 1, 0:1] * cx + _dot_tn(kw, v1)
        hs = jnp.concatenate(outs, axis=1)
        h_ref[s] = (jax.nn.sigmoid(o_ref[s]) * hs[0:lr]).astype(h_ref.dtype)

    @pl.when(pl.program_id(1) == pl.num_programs(1) - 1)
    def _():
        for s in range(bb):
            for h in range(H_B):
                c_ref[s, h] = cx_ref[s, h, :, 0:DH_B]
                n_ref[s, h:h + 1, :] = jnp.transpose(cx_ref[s, h, :, DH_B:2 * DH_B])[0:1, :]


def _mlstm(q, k, v, o, gt, gate_b, c0, n0, m0, out_dtype):
    b, t, _ = q.shape
    L = MLSTM_CHUNK
    lr = min(L, t)
    nc = t // lr
    m0b = jnp.broadcast_to(jnp.pad(m0, ((0, 0), (0, SUBLANES - H_B)))[:, :, None], (b, SUBLANES, LANES))
    bb = math.gcd(b, MLSTM_SEQS)
    gb = jnp.broadcast_to(jnp.tile(gate_b, bb).reshape(bb * 2 * H_B, 1), (bb * 2 * H_B, L))
    blk = pl.BlockSpec((bb, lr, HALF), lambda i, j: (i, j, 0))
    st = lambda *s: pl.BlockSpec((bb,) + s, lambda i, j: (i,) + (0,) * len(s))
    h, c1, n1, m1 = pl.pallas_call(
        functools.partial(_mlstm_kernel, lr=lr, bb=bb),
        grid=(b // bb, nc),
        in_specs=[blk, blk, blk, blk,
                  pl.BlockSpec((bb, 2 * H_B, L), lambda i, j: (i, 0, j)),
                  pl.BlockSpec((bb * 2 * H_B, L), lambda i, j: (0, 0)),
                  st(H_B, DH_B, DH_B), st(H_B, DH_B), st(SUBLANES, LANES)],
        out_specs=[blk, st(H_B, DH_B, DH_B), st(H_B, DH_B), st(SUBLANES, LANES)],
        out_shape=[jax.ShapeDtypeStruct((b, t, HALF), out_dtype),
                   jax.ShapeDtypeStruct((b, H_B, DH_B, DH_B), F32),
                   jax.ShapeDtypeStruct((b, H_B, DH_B), F32),
                   jax.ShapeDtypeStruct((b, SUBLANES, LANES), F32)],
        scratch_shapes=[pltpu.VMEM((bb, H_B, DH_B, 2 * DH_B), F32)],
        compiler_params=_cp("parallel", "arbitrary"),
    )(q, k, v, o, gt, gb, c0, n0, m0b)
    return h, c1, n1, m1[:, :H_B, 0]


def _t5_bucket(dist):
    n = jnp.maximum(dist, 0)
    exact = N_BUCKETS // 2
    nf = jnp.maximum(n, 1).astype(F32)
    large = exact + (jnp.log(nf / exact) / math.log(MAX_DIST / exact) * (N_BUCKETS - exact)).astype(jnp.int32)
    return jnp.where(n < exact, n, jnp.minimum(large, N_BUCKETS - 1))


def _bias_of_dist(rel_bias, dist):
    onehot = jax.nn.one_hot(_t5_bucket(dist), N_BUCKETS, dtype=F32)
    b = jnp.einsum('...k,kh->h...', onehot, rel_bias.astype(F32), precision=lax.Precision.HIGHEST)
    return jnp.where(dist >= 0, b, NEG_INF)


def _lambda(lp_ref, lam_init):
    lp = lp_ref[...]
    s1 = jnp.sum(lp[0:1] * lp[1:2], axis=-1, keepdims=True)
    s2 = jnp.sum(lp[2:3] * lp[3:4], axis=-1, keepdims=True)
    return jnp.exp(s1) - jnp.exp(s2) + lam_init


def _head_norm(x, hg, lam_init):
    return x * lax.rsqrt(jnp.mean(x * x, axis=-1, keepdims=True) + 1e-6) * hg * (1.0 - lam_init)


def _diff_attn_kernel(q_ref, k_ref, v_ref, bd_ref, bp_ref, lp_ref, hg_ref, o_ref, kb_ref, vt_ref, *, lam_init):
    blk = DIFF_BLK
    i = pl.program_id(2)

    @pl.when(i == 0)
    def _():
        kb_ref[...] = k_ref[...].astype(BF16)
        for c in range(vt_ref.shape[0]):
            vt_ref[c] = jnp.transpose(v_ref[c * blk:(c + 1) * blk, :]).astype(BF16)

    q = q_ref[...]
    lane = lax.broadcasted_iota(jnp.int32, q.shape, 1)
    zero = jnp.zeros_like(q)
    q2 = jnp.concatenate([jnp.where(lane < DH_C, q, zero), jnp.where(lane >= DH_C, q, zero)], axis=0)

    def scores(j):
        off = pl.multiple_of(j * blk, blk)
        return _dot_nt(kb_ref[pl.ds(off, blk), :], q2)

    def update(carry, s, vt):
        m, l, acc = carry
        mn = jnp.maximum(m, jnp.max(s, axis=0, keepdims=True))
        p = jnp.exp(s - mn)
        al = jnp.exp(m - mn)
        return mn, al * l + jnp.sum(p, axis=0, keepdims=True), al * acc + _dot(vt, p.astype(BF16))

    s = scores(i) + bd_ref[0]
    m = jnp.max(s, axis=0, keepdims=True)
    p = jnp.exp(s - m)
    carry = (m, jnp.sum(p, axis=0, keepdims=True), _dot(vt_ref[i], p.astype(BF16)))

    def prev_step(c):
        return update(c, scores(i - 1) + bp_ref[0], vt_ref[i - 1])

    carry = lax.cond(i >= 1, prev_step, lambda c: c, carry)

    def far_step(j, c):
        return update(c, scores(j), vt_ref[j])

    m, l, acc = lax.fori_loop(0, jnp.maximum(i - 1, 0), far_step, carry)
    o = acc / l
    o = jnp.transpose(o[:, 0:blk] - _lambda(lp_ref, lam_init) * o[:, blk:2 * blk])
    o_ref[...] = _head_norm(o, hg_ref[0], lam_init).astype(o_ref.dtype)


def _diff_attention(cq, ck, cv, b, t, rel_bias, lam_p, head_g, lam_init):
    blk = DIFF_BLK
    nq = t // blk
    r = jnp.arange(blk, dtype=jnp.int32)
    d0 = r[None, :] - r[:, None]
    assert blk + 1 >= MAX_DIST
    far = _bias_of_dist(rel_bias, jnp.full((1, 1), 2 * blk, jnp.int32))
    rel = lambda d: jnp.tile(jnp.where(d >= 0, _bias_of_dist(rel_bias, d) - far, NEG_INF), (1, 1, 2))
    bd = rel(d0)
    bp = rel(d0 + blk)
    per_head = lambda *s: pl.BlockSpec((1,) + s, lambda bi, h, i: (h, 0, 0))
    return pl.pallas_call(
        functools.partial(_diff_attn_kernel, lam_init=lam_init),
        grid=(b, H_C, nq),
        in_specs=[
            pl.BlockSpec((blk, LANES), lambda bi, h, i: (bi * nq + i, h)),
            pl.BlockSpec((t, LANES), lambda bi, h, i: (bi, h)),
            pl.BlockSpec((t, LANES), lambda bi, h, i: (bi, h)),
            per_head(blk, 2 * blk), per_head(blk, 2 * blk),
            pl.BlockSpec((4, DH_C), lambda bi, h, i: (0, 0)),
            per_head(1, LANES),
        ],
        out_specs=pl.BlockSpec((blk, LANES), lambda bi, h, i: (bi * nq + i, h)),
        out_shape=jax.ShapeDtypeStruct((b * t, HALF), BF16),
        scratch_shapes=[pltpu.VMEM((t, LANES), BF16), pltpu.VMEM((nq, LANES, blk), BF16)],
        compiler_params=_cp("parallel", "parallel", "arbitrary"),
    )(cq, ck, cv, bd, bp, lam_p, head_g.reshape(H_C, 1, LANES))


def _sb_tile(q2, kblk, vblk, upper, r, mask):
    z = _dot_nt(q2, kblk)
    lk = _log_sigmoid_neg(z)
    if mask is not None:
        lk = jnp.where(mask, lk, 0.0)
    hi, lo = _split_bf16(lk)
    after = _dot(hi, upper) + _dot(lo, upper) + r
    w = jnp.exp(lk + z + after)
    if mask is not None:
        w = jnp.where(mask, w, 0.0)
    return _dot(w.astype(BF16), vblk), r + jnp.sum(lk, axis=-1, keepdims=True)


def _strict_upper(n):
    j = lax.broadcasted_iota(jnp.int32, (n, n), 0)
    s = lax.broadcasted_iota(jnp.int32, (n, n), 1)
    return jnp.where(j > s, 1.0, 0.0).astype(BF16)


def _sb_attn_kernel(q_ref, k_ref, v_ref, o_ref, kb_ref, vb_ref):
    blk = SB_BLK
    i = pl.program_id(2)

    @pl.when(i == 0)
    def _():
        kb_ref[...] = k_ref[...].astype(BF16)
        vb_ref[...] = v_ref[...].astype(BF16)

    q = q_ref[...]
    lane = lax.broadcasted_iota(jnp.int32, q.shape, 1)
    zero = jnp.zeros_like(q)
    q2 = jnp.concatenate([jnp.where(lane < DH_D, q, zero), jnp.where(lane >= DH_D, q, zero)], axis=0)
    upper = _strict_upper(blk)
    tpos = lax.broadcasted_iota(jnp.int32, (2 * blk, blk), 0) % blk
    spos = lax.broadcasted_iota(jnp.int32, (2 * blk, blk), 1)

    def tile(j, r, mask):
        off = pl.multiple_of(j * blk, blk)
        return _sb_tile(q2, kb_ref[pl.ds(off, blk), :], vb_ref[pl.ds(off, blk), :], upper, r, mask)

    acc, r = tile(i, jnp.zeros((2 * blk, 1), F32), spos < tpos)

    def cond(c):
        j, _, r = c
        return jnp.logical_and(j >= 0, jnp.max(r) > SB_DEAD)

    def body(c):
        j, acc, r = c
        pv, r = tile(j, r, None)
        return j - 1, acc + pv, r

    _, acc, _ = lax.while_loop(cond, body, (i - 1, acc, r))
    o_ref[...] = jnp.where(lane < DH_D, acc[0:blk], acc[blk:2 * blk]).astype(o_ref.dtype)


def _sb_attention(sq, sk, sv, b, t):
    blk = SB_BLK
    nq = t // blk
    return pl.pallas_call(
        _sb_attn_kernel,
        grid=(b, HALF // LANES, nq),
        in_specs=[
            pl.BlockSpec((blk, LANES), lambda bi, h, i: (bi * nq + i, h)),
            pl.BlockSpec((t, LANES), lambda bi, h, i: (bi, h)),
            pl.BlockSpec((t, LANES), lambda bi, h, i: (bi, h)),
        ],
        out_specs=pl.BlockSpec((blk, LANES), lambda bi, h, i: (bi * nq + i, h)),
        out_shape=jax.ShapeDtypeStruct((b * t, HALF), BF16),
        scratch_shapes=[pltpu.VMEM((t, LANES), BF16), pltpu.VMEM((t, LANES), BF16)],
        compiler_params=_cp("parallel", "parallel", "arbitrary"),
    )(sq, sk, sv)


_DEC_ROWS = 64


def _pad_rows(x, n):
    return jnp.concatenate([x, jnp.zeros((n - x.shape[0], x.shape[1]), F32)], axis=0).astype(BF16)


def _dec_diff_kernel(pt_ref, cq_ref, ckf_ref, cvf_ref, bias_ref, bnew_ref, lp_ref, hg_ref, *rest,
                     ts, lam_init, npg):
    pages = rest[:2 * npg]
    oc_ref, qa_ref, m_ref, l_ref, acc_ref = rest[2 * npg:]
    g = pl.program_id(1)

    def update(tiles):
        m = m_ref[...]
        mn = m
        for s, _ in tiles:
            mn = jnp.maximum(mn, jnp.max(s, axis=-1, keepdims=True))
        al = jnp.exp(m - mn)
        l = al * l_ref[...]
        acc = al * acc_ref[...]
        for s, v in tiles:
            p = jnp.exp(s - mn)
            l = l + jnp.sum(p, axis=-1, keepdims=True)
            acc = acc + _dot(p.astype(BF16), v)
        m_ref[...] = mn
        l_ref[...] = l
        acc_ref[...] = acc

    @pl.when(g == 0)
    def _():
        cq = cq_ref[0]
        half = lax.broadcasted_iota(jnp.int32, (ts, LANES), 1) >= DH_C
        qa_ref[...] = jnp.concatenate(
            [jnp.where(half if c % 2 else jnp.logical_not(half), cq[:, (c // 2) * LANES:(c // 2 + 1) * LANES], 0.0)
             for c in range(2 * H_C)], axis=0).astype(BF16)
        m_ref[...] = jnp.full_like(m_ref, NEG_INF)
        l_ref[...] = jnp.zeros_like(l_ref)
        acc_ref[...] = jnp.zeros_like(acc_ref)
        update([(_dot_nt(qa_ref[...], _pad_rows(ckf_ref[0], LANES)) + bnew_ref[...], _pad_rows(cvf_ref[0], LANES))])

    qa = qa_ref[...]
    update([(_dot_nt(qa, pages[2 * p][0].astype(BF16)) + bias_ref[g * npg + p], pages[2 * p + 1][0].astype(BF16))
            for p in range(npg)])

    @pl.when(g == pl.num_programs(1) - 1)
    def _():
        lam = _lambda(lp_ref, lam_init)
        o = acc_ref[...] / l_ref[...]
        for h in range(H_C):
            r0 = h * 2 * ts
            oh = o[r0:r0 + ts] - lam * o[r0 + ts:r0 + 2 * ts]
            oc_ref[0, :, h * LANES:(h + 1) * LANES] = _head_norm(oh, hg_ref[h:h + 1, :], lam_init)


def _sb_queries(sq):
    lane = lax.broadcasted_iota(jnp.int32, sq.shape, 1)
    return jnp.concatenate(
        [jnp.where((lane >= c * DH_D) & (lane < (c + 1) * DH_D), sq, 0.0) for c in range(H_D)], axis=0).astype(BF16)


def _sb_fold(tiles, upper, r, acc):
    for z, pv, mask in tiles:
        lk = _log_sigmoid_neg(z)
        if mask is not None:
            lk = jnp.where(mask, lk, 0.0)
        hi, lo = _split_bf16(lk)
        w = jnp.exp(lk + z + _dot(hi, upper) + _dot(lo, upper) + r)
        if mask is not None:
            w = jnp.where(mask, w, 0.0)
        acc = acc + pv(w.astype(BF16))
        r = r + jnp.sum(lk, axis=-1, keepdims=True)
    return r, acc


def _sb_page_tiles(qs, pages):
    return [(_dot(qs, pages[2 * p][0].astype(BF16)), functools.partial(_dot_nt, b=pages[2 * p + 1][0].astype(BF16)),
             None) for p in range(len(pages) // 2)]


def _sb_heads_to_lanes(acc, ts):
    lane = lax.broadcasted_iota(jnp.int32, (ts, LANES), 1)
    outs = []
    for pr in range(H_D // 2):
        sl = slice(pr * LANES, (pr + 1) * LANES)
        r0 = pr * 2 * ts
        outs.append(jnp.where(lane < DH_D, acc[r0:r0 + ts, sl], acc[r0 + ts:r0 + 2 * ts, sl]))
    return jnp.concatenate(outs, axis=1)


def _dec_sb_first_kernel(pt_ref, sq_ref, sk_ref, sv_ref, *rest, ts, npg, bb):
    pages = rest[:2 * npg * bb]
    os_ref, acc_ref, r_ref, alive_ref = rest[2 * npg * bb:]
    psz = pages[0].shape[2]
    upper = _strict_upper(psz)
    tq = lax.broadcasted_iota(jnp.int32, (_DEC_ROWS, psz), 0) % ts
    kpos = lax.broadcasted_iota(jnp.int32, (_DEC_ROWS, psz), 1)
    for s in range(bb):
        qs = _sb_queries(sq_ref[s])
        sv_new = _pad_rows(sv_ref[s], psz)
        tiles = [(_dot_nt(qs, _pad_rows(sk_ref[s], psz)), lambda w, sv_new=sv_new: _dot(w, sv_new), kpos < tq)]
        r, acc = _sb_fold(tiles + _sb_page_tiles(qs, pages[2 * npg * s:2 * npg * (s + 1)]), upper,
                          jnp.zeros((_DEC_ROWS, 1), F32), jnp.zeros((_DEC_ROWS, HALF), F32))
        os_ref[s] = _sb_heads_to_lanes(acc, ts)
        acc_ref[s] = acc
        r_ref[s] = jnp.broadcast_to(r, (_DEC_ROWS, LANES))
        alive = jnp.max(r, axis=0, keepdims=True) > SB_DEAD
        alive_ref[s] = jnp.broadcast_to(jnp.where(alive, 1, 0), (SUBLANES, LANES)).astype(jnp.int32)


def _dec_sb_rest_kernel(pt_ref, al_ref, sq_ref, acc_in_ref, r_in_ref, *rest, ts, npg, nrest):
    pages = rest[:2 * nrest]
    os_ref, acc_ref, r_ref = rest[2 * nrest:]
    acc_ref[...] = acc_in_ref[0]
    r_ref[...] = r_in_ref[0][:, 0:1]

    @pl.when(al_ref[pl.program_id(0)] == 1)
    def _():
        qs = _sb_queries(sq_ref[0])
        upper = _strict_upper(pages[0].shape[2])
        for grp in range(nrest // npg):
            @pl.when(jnp.max(r_ref[...]) > SB_DEAD)
            def _(grp=grp):
                tiles = _sb_page_tiles(qs, pages[2 * npg * grp:2 * npg * (grp + 1)])
                r, acc = _sb_fold(tiles, upper, r_ref[...], acc_ref[...])
                r_ref[...] = r
                acc_ref[...] = acc

    os_ref[0] = _sb_heads_to_lanes(acc_ref[...], ts)


def _decode_attention(cq, sq, new_rows, caches, page_table, rel_bias, lam_p, head_g, lam_init):
    b, ts, _ = cq.shape
    n_pages = page_table.shape[1]
    psz = caches[2].shape[2]
    past = n_pages * psz
    npd = math.gcd(n_pages, DEC_DIFF_PAGES)
    nps = math.gcd(n_pages, DEC_PAGES)
    nrest = n_pages - nps
    nkn = LANES // H_C
    assert _DEC_ROWS == 2 * H_C * ts == H_D * ts and ts <= nkn
    ck, cv, sk, sv = new_rows
    ckf = ck.reshape(b, ts * H_C, LANES)
    cvf = cv.reshape(b, ts * H_C, LANES)
    dk, dv, skt, svt = caches
    pt = page_table.reshape(-1)

    def table(base):
        base = jnp.moveaxis(base, 0, -3)
        own = jnp.arange(H_C)[:, None, None, None] == jnp.arange(H_C)[None, None, None, :]
        tab = jnp.where(own, base[..., None], NEG_INF)
        tab = jnp.broadcast_to(tab[..., :, None, :, :, :], tab.shape[:-3] + (2,) + tab.shape[-3:])
        return tab.reshape(tab.shape[:-5] + (_DEC_ROWS, tab.shape[-2] * H_C))

    tq = jnp.arange(ts, dtype=jnp.int32)
    kpos = (jnp.arange(n_pages - 1, -1, -1, dtype=jnp.int32)[:, None] * psz
            + jnp.arange(psz, dtype=jnp.int32)[None, :])
    dist = past + tq[None, :, None] - kpos[:, None, :]
    bias = table(_bias_of_dist(rel_bias, dist))
    knew = jnp.arange(nkn, dtype=jnp.int32)
    dnew = jnp.where(knew[None, :] < ts, tq[:, None] - knew[None, :], -1)
    bnew = table(_bias_of_dist(rel_bias, dnew))

    page = lambda idx: pl.BlockSpec((1, HALF, LANES), idx)
    const = lambda shape: pl.BlockSpec(shape, lambda *_: (0,) * len(shape))

    row = pl.BlockSpec((1, ts, HALF), lambda bi, g, pt: (bi, 0, 0))
    rowf = pl.BlockSpec((1, ts * H_C, LANES), lambda bi, g, pt: (bi, 0, 0))
    dspecs, dargs = [], []
    for p in range(npd):
        for c in (dk, dv):
            dspecs.append(page(lambda bi, g, pt, p=p: (pt[bi * n_pages + n_pages - 1 - (g * npd + p)], 0, 0)))
            dargs.append(c)
    oc = pl.pallas_call(
        functools.partial(_dec_diff_kernel, ts=ts, lam_init=lam_init, npg=npd),
        grid_spec=pltpu.PrefetchScalarGridSpec(
            num_scalar_prefetch=1,
            grid=(b, n_pages // npd),
            in_specs=[row, rowf, rowf, const((n_pages, _DEC_ROWS, psz * H_C)), const((_DEC_ROWS, LANES)),
                      const((4, DH_C)), const((H_C, LANES))] + dspecs,
            out_specs=row,
            scratch_shapes=[pltpu.VMEM((_DEC_ROWS, LANES), BF16), pltpu.VMEM((_DEC_ROWS, 1), F32),
                            pltpu.VMEM((_DEC_ROWS, 1), F32), pltpu.VMEM((_DEC_ROWS, LANES), F32)],
        ),
        out_shape=jax.ShapeDtypeStruct((b, ts, HALF), F32),
        compiler_params=_cp("parallel", "arbitrary"),
    )(pt, cq, ckf, cvf, bias, bnew, lam_p, head_g.reshape(H_C, LANES), *dargs)

    bb = math.gcd(b, DEC_SB_SEQS)
    rows = lambda n, *s: pl.BlockSpec((n,) + s, lambda bi, *_: (bi,) + (0,) * len(s))
    sspecs, sargs = [], []
    for s in range(bb):
        for p in range(nps):
            for c in (skt, svt):
                sspecs.append(page(lambda bi, pt, s=s, p=p: (pt[(bi * bb + s) * n_pages + n_pages - 1 - p], 0, 0)))
                sargs.append(c)
    os_first, acc, r, alive = pl.pallas_call(
        functools.partial(_dec_sb_first_kernel, ts=ts, npg=nps, bb=bb),
        grid_spec=pltpu.PrefetchScalarGridSpec(
            num_scalar_prefetch=1,
            grid=(b // bb,),
            in_specs=[rows(bb, ts, HALF)] * 3 + sspecs,
            out_specs=[rows(bb, ts, HALF), rows(bb, _DEC_ROWS, HALF), rows(bb, _DEC_ROWS, LANES),
                       rows(bb, SUBLANES, LANES)],
        ),
        out_shape=[jax.ShapeDtypeStruct((b, ts, HALF), F32), jax.ShapeDtypeStruct((b, _DEC_ROWS, HALF), F32),
                   jax.ShapeDtypeStruct((b, _DEC_ROWS, LANES), F32),
                   jax.ShapeDtypeStruct((b, SUBLANES, LANES), jnp.int32)],
        compiler_params=_cp("parallel"),
    )(pt, sq, sk, sv, *sargs)
    if nrest == 0:
        return oc, os_first

    alive = alive[:, 0, 0]

    def rest_pages():
        rspecs, rargs = [], []
        for p in range(nrest):
            for c in (skt, svt):
                rspecs.append(page(lambda bi, pt, al, p=p: (
                    jnp.where(al[bi] == 1, pt[bi * n_pages + n_pages - 1 - nps - p], pt[0]), 0, 0)))
                rargs.append(c)
        return pl.pallas_call(
            functools.partial(_dec_sb_rest_kernel, ts=ts, npg=nps, nrest=nrest),
            grid_spec=pltpu.PrefetchScalarGridSpec(
                num_scalar_prefetch=2,
                grid=(b,),
                in_specs=[rows(1, ts, HALF), rows(1, _DEC_ROWS, HALF), rows(1, _DEC_ROWS, LANES)] + rspecs,
                out_specs=rows(1, ts, HALF),
                scratch_shapes=[pltpu.VMEM((_DEC_ROWS, HALF), F32), pltpu.VMEM((_DEC_ROWS, 1), F32)],
            ),
            out_shape=jax.ShapeDtypeStruct((b, ts, HALF), F32),
            compiler_params=_cp("arbitrary"),
        )(pt, alive, sq, acc, r, *rargs)

    os_ = lax.cond(jnp.any(alive == 1), rest_pages, lambda: os_first)
    return oc, os_


def _trunk(x, p, even_states, odd_past, page_table, W):
    b, t, d = x.shape
    m = b * t
    prompt = odd_past is None
    act = BF16 if prompt else F32
    h = x.reshape(m, d)
    depth = p.shape[0]
    new_even, new_odd = [], []
    for l in range(depth):
        j = l // 2
        h = _ffn_half(h, W['ffn_norm1'][l], W['ffn1_wi'][l], W['ffn1_wo'][l])
        if l % 2 == 0:
            buf, c0, n0, m0 = even_states[j]
            u, q, k, v, o, gt = _inproj_even(h, W['mix_norm'][l], W['ev_w_in'][j], W['ev_w_gt'][j], act)
            a_out, buf1 = _conv_module(u.reshape(b, t, HALF), buf, W['ev_conv_w'][j], W['ev_conv_b'][j],
                                       W['ev_ln_g'][j], W['ev_ln_b'][j], act)
            gt = jnp.moveaxis(gt.reshape(2 * H_B, b, t), 1, 0)
            if t < MLSTM_CHUNK:
                padv = jnp.where(jnp.arange(2 * H_B) < H_B, NEG_INF, -NEG_INF).astype(F32)
                gt = jnp.concatenate(
                    [gt, jnp.broadcast_to(padv[None, :, None], (b, 2 * H_B, MLSTM_CHUNK - t))], axis=2)
            r3 = lambda a: a.reshape(b, t, HALF)
            b_out, c1, n1, m1 = _mlstm(r3(q), r3(k), r3(v), r3(o), gt, W['ev_gate_b'][j], c0, n0, m0, act)
            new_even.append((buf1, c1, n1, m1))
            h = _outproj(h, a_out.reshape(m, HALF), b_out.reshape(m, HALF), W['ev_w_out'][j])
        else:
            lam_init = 0.8 - 0.6 * math.exp(-0.3 * l)
            cq, ck, cv, sq, sk, sv = _inproj_odd(h, W['mix_norm'][l], W['od_w_in'][j], act)
            new_odd.append((ck, cv, sk, sv))
            if prompt:
                oc = _diff_attention(cq, ck, cv, b, t, W['rel_bias'], W['od_lambda'][j], W['od_head_g'][j], lam_init)
                os_ = _sb_attention(sq, sk, sv, b, t)
            else:
                r3 = lambda a: a.reshape(b, t, HALF)
                oc, os_ = _decode_attention(r3(cq), r3(sq), [r3(a) for a in (ck, cv, sk, sv)], odd_past[j],
                                            page_table[j], W['rel_bias'], W['od_lambda'][j], W['od_head_g'][j],
                                            lam_init)
            h = _outproj(h, oc.reshape(m, HALF), os_.reshape(m, HALF), W['od_w_out'][j])
        h = _ffn_half(h, W['ffn_norm2'][l], W['ffn2_wi'][l], W['ffn2_wo'][l])
        h = _ple(h, W['ple_norm'][l], W['ple_wg'][l], p[l].reshape(m, -1), W['ple_wp'][l], W['final_norm'],
                 final=(l == depth - 1))
    return h.reshape(b, t, d), new_even, new_odd


def kernel(x_prompt, x_sample, p_prompt, p_sample, state_conv, state_mlstm_C, state_mlstm_n, state_mlstm_m, cache_diff_k, cache_diff_v, cache_sb_k, cache_sb_v, page_table, ffn_norm1, ffn1_wi, ffn1_wo, mix_norm, ffn_norm2, ffn2_wi, ffn2_wo, ple_norm, ple_wg, ple_wp, ev_w_in, ev_conv_w, ev_conv_b, ev_ln_g, ev_ln_b, ev_gate_b, ev_w_out, od_w_in, od_lambda, od_head_g, od_w_out, rel_bias, final_norm):
    bf = lambda a: a.astype(BF16)
    n_even, n_odd = ev_w_in.shape[0], od_w_in.shape[0]
    W = dict(ffn_norm1=ffn_norm1, ffn1_wi=bf(ffn1_wi), ffn1_wo=bf(ffn1_wo), mix_norm=mix_norm,
             ffn_norm2=ffn_norm2, ffn2_wi=bf(ffn2_wi), ffn2_wo=bf(ffn2_wo),
             ple_norm=ple_norm, ple_wg=bf(ple_wg), ple_wp=bf(ple_wp),
             ev_w_in=bf(ev_w_in[:, :, :6 * HALF]), ev_w_gt=bf(jnp.swapaxes(ev_w_in[:, :, 6 * HALF:], 1, 2)),
             ev_conv_w=ev_conv_w, ev_conv_b=ev_conv_b, ev_ln_g=ev_ln_g, ev_ln_b=ev_ln_b,
             ev_gate_b=ev_gate_b, ev_w_out=bf(ev_w_out),
             od_w_in=bf(od_w_in), od_lambda=od_lambda, od_head_g=od_head_g, od_w_out=bf(od_w_out),
             rel_bias=rel_bias, final_norm=final_norm)
    bp, tp = x_prompt.shape[0], x_prompt.shape[1]
    bs, ts = x_sample.shape[0], x_sample.shape[1]
    even_p = [(jnp.zeros((bp, CONV_W - 1, HALF), F32), jnp.zeros((bp, H_B, DH_B, DH_B), F32),
               jnp.zeros((bp, H_B, DH_B), F32), jnp.zeros((bp, H_B), F32)) for _ in range(n_even)]
    y_prompt, ev_p, od_p = _trunk(x_prompt, p_prompt, even_p, None, None, W)
    even_s = [(state_conv[j], state_mlstm_C[j], state_mlstm_n[j], state_mlstm_m[j]) for j in range(n_even)]
    n_pool, psz = cache_diff_k.shape[1], cache_diff_k.shape[2]
    pool_d = lambda c: c.reshape(n_odd * n_pool, psz * H_C, 2 * DH_C)
    pool_s = lambda c: jnp.transpose(c, (0, 1, 3, 4, 2)).reshape(n_odd * n_pool, H_D * DH_D, psz)
    caches = (pool_d(cache_diff_k), pool_d(cache_diff_v), pool_s(cache_sb_k), pool_s(cache_sb_v))
    tables = [page_table + j * n_pool for j in range(n_odd)]
    y_sample, ev_s, od_s = _trunk(x_sample, p_sample, even_s, [caches] * n_odd, tables, W)
    stack = lambda states, i, shape: jnp.stack([s[i].reshape(shape) for s in states])
    ev = lambda states, i: jnp.stack([s[i] for s in states])
    return (y_prompt, y_sample,
            ev(ev_p, 0), ev(ev_s, 0), ev(ev_p, 1), ev(ev_s, 1),
            ev(ev_p, 2), ev(ev_s, 2), ev(ev_p, 3), ev(ev_s, 3),
            stack(od_p, 0, (bp, tp, H_C, 2 * DH_C)), stack(od_s, 0, (bs, ts, H_C, 2 * DH_C)),
            stack(od_p, 1, (bp, tp, H_C, 2 * DH_C)), stack(od_s, 1, (bs, ts, H_C, 2 * DH_C)),
            stack(od_p, 2, (bp, tp, H_D, DH_D)), stack(od_s, 2, (bs, ts, H_D, DH_D)),
            stack(od_p, 3, (bp, tp, H_D, DH_D)), stack(od_s, 3, (bs, ts, H_D, DH_D)))
```

```python
import functools
import math

import jax
import jax.numpy as jnp
from jax import lax
from jax.experimental import pallas as pl
from jax.experimental.pallas import tpu as pltpu

F32 = jnp.float32
BF16 = jnp.bfloat16

LANES = 128
SUBLANES = 8
VMEM_LIMIT_BYTES = 56 * 1024 * 1024

D_MODEL = 1024
D_FF = 2816
HALF = 512
CONV_W = 31
H_B, DH_B = 4, 128
H_C, DH_C = 4, 64
H_D, DH_D = 8, 64
N_BUCKETS = 32
MAX_DIST = 128
MLSTM_CHUNK = 128
NEG_INF = -1e30
SB_DEAD = -104.0

ROW_TILE = 512
FF_CHUNK = 256
DIFF_BLK = 512
SB_BLK = 256
CONV_ROWS = 256
CONV_SUB = 32
CONV_SEQS = 16
MLSTM_SEQS = 4
DEC_PAGES = 4
DEC_DIFF_PAGES = 8
DEC_SB_SEQS = 2


def _cp(*sem):
    return pltpu.CompilerParams(dimension_semantics=sem, vmem_limit_bytes=VMEM_LIMIT_BYTES)


def _rms(x, g, eps=1e-6):
    return x * lax.rsqrt(jnp.mean(x * x, axis=-1, keepdims=True) + eps) * g


def _dot(a, b):
    return jnp.dot(a, b, preferred_element_type=F32)


def _dot_nt(a, b):
    return lax.dot_general(a, b, (((1,), (1,)), ((), ())), preferred_element_type=F32)


def _dot_tn(a, b):
    return lax.dot_general(a, b, (((0,), (0,)), ((), ())), preferred_element_type=F32)


def _log_sigmoid_neg(z):
    return -(jnp.maximum(z, 0.0) + jnp.log1p(jnp.exp(-jnp.abs(z))))


def _split_bf16(x):
    hi = x.astype(BF16)
    lo = (x - hi.astype(F32)).astype(BF16)
    return hi, lo


def _swiglu_half(x, g_ref, wi_ref, wo_ref, act_ref):
    hn = _rms(x, g_ref[...]).astype(BF16)
    for c in range(D_FF // FF_CHUNK):
        lo = c * FF_CHUNK
        gate = _dot(hn, wi_ref[:, lo:lo + FF_CHUNK])
        up = _dot(hn, wi_ref[:, D_FF + lo:D_FF + lo + FF_CHUNK])
        act_ref[:, lo:lo + FF_CHUNK] = (gate * jax.nn.sigmoid(gate) * up).astype(BF16)
    return x + 0.5 * _dot(act_ref[...], wo_ref[...])


def _ffn_kernel(x_ref, g_ref, wi_ref, wo_ref, o_ref, act_ref):
    o_ref[...] = _swiglu_half(x_ref[...], g_ref, wi_ref, wo_ref, act_ref)


def _ffn_half(h, g, wi, wo):
    m, d = h.shape
    tm = min(ROW_TILE, m)
    resident = lambda shape: pl.BlockSpec(shape, lambda i: (0, 0), pipeline_mode=pl.Buffered(1))
    return pl.pallas_call(
        _ffn_kernel,
        grid=(m // tm,),
        in_specs=[
            pl.BlockSpec((tm, d), lambda i: (i, 0)),
            pl.BlockSpec((1, d), lambda i: (0, 0)),
            resident((d, 2 * D_FF)),
            resident((D_FF, d)),
        ],
        out_specs=pl.BlockSpec((tm, d), lambda i: (i, 0)),
        out_shape=jax.ShapeDtypeStruct((m, d), F32),
        scratch_shapes=[pltpu.VMEM((tm, D_FF), BF16)],
        compiler_params=_cp("parallel"),
    )(h, g.reshape(1, d), wi, wo)


def _layer_tail_kernel(x_ref, a_ref, b_ref, wa_ref, wb_ref, g_ref, wi_ref, wo_ref, pg_ref, wg_ref, p_ref, wp_ref,
                       fg_ref, o_ref, act_ref, *, final):
    h = (x_ref[...] + _dot(a_ref[...].astype(BF16), wa_ref[...]) + _dot(b_ref[...].astype(BF16), wb_ref[...]))
    h = _swiglu_half(h, g_ref, wi_ref, wo_ref, act_ref)
    gate = jax.nn.sigmoid(_dot(_rms(h, pg_ref[...]).astype(BF16), wg_ref[...]))
    h = h + gate * _dot(p_ref[...].astype(BF16), wp_ref[...])
    if final:
        h = _rms(h, fg_ref[...])
    o_ref[...] = h


def _layer_tail(h, a, b, w_out, g, wi, wo, pg, wg, p, wp, fg, final):
    m, d = h.shape
    tm = min(ROW_TILE, m)
    pd = p.shape[1]
    row = lambda i: (i, 0)
    resident = lambda shape, idx=(0, 0): pl.BlockSpec(shape, lambda i: idx, pipeline_mode=pl.Buffered(1))
    vec = lambda: pl.BlockSpec((1, d), lambda i: (0, 0))
    return pl.pallas_call(
        functools.partial(_layer_tail_kernel, final=final),
        grid=(m // tm,),
        in_specs=[
            pl.BlockSpec((tm, d), row), pl.BlockSpec((tm, HALF), row), pl.BlockSpec((tm, HALF), row),
            resident((HALF, d)), resident((HALF, d), (1, 0)),
            vec(), resident((d, 2 * D_FF)), resident((D_FF, d)),
            vec(), resident((d, d)), pl.BlockSpec((tm, pd), row), resident((pd, d)), vec(),
        ],
        out_specs=pl.BlockSpec((tm, d), row),
        out_shape=jax.ShapeDtypeStruct((m, d), F32),
        scratch_shapes=[pltpu.VMEM((tm, D_FF), BF16)],
        compiler_params=_cp("parallel"),
    )(h, a, b, w_out, w_out, g.reshape(1, d), wi, wo, pg.reshape(1, d), wg, p, wp, fg.reshape(1, d))


def _inproj_even_kernel(x_ref, g_ref, w_ref, wgt_ref, u_ref, q_ref, k_ref, v_ref, o_ref, gt_ref):
    hn = _rms(x_ref[...], g_ref[...]).astype(BF16)

    def col(c):
        return _dot(hn, w_ref[:, c * HALF:(c + 1) * HALF])

    u_ref[...] = col(0) * jax.nn.sigmoid(col(1))
    q_ref[...] = col(2).astype(q_ref.dtype)
    k_ref[...] = (col(3) * (DH_B ** -0.5)).astype(k_ref.dtype)
    v_ref[...] = col(4).astype(v_ref.dtype)
    o_ref[...] = col(5)
    gt_ref[...] = _dot_nt(wgt_ref[...], hn)


def _inproj_even(h, g, w, wgt, qkv_dtype):
    m, d = h.shape
    tm = min(ROW_TILE, m)
    row = lambda i: (i, 0)
    out = lambda dt: jax.ShapeDtypeStruct((m, HALF), dt)
    return pl.pallas_call(
        _inproj_even_kernel,
        grid=(m // tm,),
        in_specs=[
            pl.BlockSpec((tm, d), row),
            pl.BlockSpec((1, d), lambda i: (0, 0)),
            pl.BlockSpec((d, 6 * HALF), lambda i: (0, 0)),
            pl.BlockSpec((2 * H_B, d), lambda i: (0, 0)),
        ],
        out_specs=[pl.BlockSpec((tm, HALF), row)] * 5 + [pl.BlockSpec((2 * H_B, tm), lambda i: (0, i))],
        out_shape=[out(F32), out(qkv_dtype), out(qkv_dtype), out(qkv_dtype), out(F32),
                   jax.ShapeDtypeStruct((2 * H_B, m), F32)],
        compiler_params=_cp("parallel"),
    )(h, g.reshape(1, d), w, wgt)


def _inproj_odd_kernel(x_ref, g_ref, w_ref, wkvt_ref, cq_ref, ck_ref, cv_ref, sq_ref, sk_ref, sv_ref, *, time_minor):
    hn = _rms(x_ref[...], g_ref[...]).astype(BF16)

    def col(c):
        return _dot(hn, w_ref[:, c * HALF:(c + 1) * HALF])

    cq_ref[...] = (col(0) * (DH_C ** -0.5)).astype(cq_ref.dtype)
    ck_ref[...] = col(1)
    cv_ref[...] = col(2)
    sq_ref[...] = (col(3) * (DH_D ** -0.5)).astype(sq_ref.dtype)
    if time_minor:
        sk_ref[0] = _dot_nt(wkvt_ref[0:HALF, :], hn)
        sv_ref[0] = _dot_nt(wkvt_ref[HALF:2 * HALF, :], hn)
    else:
        sk_ref[...] = col(4)
        sv_ref[...] = col(5)


def _inproj_odd(h, g, w, wkvt, q_dtype, b, t, time_minor):
    m, d = h.shape
    tm = min(ROW_TILE, m)
    row = lambda i: (i, 0)
    out = lambda dt: jax.ShapeDtypeStruct((m, HALF), dt)
    rows = pl.BlockSpec((tm, HALF), row)
    if time_minor:
        assert t % tm == 0
        nt = t // tm
        kv_spec = pl.BlockSpec((1, HALF, tm), lambda i: (i // nt, 0, i % nt))
        kv_shape = jax.ShapeDtypeStruct((b, HALF, t), F32)
    else:
        kv_spec, kv_shape = rows, out(F32)
    return pl.pallas_call(
        functools.partial(_inproj_odd_kernel, time_minor=time_minor),
        grid=(m // tm,),
        in_specs=[
            pl.BlockSpec((tm, d), row),
            pl.BlockSpec((1, d), lambda i: (0, 0)),
            pl.BlockSpec((d, 6 * HALF), lambda i: (0, 0)),
            pl.BlockSpec((2 * HALF, d), lambda i: (0, 0)),
        ],
        out_specs=[rows] * 4 + [kv_spec] * 2,
        out_shape=[out(q_dtype), out(F32), out(F32), out(q_dtype), kv_shape, kv_shape],
        compiler_params=_cp("parallel"),
    )(h, g.reshape(1, d), w, wkvt)


_HIST = 32


def _conv_kernel(u_ref, buf_ref, taps_ref, cb_ref, lg_ref, lb_ref, a_ref, st_ref, win_ref, sh_ref, *, tt, sub, bb):
    t = pl.program_id(1)
    pad = _HIST - (CONV_W - 1)
    span = tt + _HIST - SUBLANES

    for s in range(bb):
        @pl.when(t == 0)
        def _(s=s):
            win_ref[s, 0:SUBLANES, :] = jnp.zeros((SUBLANES, HALF), F32)
            win_ref[s, pad:_HIST, :] = buf_ref[s]

        win_ref[s, _HIST:_HIST + tt, :] = u_ref[s]
        for r in range(1, SUBLANES):
            sh_ref[s, r - 1] = win_ref[s, pl.ds(r, span), :]
        for rb in range(tt // sub):
            acc = jnp.zeros((sub, HALF), F32)
            for w in range(CONV_W):
                a8, r = divmod(pad + w, SUBLANES)
                lo = rb * sub + a8 * SUBLANES
                src = win_ref[s, lo:lo + sub, :] if r == 0 else sh_ref[s, r - 1, lo:lo + sub, :]
                acc = acc + src * taps_ref[w:w + 1, :]
            c = acc + cb_ref[...]
            mu = jnp.mean(c, axis=-1, keepdims=True)
            var = jnp.mean(jnp.square(c - mu), axis=-1, keepdims=True)
            cn = (c - mu) * lax.rsqrt(var + 1e-5) * lg_ref[...] + lb_ref[...]
            a_ref[s, rb * sub:(rb + 1) * sub, :] = (cn * jax.nn.sigmoid(cn)).astype(a_ref.dtype)

        @pl.when(t == pl.num_programs(1) - 1)
        def _(s=s):
            st_ref[s] = win_ref[s, tt + pad:tt + _HIST, :]

        win_ref[s, 0:_HIST, :] = win_ref[s, tt:tt + _HIST, :]


def _conv_module(u, buf, taps, cb, lg, lb, out_dtype):
    b, t, _ = u.shape
    tt = min(CONV_ROWS, t)
    sub = min(CONV_SUB, tt)
    bb = 1 if t > tt else math.gcd(b, CONV_SEQS)
    vec = lambda: pl.BlockSpec((1, HALF), lambda i, j: (0, 0))
    return pl.pallas_call(
        functools.partial(_conv_kernel, tt=tt, sub=sub, bb=bb),
        grid=(b // bb, t // tt),
        in_specs=[
            pl.BlockSpec((bb, tt, HALF), lambda i, j: (i, j, 0)),
            pl.BlockSpec((bb, CONV_W - 1, HALF), lambda i, j: (i, 0, 0)),
            pl.BlockSpec((CONV_W, HALF), lambda i, j: (0, 0)),
            vec(), vec(), vec(),
        ],
        out_specs=[
            pl.BlockSpec((bb, tt, HALF), lambda i, j: (i, j, 0)),
            pl.BlockSpec((bb, CONV_W - 1, HALF), lambda i, j: (i, 0, 0)),
        ],
        out_shape=[jax.ShapeDtypeStruct((b, t, HALF), out_dtype),
                   jax.ShapeDtypeStruct((b, CONV_W - 1, HALF), F32)],
        scratch_shapes=[pltpu.VMEM((bb, _HIST + tt, HALF), F32),
                        pltpu.VMEM((bb, SUBLANES - 1, tt + _HIST - SUBLANES, HALF), F32)],
        compiler_params=_cp("parallel", "arbitrary"),
    )(u, buf, taps, cb.reshape(1, HALF), lg.reshape(1, HALF), lb.reshape(1, HALF))


def _scan_lanes(x, op, fill):
    lane = lax.broadcasted_iota(jnp.int32, x.shape, 1)
    sh = 1
    while sh < x.shape[1]:
        x = op(x, jnp.where(lane >= sh, pltpu.roll(x, sh, 1), fill))
        sh *= 2
    return x


def _mlstm_kernel(q_ref, k_ref, v_ref, o_ref, g_ref, gb_ref, c0_ref, n0_ref, m0_ref,
                  h_ref, c_ref, n_ref, m_ref, cx_ref, *, lr, bb):
    L = MLSTM_CHUNK
    ng = 2 * H_B
    nr = bb * ng

    @pl.when(pl.program_id(1) == 0)
    def _():
        m_ref[...] = m0_ref[...]
        for s in range(bb):
            for h in range(H_B):
                cx_ref[s, h, :, 0:DH_B] = c0_ref[s, h]
                cx_ref[s, h, :, DH_B:2 * DH_B] = jnp.transpose(jnp.broadcast_to(n0_ref[s, h:h + 1, :], (DH_B, DH_B)))

    def rows(ref, s):
        x = ref[s]
        if lr < L:
            x = jnp.concatenate([x.astype(F32), jnp.zeros((L - lr, HALF), F32)], axis=0)
        return x.astype(BF16)

    g = g_ref[...].reshape(nr, L) + gb_ref[...]
    row = lax.broadcasted_iota(jnp.int32, g.shape, 0)
    is_li = row % ng < H_B
    bcum = _scan_lanes(jnp.where(is_li, 0.0, _log_sigmoid_neg(-g)), jnp.add, 0.0)
    b = pltpu.roll(bcum, nr - H_B, 0)
    a = jnp.where(is_li, g - b, 0.0)
    m_old = m_ref[...].reshape(nr, LANES)
    mx = jnp.maximum(m_old, _scan_lanes(a, jnp.maximum, -3e38))
    b_last = b[:, L - 1:L]
    m_new = b_last + mx[:, L - 1:L]
    w_inter = jnp.exp(m_old - mx)
    inv_floor = jnp.exp(-(b + mx))
    w_key = jnp.exp(a + b_last - m_new)
    decay = jnp.exp(b_last + m_old - m_new)
    m_ref[...] = jnp.where(is_li, jnp.broadcast_to(m_new, (nr, LANES)), 0.0).reshape(bb, ng, LANES)
    packed = jnp.concatenate([mx, w_inter, inv_floor, w_key] + [jnp.zeros((L - 4 * nr, L), F32)] * (4 * nr < L), axis=0)
    cols = jnp.transpose(packed)
    tpos = lax.broadcasted_iota(jnp.int32, (L, L), 0)
    spos = lax.broadcasted_iota(jnp.int32, (L, L), 1)
    causal = spos <= tpos
    ones = jnp.ones((L, DH_B), BF16)
    for s in range(bb):
        q, k, v = rows(q_ref, s), rows(k_ref, s), rows(v_ref, s)
        outs = []
        for h in range(H_B):
            sl = slice(h * DH_B, (h + 1) * DH_B)
            qh, kh = q[:, sl], k[:, sl]
            v1 = jnp.concatenate([v[:, sl], ones], axis=1)
            i = s * ng + h
            col = lambda vec: cols[:, vec * nr + i:vec * nr + i + 1]
            gate = jnp.where(causal, jnp.exp(a[i:i + 1, :] - col(0)), 0.0)
            sc = _dot_nt(qh, kh) * gate
            cx = cx_ref[s, h]
            mix = col(1) * _dot(qh, cx.astype(BF16)) + _dot(sc.astype(BF16), v1)
            outs.append(mix[:, 0:DH_B] / jnp.maximum(jnp.abs(mix[:, DH_B:2 * DH_B]), col(2)))
            kw = (kh.astype(F32) * col(3)).astype(BF16)
            cx_ref[s, h] = decay[i:i + 1, 0:1] * cx + _dot_tn(kw, v1)
        hs = jnp.concatenate(outs, axis=1)
        h_ref[s] = (jax.nn.sigmoid(o_ref[s]) * hs[0:lr]).astype(h_ref.dtype)

    @pl.when(pl.program_id(1) == pl.num_programs(1) - 1)
    def _():
        for s in range(bb):
            for h in range(H_B):
                c_ref[s, h] = cx_ref[s, h, :, 0:DH_B]
                n_ref[s, h:h + 1, :] = jnp.transpose(cx_ref[s, h, :, DH_B:2 * DH_B])[0:1, :]


def _mlstm(q, k, v, o, gt, gate_b, c0, n0, m0, out_dtype):
    b, t, _ = q.shape
    L = MLSTM_CHUNK
    lr = min(L, t)
    nc = t // lr
    m0b = jnp.broadcast_to(jnp.pad(m0, ((0, 0), (0, SUBLANES - H_B)))[:, :, None], (b, SUBLANES, LANES))
    bb = math.gcd(b, MLSTM_SEQS)
    gb = jnp.broadcast_to(jnp.tile(gate_b, bb).reshape(bb * 2 * H_B, 1), (bb * 2 * H_B, L))
    blk = pl.BlockSpec((bb, lr, HALF), lambda i, j: (i, j, 0))
    st = lambda *s: pl.BlockSpec((bb,) + s, lambda i, j: (i,) + (0,) * len(s))
    h, c1, n1, m1 = pl.pallas_call(
        functools.partial(_mlstm_kernel, lr=lr, bb=bb),
        grid=(b // bb, nc),
        in_specs=[blk, blk, blk, blk,
                  pl.BlockSpec((bb, 2 * H_B, L), lambda i, j: (i, 0, j)),
                  pl.BlockSpec((bb * 2 * H_B, L), lambda i, j: (0, 0)),
                  st(H_B, DH_B, DH_B), st(H_B, DH_B), st(SUBLANES, LANES)],
        out_specs=[blk, st(H_B, DH_B, DH_B), st(H_B, DH_B), st(SUBLANES, LANES)],
        out_shape=[jax.ShapeDtypeStruct((b, t, HALF), out_dtype),
                   jax.ShapeDtypeStruct((b, H_B, DH_B, DH_B), F32),
                   jax.ShapeDtypeStruct((b, H_B, DH_B), F32),
                   jax.ShapeDtypeStruct((b, SUBLANES, LANES), F32)],
        scratch_shapes=[pltpu.VMEM((bb, H_B, DH_B, 2 * DH_B), F32)],
        compiler_params=_cp("parallel", "arbitrary"),
    )(q, k, v, o, gt, gb, c0, n0, m0b)
    return h, c1, n1, m1[:, :H_B, 0]


def _t5_bucket(dist):
    n = jnp.maximum(dist, 0)
    exact = N_BUCKETS // 2
    nf = jnp.maximum(n, 1).astype(F32)
    large = exact + (jnp.log(nf / exact) / math.log(MAX_DIST / exact) * (N_BUCKETS - exact)).astype(jnp.int32)
    return jnp.where(n < exact, n, jnp.minimum(large, N_BUCKETS - 1))


def _bias_of_dist(rel_bias, dist):
    onehot = jax.nn.one_hot(_t5_bucket(dist), N_BUCKETS, dtype=F32)
    b = jnp.einsum('...k,kh->h...', onehot, rel_bias.astype(F32), precision=lax.Precision.HIGHEST)
    return jnp.where(dist >= 0, b, NEG_INF)


def _lambda(lp_ref, lam_init):
    lp = lp_ref[...]
    s1 = jnp.sum(lp[0:1] * lp[1:2], axis=-1, keepdims=True)
    s2 = jnp.sum(lp[2:3] * lp[3:4], axis=-1, keepdims=True)
    return jnp.exp(s1) - jnp.exp(s2) + lam_init


def _head_norm(x, hg, lam_init):
    return x * lax.rsqrt(jnp.mean(x * x, axis=-1, keepdims=True) + 1e-6) * hg * (1.0 - lam_init)


def _diff_attn_kernel(q_ref, k_ref, v_ref, bd_ref, bp_ref, lp_ref, hg_ref, o_ref, kb_ref, vt_ref, *, lam_init):
    blk = DIFF_BLK
    i = pl.program_id(2)

    @pl.when(i == 0)
    def _():
        kb_ref[...] = k_ref[...].astype(BF16)
        for c in range(vt_ref.shape[0]):
            vt_ref[c] = jnp.transpose(v_ref[c * blk:(c + 1) * blk, :]).astype(BF16)

    q = q_ref[...]
    lane = lax.broadcasted_iota(jnp.int32, q.shape, 1)
    zero = jnp.zeros_like(q)
    q2 = jnp.concatenate([jnp.where(lane < DH_C, q, zero), jnp.where(lane >= DH_C, q, zero)], axis=0)

    def scores(j):
        off = pl.multiple_of(j * blk, blk)
        return _dot_nt(kb_ref[pl.ds(off, blk), :], q2)

    def update(carry, s, vt):
        m, l, acc = carry
        mn = jnp.maximum(m, jnp.max(s, axis=0, keepdims=True))
        p = jnp.exp(s - mn)
        al = jnp.exp(m - mn)
        return mn, al * l + jnp.sum(p, axis=0, keepdims=True), al * acc + _dot(vt, p.astype(BF16))

    s = scores(i) + bd_ref[0]
    m = jnp.max(s, axis=0, keepdims=True)
    p = jnp.exp(s - m)
    carry = (m, jnp.sum(p, axis=0, keepdims=True), _dot(vt_ref[i], p.astype(BF16)))

    def prev_step(c):
        return update(c, scores(i - 1) + bp_ref[0], vt_ref[i - 1])

    carry = lax.cond(i >= 1, prev_step, lambda c: c, carry)

    def far_step(j, c):
        return update(c, scores(j), vt_ref[j])

    m, l, acc = lax.fori_loop(0, jnp.maximum(i - 1, 0), far_step, carry)
    o = acc / l
    o = jnp.transpose(o[:, 0:blk] - _lambda(lp_ref, lam_init) * o[:, blk:2 * blk])
    o_ref[...] = _head_norm(o, hg_ref[0], lam_init).astype(o_ref.dtype)


def _diff_attention(cq, ck, cv, b, t, rel_bias, lam_p, head_g, lam_init):
    blk = DIFF_BLK
    nq = t // blk
    r = jnp.arange(blk, dtype=jnp.int32)
    d0 = r[None, :] - r[:, None]
    assert blk + 1 >= MAX_DIST
    far = _bias_of_dist(rel_bias, jnp.full((1, 1), 2 * blk, jnp.int32))
    rel = lambda d: jnp.tile(jnp.where(d >= 0, _bias_of_dist(rel_bias, d) - far, NEG_INF), (1, 1, 2))
    bd = rel(d0)
    bp = rel(d0 + blk)
    per_head = lambda *s: pl.BlockSpec((1,) + s, lambda bi, h, i: (h, 0, 0))
    return pl.pallas_call(
        functools.partial(_diff_attn_kernel, lam_init=lam_init),
        grid=(b, H_C, nq),
        in_specs=[
            pl.BlockSpec((blk, LANES), lambda bi, h, i: (bi * nq + i, h)),
            pl.BlockSpec((t, LANES), lambda bi, h, i: (bi, h)),
            pl.BlockSpec((t, LANES), lambda bi, h, i: (bi, h)),
            per_head(blk, 2 * blk), per_head(blk, 2 * blk),
            pl.BlockSpec((4, DH_C), lambda bi, h, i: (0, 0)),
            per_head(1, LANES),
        ],
        out_specs=pl.BlockSpec((blk, LANES), lambda bi, h, i: (bi * nq + i, h)),
        out_shape=jax.ShapeDtypeStruct((b * t, HALF), BF16),
        scratch_shapes=[pltpu.VMEM((t, LANES), BF16), pltpu.VMEM((nq, LANES, blk), BF16)],
        compiler_params=_cp("parallel", "parallel", "arbitrary"),
    )(cq, ck, cv, bd, bp, lam_p, head_g.reshape(H_C, 1, LANES))


def _sb_tile(q2, kt, vt, upper, r, mask):
    z = _dot(q2, kt)
    lk = _log_sigmoid_neg(z)
    if mask is not None:
        lk = jnp.where(mask, lk, 0.0)
    hi, lo = _split_bf16(lk)
    after = _dot(hi, upper) + _dot(lo, upper) + r
    w = jnp.exp(lk + z + after)
    if mask is not None:
        w = jnp.where(mask, w, 0.0)
    return _dot_nt(w.astype(BF16), vt), r + jnp.sum(lk, axis=-1, keepdims=True)


def _strict_upper(n):
    j = lax.broadcasted_iota(jnp.int32, (n, n), 0)
    s = lax.broadcasted_iota(jnp.int32, (n, n), 1)
    return jnp.where(j > s, 1.0, 0.0).astype(BF16)


def _sb_attn_kernel(q_ref, k_ref, v_ref, o_ref, kb_ref, vb_ref):
    blk = SB_BLK
    i = pl.program_id(2)

    @pl.when(i == 0)
    def _():
        for c in range(kb_ref.shape[0]):
            kb_ref[c] = k_ref[0, :, c * blk:(c + 1) * blk].astype(BF16)
            vb_ref[c] = v_ref[0, :, c * blk:(c + 1) * blk].astype(BF16)

    q = q_ref[...]
    lane = lax.broadcasted_iota(jnp.int32, q.shape, 1)
    zero = jnp.zeros_like(q)
    q2 = jnp.concatenate([jnp.where(lane < DH_D, q, zero), jnp.where(lane >= DH_D, q, zero)], axis=0)
    upper = _strict_upper(blk)
    tpos = lax.broadcasted_iota(jnp.int32, (2 * blk, blk), 0) % blk
    spos = lax.broadcasted_iota(jnp.int32, (2 * blk, blk), 1)

    def tile(j, r, mask):
        return _sb_tile(q2, kb_ref[j], vb_ref[j], upper, r, mask)

    acc, r = tile(i, jnp.zeros((2 * blk, 1), F32), spos < tpos)

    def cond(c):
        j, _, r = c
        return jnp.logical_and(j >= 0, jnp.max(r) > SB_DEAD)

    def body(c):
        j, acc, r = c
        pv, r = tile(j, r, None)
        return j - 1, acc + pv, r

    _, acc, _ = lax.while_loop(cond, body, (i - 1, acc, r))
    o_ref[...] = jnp.where(lane < DH_D, acc[0:blk], acc[blk:2 * blk]).astype(o_ref.dtype)


def _sb_attention(sq, sk, sv, b, t):
    blk = SB_BLK
    nq = t // blk
    return pl.pallas_call(
        _sb_attn_kernel,
        grid=(b, HALF // LANES, nq),
        in_specs=[
            pl.BlockSpec((blk, LANES), lambda bi, h, i: (bi * nq + i, h)),
            pl.BlockSpec((1, LANES, t), lambda bi, h, i: (bi, h, 0)),
            pl.BlockSpec((1, LANES, t), lambda bi, h, i: (bi, h, 0)),
        ],
        out_specs=pl.BlockSpec((blk, LANES), lambda bi, h, i: (bi * nq + i, h)),
        out_shape=jax.ShapeDtypeStruct((b * t, HALF), BF16),
        scratch_shapes=[pltpu.VMEM((nq, LANES, blk), BF16), pltpu.VMEM((nq, LANES, blk), BF16)],
        compiler_params=_cp("parallel", "parallel", "arbitrary"),
    )(sq, sk, sv)


_DEC_ROWS = 64


def _pad_rows(x, n):
    return jnp.concatenate([x, jnp.zeros((n - x.shape[0], x.shape[1]), F32)], axis=0).astype(BF16)


def _dec_diff_kernel(pt_ref, cq_ref, ckf_ref, cvf_ref, bias_ref, bnew_ref, lp_ref, hg_ref, *rest,
                     ts, lam_init, npg):
    pages = rest[:2 * npg]
    oc_ref, qa_ref, m_ref, l_ref, acc_ref = rest[2 * npg:]
    g = pl.program_id(1)

    def update(tiles):
        m = m_ref[...]
        mn = m
        for s, _ in tiles:
            mn = jnp.maximum(mn, jnp.max(s, axis=-1, keepdims=True))
        al = jnp.exp(m - mn)
        l = al * l_ref[...]
        acc = al * acc_ref[...]
        for s, v in tiles:
            p = jnp.exp(s - mn)
            l = l + jnp.sum(p, axis=-1, keepdims=True)
            acc = acc + _dot(p.astype(BF16), v)
        m_ref[...] = mn
        l_ref[...] = l
        acc_ref[...] = acc

    @pl.when(g == 0)
    def _():
        cq = cq_ref[0]
        half = lax.broadcasted_iota(jnp.int32, (ts, LANES), 1) >= DH_C
        qa_ref[...] = jnp.concatenate(
            [jnp.where(half if c % 2 else jnp.logical_not(half), cq[:, (c // 2) * LANES:(c // 2 + 1) * LANES], 0.0)
             for c in range(2 * H_C)], axis=0).astype(BF16)
        m_ref[...] = jnp.full_like(m_ref, NEG_INF)
        l_ref[...] = jnp.zeros_like(l_ref)
        acc_ref[...] = jnp.zeros_like(acc_ref)
        update([(_dot_nt(qa_ref[...], _pad_rows(ckf_ref[0], LANES)) + bnew_ref[...], _pad_rows(cvf_ref[0], LANES))])

    qa = qa_ref[...]
    update([(_dot_nt(qa, pages[2 * p][0].astype(BF16)) + bias_ref[g * npg + p], pages[2 * p + 1][0].astype(BF16))
            for p in range(npg)])

    @pl.when(g == pl.num_programs(1) - 1)
    def _():
        lam = _lambda(lp_ref, lam_init)
        o = acc_ref[...] / l_ref[...]
        for h in range(H_C):
            r0 = h * 2 * ts
            oh = o[r0:r0 + ts] - lam * o[r0 + ts:r0 + 2 * ts]
            oc_ref[0, :, h * LANES:(h + 1) * LANES] = _head_norm(oh, hg_ref[h:h + 1, :], lam_init)


def _sb_queries(sq):
    lane = lax.broadcasted_iota(jnp.int32, sq.shape, 1)
    return jnp.concatenate(
        [jnp.where((lane >= c * DH_D) & (lane < (c + 1) * DH_D), sq, 0.0) for c in range(H_D)], axis=0).astype(BF16)


def _sb_fold(tiles, upper, r, acc):
    for z, pv, mask in tiles:
        lk = _log_sigmoid_neg(z)
        if mask is not None:
            lk = jnp.where(mask, lk, 0.0)
        hi, lo = _split_bf16(lk)
        w = jnp.exp(lk + z + _dot(hi, upper) + _dot(lo, upper) + r)
        if mask is not None:
            w = jnp.where(mask, w, 0.0)
        acc = acc + pv(w.astype(BF16))
        r = r + jnp.sum(lk, axis=-1, keepdims=True)
    return r, acc


def _sb_page_tiles(qs, pages):
    return [(_dot(qs, pages[2 * p][0].astype(BF16)), functools.partial(_dot_nt, b=pages[2 * p + 1][0].astype(BF16)),
             None) for p in range(len(pages) // 2)]


def _sb_heads_to_lanes(acc, ts):
    lane = lax.broadcasted_iota(jnp.int32, (ts, LANES), 1)
    outs = []
    for pr in range(H_D // 2):
        sl = slice(pr * LANES, (pr + 1) * LANES)
        r0 = pr * 2 * ts
        outs.append(jnp.where(lane < DH_D, acc[r0:r0 + ts, sl], acc[r0 + ts:r0 + 2 * ts, sl]))
    return jnp.concatenate(outs, axis=1)


def _dec_sb_first_kernel(pt_ref, sq_ref, sk_ref, sv_ref, *rest, ts, npg, bb):
    pages = rest[:2 * npg * bb]
    os_ref, acc_ref, r_ref, alive_ref = rest[2 * npg * bb:]
    psz = pages[0].shape[2]
    upper = _strict_upper(psz)
    tq = lax.broadcasted_iota(jnp.int32, (_DEC_ROWS, psz), 0) % ts
    kpos = lax.broadcasted_iota(jnp.int32, (_DEC_ROWS, psz), 1)
    for s in range(bb):
        qs = _sb_queries(sq_ref[s])
        sv_new = _pad_rows(sv_ref[s], psz)
        tiles = [(_dot_nt(qs, _pad_rows(sk_ref[s], psz)), lambda w, sv_new=sv_new: _dot(w, sv_new), kpos < tq)]
        r, acc = _sb_fold(tiles + _sb_page_tiles(qs, pages[2 * npg * s:2 * npg * (s + 1)]), upper,
                          jnp.zeros((_DEC_ROWS, 1), F32), jnp.zeros((_DEC_ROWS, HALF), F32))
        os_ref[s] = _sb_heads_to_lanes(acc, ts)
        acc_ref[s] = acc
        r_ref[s] = jnp.broadcast_to(r, (_DEC_ROWS, LANES))
        alive = jnp.max(r, axis=0, keepdims=True) > SB_DEAD
        alive_ref[s] = jnp.broadcast_to(jnp.where(alive, 1, 0), (SUBLANES, LANES)).astype(jnp.int32)


def _dec_sb_rest_kernel(pt_ref, al_ref, sq_ref, acc_in_ref, r_in_ref, *rest, ts, npg, nrest):
    pages = rest[:2 * nrest]
    os_ref, acc_ref, r_ref = rest[2 * nrest:]
    acc_ref[...] = acc_in_ref[0]
    r_ref[...] = r_in_ref[0][:, 0:1]

    @pl.when(al_ref[pl.program_id(0)] == 1)
    def _():
        qs = _sb_queries(sq_ref[0])
        upper = _strict_upper(pages[0].shape[2])
        for grp in range(nrest // npg):
            @pl.when(jnp.max(r_ref[...]) > SB_DEAD)
            def _(grp=grp):
                tiles = _sb_page_tiles(qs, pages[2 * npg * grp:2 * npg * (grp + 1)])
                r, acc = _sb_fold(tiles, upper, r_ref[...], acc_ref[...])
                r_ref[...] = r
                acc_ref[...] = acc

    os_ref[0] = _sb_heads_to_lanes(acc_ref[...], ts)


def _decode_attention(cq, sq, new_rows, caches, page_table, rel_bias, lam_p, head_g, lam_init):
    b, ts, _ = cq.shape
    n_pages = page_table.shape[1]
    psz = caches[2].shape[2]
    past = n_pages * psz
    npd = math.gcd(n_pages, DEC_DIFF_PAGES)
    nps = math.gcd(n_pages, DEC_PAGES)
    nrest = n_pages - nps
    nkn = LANES // H_C
    assert _DEC_ROWS == 2 * H_C * ts == H_D * ts and ts <= nkn
    ck, cv, sk, sv = new_rows
    ckf = ck.reshape(b, ts * H_C, LANES)
    cvf = cv.reshape(b, ts * H_C, LANES)
    dk, dv, skt, svt = caches
    pt = page_table.reshape(-1)

    def table(base):
        base = jnp.moveaxis(base, 0, -3)
        own = jnp.arange(H_C)[:, None, None, None] == jnp.arange(H_C)[None, None, None, :]
        tab = jnp.where(own, base[..., None], NEG_INF)
        tab = jnp.broadcast_to(tab[..., :, None, :, :, :], tab.shape[:-3] + (2,) + tab.shape[-3:])
        return tab.reshape(tab.shape[:-5] + (_DEC_ROWS, tab.shape[-2] * H_C))

    tq = jnp.arange(ts, dtype=jnp.int32)
    kpos = (jnp.arange(n_pages - 1, -1, -1, dtype=jnp.int32)[:, None] * psz
            + jnp.arange(psz, dtype=jnp.int32)[None, :])
    dist = past + tq[None, :, None] - kpos[:, None, :]
    bias = table(_bias_of_dist(rel_bias, dist))
    knew = jnp.arange(nkn, dtype=jnp.int32)
    dnew = jnp.where(knew[None, :] < ts, tq[:, None] - knew[None, :], -1)
    bnew = table(_bias_of_dist(rel_bias, dnew))

    page = lambda idx: pl.BlockSpec((1, HALF, LANES), idx)
    const = lambda shape: pl.BlockSpec(shape, lambda *_: (0,) * len(shape))

    row = pl.BlockSpec((1, ts, HALF), lambda bi, g, pt: (bi, 0, 0))
    rowf = pl.BlockSpec((1, ts * H_C, LANES), lambda bi, g, pt: (bi, 0, 0))
    dspecs, dargs = [], []
    for p in range(npd):
        for c in (dk, dv):
            dspecs.append(page(lambda bi, g, pt, p=p: (pt[bi * n_pages + n_pages - 1 - (g * npd + p)], 0, 0)))
            dargs.append(c)
    oc = pl.pallas_call(
        functools.partial(_dec_diff_kernel, ts=ts, lam_init=lam_init, npg=npd),
        grid_spec=pltpu.PrefetchScalarGridSpec(
            num_scalar_prefetch=1,
            grid=(b, n_pages // npd),
            in_specs=[row, rowf, rowf, const((n_pages, _DEC_ROWS, psz * H_C)), const((_DEC_ROWS, LANES)),
                      const((4, DH_C)), const((H_C, LANES))] + dspecs,
            out_specs=row,
            scratch_shapes=[pltpu.VMEM((_DEC_ROWS, LANES), BF16), pltpu.VMEM((_DEC_ROWS, 1), F32),
                            pltpu.VMEM((_DEC_ROWS, 1), F32), pltpu.VMEM((_DEC_ROWS, LANES), F32)],
        ),
        out_shape=jax.ShapeDtypeStruct((b, ts, HALF), F32),
        compiler_params=_cp("parallel", "arbitrary"),
    )(pt, cq, ckf, cvf, bias, bnew, lam_p, head_g.reshape(H_C, LANES), *dargs)

    bb = math.gcd(b, DEC_SB_SEQS)
    rows = lambda n, *s: pl.BlockSpec((n,) + s, lambda bi, *_: (bi,) + (0,) * len(s))
    sspecs, sargs = [], []
    for s in range(bb):
        for p in range(nps):
            for c in (skt, svt):
                sspecs.append(page(lambda bi, pt, s=s, p=p: (pt[(bi * bb + s) * n_pages + n_pages - 1 - p], 0, 0)))
                sargs.append(c)
    os_first, acc, r, alive = pl.pallas_call(
        functools.partial(_dec_sb_first_kernel, ts=ts, npg=nps, bb=bb),
        grid_spec=pltpu.PrefetchScalarGridSpec(
            num_scalar_prefetch=1,
            grid=(b // bb,),
            in_specs=[rows(bb, ts, HALF)] * 3 + sspecs,
            out_specs=[rows(bb, ts, HALF), rows(bb, _DEC_ROWS, HALF), rows(bb, _DEC_ROWS, LANES),
                       rows(bb, SUBLANES, LANES)],
        ),
        out_shape=[jax.ShapeDtypeStruct((b, ts, HALF), F32), jax.ShapeDtypeStruct((b, _DEC_ROWS, HALF), F32),
                   jax.ShapeDtypeStruct((b, _DEC_ROWS, LANES), F32),
                   jax.ShapeDtypeStruct((b, SUBLANES, LANES), jnp.int32)],
        compiler_params=_cp("parallel"),
    )(pt, sq, sk, sv, *sargs)
    if nrest == 0:
        return oc, os_first

    alive = alive[:, 0, 0]

    def rest_pages():
        rspecs, rargs = [], []
        for p in range(nrest):
            for c in (skt, svt):
                rspecs.append(page(lambda bi, pt, al, p=p: (
                    jnp.where(al[bi] == 1, pt[bi * n_pages + n_pages - 1 - nps - p], pt[0]), 0, 0)))
                rargs.append(c)
        return pl.pallas_call(
            functools.partial(_dec_sb_rest_kernel, ts=ts, npg=nps, nrest=nrest),
            grid_spec=pltpu.PrefetchScalarGridSpec(
                num_scalar_prefetch=2,
                grid=(b,),
                in_specs=[rows(1, ts, HALF), rows(1, _DEC_ROWS, HALF), rows(1, _DEC_ROWS, LANES)] + rspecs,
                out_specs=rows(1, ts, HALF),
                scratch_shapes=[pltpu.VMEM((_DEC_ROWS, HALF), F32), pltpu.VMEM((_DEC_ROWS, 1), F32)],
            ),
            out_shape=jax.ShapeDtypeStruct((b, ts, HALF), F32),
            compiler_params=_cp("arbitrary"),
        )(pt, alive, sq, acc, r, *rargs)

    os_ = lax.cond(jnp.any(alive == 1), rest_pages, lambda: os_first)
    return oc, os_


def _trunk(x, p, even_states, odd_past, page_table, W):
    b, t, d = x.shape
    m = b * t
    prompt = odd_past is None
    act = BF16 if prompt else F32
    h = x.reshape(m, d)
    depth = p.shape[0]
    new_even, new_odd = [], []
    for l in range(depth):
        j = l // 2
        h = _ffn_half(h, W['ffn_norm1'][l], W['ffn1_wi'][l], W['ffn1_wo'][l])
        if l % 2 == 0:
            buf, c0, n0, m0 = even_states[j]
            u, q, k, v, o, gt = _inproj_even(h, W['mix_norm'][l], W['ev_w_in'][j], W['ev_w_gt'][j], act)
            a_out, buf1 = _conv_module(u.reshape(b, t, HALF), buf, W['ev_conv_w'][j], W['ev_conv_b'][j],
                                       W['ev_ln_g'][j], W['ev_ln_b'][j], act)
            gt = jnp.moveaxis(gt.reshape(2 * H_B, b, t), 1, 0)
            if t < MLSTM_CHUNK:
                padv = jnp.where(jnp.arange(2 * H_B) < H_B, NEG_INF, -NEG_INF).astype(F32)
                gt = jnp.concatenate(
                    [gt, jnp.broadcast_to(padv[None, :, None], (b, 2 * H_B, MLSTM_CHUNK - t))], axis=2)
            r3 = lambda a: a.reshape(b, t, HALF)
            b_out, c1, n1, m1 = _mlstm(r3(q), r3(k), r3(v), r3(o), gt, W['ev_gate_b'][j], c0, n0, m0, act)
            new_even.append((buf1, c1, n1, m1))
            mix_a, mix_b, w_out = a_out.reshape(m, HALF), b_out.reshape(m, HALF), W['ev_w_out'][j]
        else:
            lam_init = 0.8 - 0.6 * math.exp(-0.3 * l)
            cq, ck, cv, sq, sk, sv = _inproj_odd(h, W['mix_norm'][l], W['od_w_in'][j], W['od_w_kvt'][j], act, b, t,
                                                 time_minor=prompt)
            heads = lambda a, nh: a.reshape(b, t, nh, HALF // nh)
            if prompt:
                sb_rows = lambda a: jnp.transpose(a.reshape(b, H_D, DH_D, t), (0, 3, 1, 2))
            else:
                sb_rows = lambda a: heads(a, H_D)
            new_odd.append((heads(ck, H_C), heads(cv, H_C), sb_rows(sk), sb_rows(sv)))
            if prompt:
                oc = _diff_attention(cq, ck, cv, b, t, W['rel_bias'], W['od_lambda'][j], W['od_head_g'][j], lam_init)
                os_ = _sb_attention(sq, sk, sv, b, t)
            else:
                r3 = lambda a: a.reshape(b, t, HALF)
                oc, os_ = _decode_attention(r3(cq), r3(sq), [r3(a) for a in (ck, cv, sk, sv)], odd_past[j],
                                            page_table[j], W['rel_bias'], W['od_lambda'][j], W['od_head_g'][j],
                                            lam_init)
            mix_a, mix_b, w_out = oc.reshape(m, HALF), os_.reshape(m, HALF), W['od_w_out'][j]
        h = _layer_tail(h, mix_a, mix_b, w_out, W['ffn_norm2'][l], W['ffn2_wi'][l], W['ffn2_wo'][l],
                        W['ple_norm'][l], W['ple_wg'][l], p[l].reshape(m, -1), W['ple_wp'][l], W['final_norm'],
                        final=(l == depth - 1))
    return h.reshape(b, t, d), new_even, new_odd


def kernel(x_prompt, x_sample, p_prompt, p_sample, state_conv, state_mlstm_C, state_mlstm_n, state_mlstm_m, cache_diff_k, cache_diff_v, cache_sb_k, cache_sb_v, page_table, ffn_norm1, ffn1_wi, ffn1_wo, mix_norm, ffn_norm2, ffn2_wi, ffn2_wo, ple_norm, ple_wg, ple_wp, ev_w_in, ev_conv_w, ev_conv_b, ev_ln_g, ev_ln_b, ev_gate_b, ev_w_out, od_w_in, od_lambda, od_head_g, od_w_out, rel_bias, final_norm):
    bf = lambda a: a.astype(BF16)
    n_even, n_odd = ev_w_in.shape[0], od_w_in.shape[0]
    W = dict(ffn_norm1=ffn_norm1, ffn1_wi=bf(ffn1_wi), ffn1_wo=bf(ffn1_wo), mix_norm=mix_norm,
             ffn_norm2=ffn_norm2, ffn2_wi=bf(ffn2_wi), ffn2_wo=bf(ffn2_wo),
             ple_norm=ple_norm, ple_wg=bf(ple_wg), ple_wp=bf(ple_wp),
             ev_w_in=bf(ev_w_in[:, :, :6 * HALF]), ev_w_gt=bf(jnp.swapaxes(ev_w_in[:, :, 6 * HALF:], 1, 2)),
             ev_conv_w=ev_conv_w, ev_conv_b=ev_conv_b, ev_ln_g=ev_ln_g, ev_ln_b=ev_ln_b,
             ev_gate_b=ev_gate_b, ev_w_out=bf(ev_w_out),
             od_w_in=bf(od_w_in), od_w_kvt=bf(jnp.swapaxes(od_w_in[:, :, 4 * HALF:], 1, 2)), od_lambda=od_lambda, od_head_g=od_head_g, od_w_out=bf(od_w_out),
             rel_bias=rel_bias, final_norm=final_norm)
    bp, tp = x_prompt.shape[0], x_prompt.shape[1]
    bs, ts = x_sample.shape[0], x_sample.shape[1]
    even_p = [(jnp.zeros((bp, CONV_W - 1, HALF), F32), jnp.zeros((bp, H_B, DH_B, DH_B), F32),
               jnp.zeros((bp, H_B, DH_B), F32), jnp.zeros((bp, H_B), F32)) for _ in range(n_even)]
    y_prompt, ev_p, od_p = _trunk(x_prompt, p_prompt, even_p, None, None, W)
    even_s = [(state_conv[j], state_mlstm_C[j], state_mlstm_n[j], state_mlstm_m[j]) for j in range(n_even)]
    n_pool, psz = cache_diff_k.shape[1], cache_diff_k.shape[2]
    pool_d = lambda c: c.reshape(n_odd * n_pool, psz * H_C, 2 * DH_C)
    pool_s = lambda c: jnp.transpose(c, (0, 1, 3, 4, 2)).reshape(n_odd * n_pool, H_D * DH_D, psz)
    caches = (pool_d(cache_diff_k), pool_d(cache_diff_v), pool_s(cache_sb_k), pool_s(cache_sb_v))
    tables = [page_table + j * n_pool for j in range(n_odd)]
    y_sample, ev_s, od_s = _trunk(x_sample, p_sample, even_s, [caches] * n_odd, tables, W)
    ev = lambda states, i: jnp.stack([s[i] for s in states])
    return (y_prompt, y_sample,
            ev(ev_p, 0), ev(ev_s, 0), ev(ev_p, 1), ev(ev_s, 1),
            ev(ev_p, 2), ev(ev_s, 2), ev(ev_p, 3), ev(ev_s, 3),
            ev(od_p, 0), ev(od_s, 0), ev(od_p, 1), ev(od_s, 1),
            ev(od_p, 2), ev(od_s, 2), ev(od_p, 3), ev(od_s, 3))
```

```python
import functools
import math

import jax
import jax.numpy as jnp
from jax import lax
from jax.experimental import pallas as pl
from jax.experimental.pallas import tpu as pltpu

F32 = jnp.float32
BF16 = jnp.bfloat16

LANES = 128
SUBLANES = 8
VMEM_LIMIT_BYTES = 56 * 1024 * 1024

D_MODEL = 1024
D_FF = 2816
HALF = 512
CONV_W = 31
H_B, DH_B = 4, 128
H_C, DH_C = 4, 64
H_D, DH_D = 8, 64
N_BUCKETS = 32
MAX_DIST = 128
MLSTM_CHUNK = 128
NEG_INF = -1e30
SB_DEAD = -104.0

ROW_TILE = 512
FF_CHUNK = 256
DIFF_BLK = 512
SB_BLK = 256
CONV_ROWS = 256
CONV_SUB = 32
CONV_SEQS = 16
MLSTM_SEQS = 4
DEC_PAGES = 4
DEC_DIFF_PAGES = 8
DEC_SB_SEQS = 2


def _cp(*sem):
    return pltpu.CompilerParams(dimension_semantics=sem, vmem_limit_bytes=VMEM_LIMIT_BYTES)


def _rms(x, g, eps=1e-6):
    return x * lax.rsqrt(jnp.mean(x * x, axis=-1, keepdims=True) + eps) * g


def _dot(a, b):
    return jnp.dot(a, b, preferred_element_type=F32)


def _dot_nt(a, b):
    return lax.dot_general(a, b, (((1,), (1,)), ((), ())), preferred_element_type=F32)


def _dot_tn(a, b):
    return lax.dot_general(a, b, (((0,), (0,)), ((), ())), preferred_element_type=F32)


def _log_sigmoid_neg(z):
    return -(jnp.maximum(z, 0.0) + jnp.log1p(jnp.exp(-jnp.abs(z))))


def _split_bf16(x):
    hi = x.astype(BF16)
    lo = (x - hi.astype(F32)).astype(BF16)
    return hi, lo


def _swiglu_half(x, g_ref, wi_ref, wo_ref, act_ref):
    hn = _rms(x, g_ref[...]).astype(BF16)
    for c in range(D_FF // FF_CHUNK):
        lo = c * FF_CHUNK
        gate = _dot(hn, wi_ref[:, lo:lo + FF_CHUNK])
        up = _dot(hn, wi_ref[:, D_FF + lo:D_FF + lo + FF_CHUNK])
        act_ref[:, lo:lo + FF_CHUNK] = (gate * jax.nn.sigmoid(gate) * up).astype(BF16)
    return x + 0.5 * _dot(act_ref[...], wo_ref[...])


def _ffn_kernel(x_ref, g_ref, wi_ref, wo_ref, o_ref, act_ref):
    o_ref[...] = _swiglu_half(x_ref[...], g_ref, wi_ref, wo_ref, act_ref)


def _ffn_half(h, g, wi, wo):
    m, d = h.shape
    tm = min(ROW_TILE, m)
    resident = lambda shape: pl.BlockSpec(shape, lambda i: (0, 0), pipeline_mode=pl.Buffered(1))
    return pl.pallas_call(
        _ffn_kernel,
        grid=(m // tm,),
        in_specs=[
            pl.BlockSpec((tm, d), lambda i: (i, 0)),
            pl.BlockSpec((1, d), lambda i: (0, 0)),
            resident((d, 2 * D_FF)),
            resident((D_FF, d)),
        ],
        out_specs=pl.BlockSpec((tm, d), lambda i: (i, 0)),
        out_shape=jax.ShapeDtypeStruct((m, d), F32),
        scratch_shapes=[pltpu.VMEM((tm, D_FF), BF16)],
        compiler_params=_cp("parallel"),
    )(h, g.reshape(1, d), wi, wo)


def _layer_tail_kernel(x_ref, a_ref, b_ref, wa_ref, wb_ref, g_ref, wi_ref, wo_ref, pg_ref, wg_ref, p_ref, wp_ref,
                       fg_ref, o_ref, act_ref, *, final):
    h = (x_ref[...] + _dot(a_ref[...].astype(BF16), wa_ref[...]) + _dot(b_ref[...].astype(BF16), wb_ref[...]))
    h = _swiglu_half(h, g_ref, wi_ref, wo_ref, act_ref)
    gate = jax.nn.sigmoid(_dot(_rms(h, pg_ref[...]).astype(BF16), wg_ref[...]))
    h = h + gate * _dot(p_ref[...].astype(BF16), wp_ref[...])
    if final:
        h = _rms(h, fg_ref[...])
    o_ref[...] = h


def _layer_tail(h, a, b, w_out, g, wi, wo, pg, wg, p, wp, fg, final):
    m, d = h.shape
    tm = min(ROW_TILE, m)
    pd = p.shape[1]
    row = lambda i: (i, 0)
    resident = lambda shape, idx=(0, 0): pl.BlockSpec(shape, lambda i: idx, pipeline_mode=pl.Buffered(1))
    vec = lambda: pl.BlockSpec((1, d), lambda i: (0, 0))
    return pl.pallas_call(
        functools.partial(_layer_tail_kernel, final=final),
        grid=(m // tm,),
        in_specs=[
            pl.BlockSpec((tm, d), row), pl.BlockSpec((tm, HALF), row), pl.BlockSpec((tm, HALF), row),
            resident((HALF, d)), resident((HALF, d), (1, 0)),
            vec(), resident((d, 2 * D_FF)), resident((D_FF, d)),
            vec(), resident((d, d)), pl.BlockSpec((tm, pd), row), resident((pd, d)), vec(),
        ],
        out_specs=pl.BlockSpec((tm, d), row),
        out_shape=jax.ShapeDtypeStruct((m, d), F32),
        scratch_shapes=[pltpu.VMEM((tm, D_FF), BF16)],
        compiler_params=_cp("parallel"),
    )(h, a, b, w_out, w_out, g.reshape(1, d), wi, wo, pg.reshape(1, d), wg, p, wp, fg.reshape(1, d))


def _inproj_even_kernel(x_ref, g_ref, w_ref, wgt_ref, u_ref, q_ref, k_ref, v_ref, o_ref, gt_ref):
    hn = _rms(x_ref[...], g_ref[...]).astype(BF16)

    def col(c):
        return _dot(hn, w_ref[:, c * HALF:(c + 1) * HALF])

    u_ref[...] = col(0) * jax.nn.sigmoid(col(1))
    q_ref[...] = col(2).astype(q_ref.dtype)
    k_ref[...] = (col(3) * (DH_B ** -0.5)).astype(k_ref.dtype)
    v_ref[...] = col(4).astype(v_ref.dtype)
    o_ref[...] = col(5)
    gt_ref[...] = _dot_nt(wgt_ref[...], hn)


def _inproj_even(h, g, w, wgt, qkv_dtype):
    m, d = h.shape
    tm = min(ROW_TILE, m)
    row = lambda i: (i, 0)
    out = lambda dt: jax.ShapeDtypeStruct((m, HALF), dt)
    return pl.pallas_call(
        _inproj_even_kernel,
        grid=(m // tm,),
        in_specs=[
            pl.BlockSpec((tm, d), row),
            pl.BlockSpec((1, d), lambda i: (0, 0)),
            pl.BlockSpec((d, 6 * HALF), lambda i: (0, 0)),
            pl.BlockSpec((2 * H_B, d), lambda i: (0, 0)),
        ],
        out_specs=[pl.BlockSpec((tm, HALF), row)] * 5 + [pl.BlockSpec((2 * H_B, tm), lambda i: (0, i))],
        out_shape=[out(F32), out(qkv_dtype), out(qkv_dtype), out(qkv_dtype), out(F32),
                   jax.ShapeDtypeStruct((2 * H_B, m), F32)],
        compiler_params=_cp("parallel"),
    )(h, g.reshape(1, d), w, wgt)


def _inproj_odd_kernel(x_ref, g_ref, w_ref, wkvt_ref, cq_ref, ck_ref, cv_ref, sq_ref, sk_ref, sv_ref, *, time_minor):
    hn = _rms(x_ref[...], g_ref[...]).astype(BF16)

    def col(c):
        return _dot(hn, w_ref[:, c * HALF:(c + 1) * HALF])

    cq_ref[...] = (col(0) * (DH_C ** -0.5)).astype(cq_ref.dtype)
    ck_ref[...] = col(1)
    cv_ref[...] = col(2)
    sq_ref[...] = (col(3) * (DH_D ** -0.5)).astype(sq_ref.dtype)
    if time_minor:
        sk_ref[0] = _dot_nt(wkvt_ref[0:HALF, :], hn)
        sv_ref[0] = _dot_nt(wkvt_ref[HALF:2 * HALF, :], hn)
    else:
        sk_ref[...] = col(4)
        sv_ref[...] = col(5)


def _inproj_odd(h, g, w, wkvt, q_dtype, b, t, time_minor):
    m, d = h.shape
    tm = min(ROW_TILE, m)
    row = lambda i: (i, 0)
    out = lambda dt: jax.ShapeDtypeStruct((m, HALF), dt)
    rows = pl.BlockSpec((tm, HALF), row)
    if time_minor:
        assert t % tm == 0
        nt = t // tm
        kv_spec = pl.BlockSpec((1, HALF, tm), lambda i: (i // nt, 0, i % nt))
        kv_shape = jax.ShapeDtypeStruct((b, HALF, t), F32)
    else:
        kv_spec, kv_shape = rows, out(F32)
    return pl.pallas_call(
        functools.partial(_inproj_odd_kernel, time_minor=time_minor),
        grid=(m // tm,),
        in_specs=[
            pl.BlockSpec((tm, d), row),
            pl.BlockSpec((1, d), lambda i: (0, 0)),
            pl.BlockSpec((d, 6 * HALF), lambda i: (0, 0)),
            pl.BlockSpec((2 * HALF, d), lambda i: (0, 0)),
        ],
        out_specs=[rows] * 4 + [kv_spec] * 2,
        out_shape=[out(q_dtype), out(F32), out(F32), out(q_dtype), kv_shape, kv_shape],
        compiler_params=_cp("parallel"),
    )(h, g.reshape(1, d), w, wkvt)


_HIST = 32


def _conv_kernel(u_ref, buf_ref, taps_ref, cb_ref, lg_ref, lb_ref, a_ref, st_ref, win_ref, sh_ref, *, tt, sub, bb):
    t = pl.program_id(1)
    pad = _HIST - (CONV_W - 1)
    span = tt + _HIST - SUBLANES

    for s in range(bb):
        @pl.when(t == 0)
        def _(s=s):
            win_ref[s, 0:SUBLANES, :] = jnp.zeros((SUBLANES, HALF), F32)
            win_ref[s, pad:_HIST, :] = buf_ref[s]

        win_ref[s, _HIST:_HIST + tt, :] = u_ref[s]
        for r in range(1, SUBLANES):
            sh_ref[s, r - 1] = win_ref[s, pl.ds(r, span), :]
        for rb in range(tt // sub):
            acc = jnp.zeros((sub, HALF), F32)
            for w in range(CONV_W):
                a8, r = divmod(pad + w, SUBLANES)
                lo = rb * sub + a8 * SUBLANES
                src = win_ref[s, lo:lo + sub, :] if r == 0 else sh_ref[s, r - 1, lo:lo + sub, :]
                acc = acc + src * taps_ref[w:w + 1, :]
            c = acc + cb_ref[...]
            mu = jnp.mean(c, axis=-1, keepdims=True)
            var = jnp.mean(jnp.square(c - mu), axis=-1, keepdims=True)
            cn = (c - mu) * lax.rsqrt(var + 1e-5) * lg_ref[...] + lb_ref[...]
            a_ref[s, rb * sub:(rb + 1) * sub, :] = (cn * jax.nn.sigmoid(cn)).astype(a_ref.dtype)

        @pl.when(t == pl.num_programs(1) - 1)
        def _(s=s):
            st_ref[s] = win_ref[s, tt + pad:tt + _HIST, :]

        win_ref[s, 0:_HIST, :] = win_ref[s, tt:tt + _HIST, :]


def _conv_module(u, buf, taps, cb, lg, lb, out_dtype):
    b, t, _ = u.shape
    tt = min(CONV_ROWS, t)
    sub = min(CONV_SUB, tt)
    bb = 1 if t > tt else math.gcd(b, CONV_SEQS)
    vec = lambda: pl.BlockSpec((1, HALF), lambda i, j: (0, 0))
    return pl.pallas_call(
        functools.partial(_conv_kernel, tt=tt, sub=sub, bb=bb),
        grid=(b // bb, t // tt),
        in_specs=[
            pl.BlockSpec((bb, tt, HALF), lambda i, j: (i, j, 0)),
            pl.BlockSpec((bb, CONV_W - 1, HALF), lambda i, j: (i, 0, 0)),
            pl.BlockSpec((CONV_W, HALF), lambda i, j: (0, 0)),
            vec(), vec(), vec(),
        ],
        out_specs=[
            pl.BlockSpec((bb, tt, HALF), lambda i, j: (i, j, 0)),
            pl.BlockSpec((bb, CONV_W - 1, HALF), lambda i, j: (i, 0, 0)),
        ],
        out_shape=[jax.ShapeDtypeStruct((b, t, HALF), out_dtype),
                   jax.ShapeDtypeStruct((b, CONV_W - 1, HALF), F32)],
        scratch_shapes=[pltpu.VMEM((bb, _HIST + tt, HALF), F32),
                        pltpu.VMEM((bb, SUBLANES - 1, tt + _HIST - SUBLANES, HALF), F32)],
        compiler_params=_cp("parallel", "arbitrary"),
    )(u, buf, taps, cb.reshape(1, HALF), lg.reshape(1, HALF), lb.reshape(1, HALF))


def _scan_lanes(x, op, fill):
    lane = lax.broadcasted_iota(jnp.int32, x.shape, 1)
    sh = 1
    while sh < x.shape[1]:
        x = op(x, jnp.where(lane >= sh, pltpu.roll(x, sh, 1), fill))
        sh *= 2
    return x


def _mlstm_kernel(q_ref, k_ref, v_ref, o_ref, g_ref, gb_ref, c0_ref, n0_ref, m0_ref,
                  h_ref, c_ref, n_ref, m_ref, cx_ref, *, lr, bb):
    L = MLSTM_CHUNK
    ng = 2 * H_B
    nr = bb * ng

    @pl.when(pl.program_id(1) == 0)
    def _():
        m_ref[...] = m0_ref[...]
        for s in range(bb):
            for h in range(H_B):
                cx_ref[s, h, :, 0:DH_B] = c0_ref[s, h]
                cx_ref[s, h, :, DH_B:2 * DH_B] = jnp.transpose(jnp.broadcast_to(n0_ref[s, h:h + 1, :], (DH_B, DH_B)))

    def rows(ref, s):
        x = ref[s]
        if lr < L:
            x = jnp.concatenate([x.astype(F32), jnp.zeros((L - lr, HALF), F32)], axis=0)
        return x.astype(BF16)

    g = g_ref[...].reshape(nr, L) + gb_ref[...]
    row = lax.broadcasted_iota(jnp.int32, g.shape, 0)
    is_li = row % ng < H_B
    bcum = _scan_lanes(jnp.where(is_li, 0.0, _log_sigmoid_neg(-g)), jnp.add, 0.0)
    b = pltpu.roll(bcum, nr - H_B, 0)
    a = jnp.where(is_li, g - b, 0.0)
    m_old = m_ref[...].reshape(nr, LANES)
    mx = jnp.maximum(m_old, _scan_lanes(a, jnp.maximum, -3e38))
    b_last = b[:, L - 1:L]
    m_new = b_last + mx[:, L - 1:L]
    w_inter = jnp.exp(m_old - mx)
    inv_floor = jnp.exp(-(b + mx))
    w_key = jnp.exp(a + b_last - m_new)
    decay = jnp.exp(b_last + m_old - m_new)
    m_ref[...] = jnp.where(is_li, jnp.broadcast_to(m_new, (nr, LANES)), 0.0).reshape(bb, ng, LANES)
    packed = jnp.concatenate([mx, w_inter, inv_floor, w_key] + [jnp.zeros((L - 4 * nr, L), F32)] * (4 * nr < L), axis=0)
    cols = jnp.transpose(packed)
    tpos = lax.broadcasted_iota(jnp.int32, (L, L), 0)
    spos = lax.broadcasted_iota(jnp.int32, (L, L), 1)
    causal = spos <= tpos
    ones = jnp.ones((L, DH_B), BF16)
    for s in range(bb):
        q, k, v = rows(q_ref, s), rows(k_ref, s), rows(v_ref, s)
        outs = []
        for h in range(H_B):
            sl = slice(h * DH_B, (h + 1) * DH_B)
            qh, kh = q[:, sl], k[:, sl]
            v1 = jnp.concatenate([v[:, sl], ones], axis=1)
            i = s * ng + h
            col = lambda vec: cols[:, vec * nr + i:vec * nr + i + 1]
            gate = jnp.where(causal, jnp.exp(a[i:i + 1, :] - col(0)), 0.0)
            sc = _dot_nt(qh, kh) * gate
            cx = cx_ref[s, h]
            mix = col(1) * _dot(qh, cx.astype(BF16)) + _dot(sc.astype(BF16), v1)
            outs.append(mix[:, 0:DH_B] / jnp.maximum(jnp.abs(mix[:, DH_B:2 * DH_B]), col(2)))
            kw = (kh.astype(F32) * col(3)).astype(BF16)
            cx_ref[s, h] = decay[i:i + 1, 0:1] * cx + _dot_tn(kw, v1)
        hs = jnp.concatenate(outs, axis=1)
        h_ref[s] = (jax.nn.sigmoid(o_ref[s]) * hs[0:lr]).astype(h_ref.dtype)

    @pl.when(pl.program_id(1) == pl.num_programs(1) - 1)
    def _():
        for s in range(bb):
            for h in range(H_B):
                c_ref[s, h] = cx_ref[s, h, :, 0:DH_B]
                n_ref[s, h:h + 1, :] = jnp.transpose(cx_ref[s, h, :, DH_B:2 * DH_B])[0:1, :]


def _mlstm(q, k, v, o, gt, gate_b, c0, n0, m0, out_dtype):
    b, t, _ = q.shape
    L = MLSTM_CHUNK
    lr = min(L, t)
    nc = t // lr
    m0b = jnp.broadcast_to(jnp.pad(m0, ((0, 0), (0, SUBLANES - H_B)))[:, :, None], (b, SUBLANES, LANES))
    bb = math.gcd(b, MLSTM_SEQS)
    gb = jnp.broadcast_to(jnp.tile(gate_b, bb).reshape(bb * 2 * H_B, 1), (bb * 2 * H_B, L))
    blk = pl.BlockSpec((bb, lr, HALF), lambda i, j: (i, j, 0))
    st = lambda *s: pl.BlockSpec((bb,) + s, lambda i, j: (i,) + (0,) * len(s))
    h, c1, n1, m1 = pl.pallas_call(
        functools.partial(_mlstm_kernel, lr=lr, bb=bb),
        grid=(b // bb, nc),
        in_specs=[blk, blk, blk, blk,
                  pl.BlockSpec((bb, 2 * H_B, L), lambda i, j: (i, 0, j)),
                  pl.BlockSpec((bb * 2 * H_B, L), lambda i, j: (0, 0)),
                  st(H_B, DH_B, DH_B), st(H_B, DH_B), st(SUBLANES, LANES)],
        out_specs=[blk, st(H_B, DH_B, DH_B), st(H_B, DH_B), st(SUBLANES, LANES)],
        out_shape=[jax.ShapeDtypeStruct((b, t, HALF), out_dtype),
                   jax.ShapeDtypeStruct((b, H_B, DH_B, DH_B), F32),
                   jax.ShapeDtypeStruct((b, H_B, DH_B), F32),
                   jax.ShapeDtypeStruct((b, SUBLANES, LANES), F32)],
        scratch_shapes=[pltpu.VMEM((bb, H_B, DH_B, 2 * DH_B), F32)],
        compiler_params=_cp("parallel", "arbitrary"),
    )(q, k, v, o, gt, gb, c0, n0, m0b)
    return h, c1, n1, m1[:, :H_B, 0]


def _t5_bucket(dist):
    n = jnp.maximum(dist, 0)
    exact = N_BUCKETS // 2
    nf = jnp.maximum(n, 1).astype(F32)
    large = exact + (jnp.log(nf / exact) / math.log(MAX_DIST / exact) * (N_BUCKETS - exact)).astype(jnp.int32)
    return jnp.where(n < exact, n, jnp.minimum(large, N_BUCKETS - 1))


def _bias_of_dist(rel_bias, dist):
    onehot = jax.nn.one_hot(_t5_bucket(dist), N_BUCKETS, dtype=F32)
    b = jnp.einsum('...k,kh->h...', onehot, rel_bias.astype(F32), precision=lax.Precision.HIGHEST)
    return jnp.where(dist >= 0, b, NEG_INF)


def _lambda(lp_ref, lam_init):
    lp = lp_ref[...]
    s1 = jnp.sum(lp[0:1] * lp[1:2], axis=-1, keepdims=True)
    s2 = jnp.sum(lp[2:3] * lp[3:4], axis=-1, keepdims=True)
    return jnp.exp(s1) - jnp.exp(s2) + lam_init


def _head_norm(x, hg, lam_init):
    return x * lax.rsqrt(jnp.mean(x * x, axis=-1, keepdims=True) + 1e-6) * hg * (1.0 - lam_init)


def _diff_attn_kernel(q_ref, k_ref, v_ref, bd_ref, bp_ref, lp_ref, hg_ref, o_ref, kb_ref, vt_ref, *, lam_init):
    blk = DIFF_BLK
    i = pl.program_id(2)

    @pl.when(i == 0)
    def _():
        kb_ref[...] = k_ref[...].astype(BF16)
        for c in range(vt_ref.shape[0]):
            vt_ref[c] = jnp.transpose(v_ref[c * blk:(c + 1) * blk, :]).astype(BF16)

    q = q_ref[...]
    lane = lax.broadcasted_iota(jnp.int32, q.shape, 1)
    zero = jnp.zeros_like(q)
    q2 = jnp.concatenate([jnp.where(lane < DH_C, q, zero), jnp.where(lane >= DH_C, q, zero)], axis=0)

    def scores(j):
        off = pl.multiple_of(j * blk, blk)
        return _dot_nt(kb_ref[pl.ds(off, blk), :], q2)

    def update(carry, s, vt):
        m, l, acc = carry
        mn = jnp.maximum(m, jnp.max(s, axis=0, keepdims=True))
        p = jnp.exp(s - mn)
        al = jnp.exp(m - mn)
        return mn, al * l + jnp.sum(p, axis=0, keepdims=True), al * acc + _dot(vt, p.astype(BF16))

    s = scores(i) + bd_ref[0]
    m = jnp.max(s, axis=0, keepdims=True)
    p = jnp.exp(s - m)
    carry = (m, jnp.sum(p, axis=0, keepdims=True), _dot(vt_ref[i], p.astype(BF16)))

    def prev_step(c):
        return update(c, scores(i - 1) + bp_ref[0], vt_ref[i - 1])

    carry = lax.cond(i >= 1, prev_step, lambda c: c, carry)

    def far_step(j, c):
        return update(c, scores(j), vt_ref[j])

    m, l, acc = lax.fori_loop(0, jnp.maximum(i - 1, 0), far_step, carry)
    o = acc / l
    o = jnp.transpose(o[:, 0:blk] - _lambda(lp_ref, lam_init) * o[:, blk:2 * blk])
    o_ref[...] = _head_norm(o, hg_ref[0], lam_init).astype(o_ref.dtype)


def _diff_attention(cq, ck, cv, b, t, rel_bias, lam_p, head_g, lam_init):
    blk = DIFF_BLK
    nq = t // blk
    r = jnp.arange(blk, dtype=jnp.int32)
    d0 = r[None, :] - r[:, None]
    assert blk + 1 >= MAX_DIST
    far = _bias_of_dist(rel_bias, jnp.full((1, 1), 2 * blk, jnp.int32))
    rel = lambda d: jnp.tile(jnp.where(d >= 0, _bias_of_dist(rel_bias, d) - far, NEG_INF), (1, 1, 2))
    bd = rel(d0)
    bp = rel(d0 + blk)
    per_head = lambda *s: pl.BlockSpec((1,) + s, lambda bi, h, i: (h, 0, 0))
    return pl.pallas_call(
        functools.partial(_diff_attn_kernel, lam_init=lam_init),
        grid=(b, H_C, nq),
        in_specs=[
            pl.BlockSpec((blk, LANES), lambda bi, h, i: (bi * nq + i, h)),
            pl.BlockSpec((t, LANES), lambda bi, h, i: (bi, h)),
            pl.BlockSpec((t, LANES), lambda bi, h, i: (bi, h)),
            per_head(blk, 2 * blk), per_head(blk, 2 * blk),
            pl.BlockSpec((4, DH_C), lambda bi, h, i: (0, 0)),
            per_head(1, LANES),
        ],
        out_specs=pl.BlockSpec((blk, LANES), lambda bi, h, i: (bi * nq + i, h)),
        out_shape=jax.ShapeDtypeStruct((b * t, HALF), BF16),
        scratch_shapes=[pltpu.VMEM((t, LANES), BF16), pltpu.VMEM((nq, LANES, blk), BF16)],
        compiler_params=_cp("parallel", "parallel", "arbitrary"),
    )(cq, ck, cv, bd, bp, lam_p, head_g.reshape(H_C, 1, LANES))


def _sb_tile(q2, kt, vt, upper, r, mask):
    z = _dot(q2, kt)
    lk = _log_sigmoid_neg(z)
    if mask is not None:
        lk = jnp.where(mask, lk, 0.0)
    hi, lo = _split_bf16(lk)
    after = _dot(hi, upper) + _dot(lo, upper) + r
    w = jnp.exp(lk + z + after)
    if mask is not None:
        w = jnp.where(mask, w, 0.0)
    return _dot_nt(w.astype(BF16), vt), r + jnp.sum(lk, axis=-1, keepdims=True)


def _strict_upper(n):
    j = lax.broadcasted_iota(jnp.int32, (n, n), 0)
    s = lax.broadcasted_iota(jnp.int32, (n, n), 1)
    return jnp.where(j > s, 1.0, 0.0).astype(BF16)


def _sb_attn_kernel(q_ref, k_ref, v_ref, o_ref, kb_ref, vb_ref):
    blk = SB_BLK
    i = pl.program_id(2)

    @pl.when(i == 0)
    def _():
        for c in range(kb_ref.shape[0]):
            kb_ref[c] = k_ref[0, :, c * blk:(c + 1) * blk].astype(BF16)
            vb_ref[c] = v_ref[0, :, c * blk:(c + 1) * blk].astype(BF16)

    q = q_ref[...]
    lane = lax.broadcasted_iota(jnp.int32, q.shape, 1)
    zero = jnp.zeros_like(q)
    q2 = jnp.concatenate([jnp.where(lane < DH_D, q, zero), jnp.where(lane >= DH_D, q, zero)], axis=0)
    upper = _strict_upper(blk)
    tpos = lax.broadcasted_iota(jnp.int32, (2 * blk, blk), 0) % blk
    spos = lax.broadcasted_iota(jnp.int32, (2 * blk, blk), 1)

    def tile(j, r, mask):
        return _sb_tile(q2, kb_ref[j], vb_ref[j], upper, r, mask)

    acc, r = tile(i, jnp.zeros((2 * blk, 1), F32), spos < tpos)

    def cond(c):
        j, _, r = c
        return jnp.logical_and(j >= 0, jnp.max(r) > SB_DEAD)

    def body(c):
        j, acc, r = c
        pv, r = tile(j, r, None)
        return j - 1, acc + pv, r

    _, acc, _ = lax.while_loop(cond, body, (i - 1, acc, r))
    o_ref[...] = jnp.where(lane < DH_D, acc[0:blk], acc[blk:2 * blk]).astype(o_ref.dtype)


def _sb_attention(sq, sk, sv, b, t):
    blk = SB_BLK
    nq = t // blk
    return pl.pallas_call(
        _sb_attn_kernel,
        grid=(b, HALF // LANES, nq),
        in_specs=[
            pl.BlockSpec((blk, LANES), lambda bi, h, i: (bi * nq + i, h)),
            pl.BlockSpec((1, LANES, t), lambda bi, h, i: (bi, h, 0)),
            pl.BlockSpec((1, LANES, t), lambda bi, h, i: (bi, h, 0)),
        ],
        out_specs=pl.BlockSpec((blk, LANES), lambda bi, h, i: (bi * nq + i, h)),
        out_shape=jax.ShapeDtypeStruct((b * t, HALF), BF16),
        scratch_shapes=[pltpu.VMEM((nq, LANES, blk), BF16), pltpu.VMEM((nq, LANES, blk), BF16)],
        compiler_params=_cp("parallel", "parallel", "arbitrary"),
    )(sq, sk, sv)


_DEC_ROWS = 64


def _pad_rows(x, n):
    return jnp.concatenate([x, jnp.zeros((n - x.shape[0], x.shape[1]), F32)], axis=0).astype(BF16)


def _dec_diff_kernel(pt_ref, cq_ref, ckf_ref, cvf_ref, bias_ref, bnew_ref, lp_ref, hg_ref, dk_hbm, dv_hbm,
                     oc_ref, qa_ref, m_ref, l_ref, acc_ref, kbuf_ref, vbuf_ref, sem_ref,
                     *, ts, lam_init, npg, n_pages):
    g = pl.program_id(1)
    n_groups = pl.num_programs(1)
    step = pl.program_id(0) * n_groups + g
    slot = lax.rem(step, 2)

    def page_copies(st, sl):
        seq, grp = lax.div(st, n_groups), lax.rem(st, n_groups)
        out = []
        for p in range(npg):
            page = pt_ref[seq * n_pages + n_pages - 1 - (grp * npg + p)]
            out.append(pltpu.make_async_copy(dk_hbm.at[page], kbuf_ref.at[sl, p], sem_ref.at[sl, 2 * p]))
            out.append(pltpu.make_async_copy(dv_hbm.at[page], vbuf_ref.at[sl, p], sem_ref.at[sl, 2 * p + 1]))
        return out

    def start(st, sl):
        for i, c in enumerate(page_copies(st, sl)):
            c.start(priority=i % 2)

    @pl.when(step == 0)
    def _():
        start(step, slot)

    @pl.when(step + 1 < pl.num_programs(0) * n_groups)
    def _():
        start(step + 1, 1 - slot)

    def update(tiles):
        m = m_ref[...]
        mn = m
        for s, _ in tiles:
            mn = jnp.maximum(mn, jnp.max(s, axis=-1, keepdims=True))
        al = jnp.exp(m - mn)
        l = al * l_ref[...]
        acc = al * acc_ref[...]
        for s, v in tiles:
            p = jnp.exp(s - mn)
            l = l + jnp.sum(p, axis=-1, keepdims=True)
            acc = acc + _dot(p.astype(BF16), v)
        m_ref[...] = mn
        l_ref[...] = l
        acc_ref[...] = acc

    @pl.when(g == 0)
    def _():
        cq = cq_ref[0]
        half = lax.broadcasted_iota(jnp.int32, (ts, LANES), 1) >= DH_C
        qa_ref[...] = jnp.concatenate(
            [jnp.where(half if c % 2 else jnp.logical_not(half), cq[:, (c // 2) * LANES:(c // 2 + 1) * LANES], 0.0)
             for c in range(2 * H_C)], axis=0).astype(BF16)
        m_ref[...] = jnp.full_like(m_ref, NEG_INF)
        l_ref[...] = jnp.zeros_like(l_ref)
        acc_ref[...] = jnp.zeros_like(acc_ref)
        update([(_dot_nt(qa_ref[...], _pad_rows(ckf_ref[0], LANES)) + bnew_ref[...], _pad_rows(cvf_ref[0], LANES))])

    for c in page_copies(step, slot):
        c.wait()
    qa = qa_ref[...]
    update([(_dot_nt(qa, kbuf_ref[slot, p].astype(BF16)) + bias_ref[g * npg + p], vbuf_ref[slot, p].astype(BF16))
            for p in range(npg)])

    @pl.when(g == pl.num_programs(1) - 1)
    def _():
        lam = _lambda(lp_ref, lam_init)
        o = acc_ref[...] / l_ref[...]
        for h in range(H_C):
            r0 = h * 2 * ts
            oh = o[r0:r0 + ts] - lam * o[r0 + ts:r0 + 2 * ts]
            oc_ref[0, :, h * LANES:(h + 1) * LANES] = _head_norm(oh, hg_ref[h:h + 1, :], lam_init)


def _sb_queries(sq):
    lane = lax.broadcasted_iota(jnp.int32, sq.shape, 1)
    return jnp.concatenate(
        [jnp.where((lane >= c * DH_D) & (lane < (c + 1) * DH_D), sq, 0.0) for c in range(H_D)], axis=0).astype(BF16)


def _sb_fold(tiles, upper, r, acc):
    for z, pv, mask in tiles:
        lk = _log_sigmoid_neg(z)
        if mask is not None:
            lk = jnp.where(mask, lk, 0.0)
        hi, lo = _split_bf16(lk)
        w = jnp.exp(lk + z + _dot(hi, upper) + _dot(lo, upper) + r)
        if mask is not None:
            w = jnp.where(mask, w, 0.0)
        acc = acc + pv(w.astype(BF16))
        r = r + jnp.sum(lk, axis=-1, keepdims=True)
    return r, acc


def _sb_page_tiles(qs, pages):
    return [(_dot(qs, pages[2 * p][0].astype(BF16)), functools.partial(_dot_nt, b=pages[2 * p + 1][0].astype(BF16)),
             None) for p in range(len(pages) // 2)]


def _sb_heads_to_lanes(acc, ts):
    lane = lax.broadcasted_iota(jnp.int32, (ts, LANES), 1)
    outs = []
    for pr in range(H_D // 2):
        sl = slice(pr * LANES, (pr + 1) * LANES)
        r0 = pr * 2 * ts
        outs.append(jnp.where(lane < DH_D, acc[r0:r0 + ts, sl], acc[r0 + ts:r0 + 2 * ts, sl]))
    return jnp.concatenate(outs, axis=1)


def _dec_sb_first_kernel(pt_ref, sq_ref, sk_ref, sv_ref, *rest, ts, npg, bb):
    pages = rest[:2 * npg * bb]
    os_ref, acc_ref, r_ref, alive_ref = rest[2 * npg * bb:]
    psz = pages[0].shape[2]
    upper = _strict_upper(psz)
    tq = lax.broadcasted_iota(jnp.int32, (_DEC_ROWS, psz), 0) % ts
    kpos = lax.broadcasted_iota(jnp.int32, (_DEC_ROWS, psz), 1)
    for s in range(bb):
        qs = _sb_queries(sq_ref[s])
        sv_new = _pad_rows(sv_ref[s], psz)
        tiles = [(_dot_nt(qs, _pad_rows(sk_ref[s], psz)), lambda w, sv_new=sv_new: _dot(w, sv_new), kpos < tq)]
        r, acc = _sb_fold(tiles + _sb_page_tiles(qs, pages[2 * npg * s:2 * npg * (s + 1)]), upper,
                          jnp.zeros((_DEC_ROWS, 1), F32), jnp.zeros((_DEC_ROWS, HALF), F32))
        os_ref[s] = _sb_heads_to_lanes(acc, ts)
        acc_ref[s] = acc
        r_ref[s] = jnp.broadcast_to(r, (_DEC_ROWS, LANES))
        alive = jnp.max(r, axis=0, keepdims=True) > SB_DEAD
        alive_ref[s] = jnp.broadcast_to(jnp.where(alive, 1, 0), (SUBLANES, LANES)).astype(jnp.int32)


def _dec_sb_rest_kernel(pt_ref, al_ref, sq_ref, acc_in_ref, r_in_ref, *rest, ts, npg, nrest):
    pages = rest[:2 * nrest]
    os_ref, acc_ref, r_ref = rest[2 * nrest:]
    acc_ref[...] = acc_in_ref[0]
    r_ref[...] = r_in_ref[0][:, 0:1]

    @pl.when(al_ref[pl.program_id(0)] == 1)
    def _():
        qs = _sb_queries(sq_ref[0])
        upper = _strict_upper(pages[0].shape[2])
        for grp in range(nrest // npg):
            @pl.when(jnp.max(r_ref[...]) > SB_DEAD)
            def _(grp=grp):
                tiles = _sb_page_tiles(qs, pages[2 * npg * grp:2 * npg * (grp + 1)])
                r, acc = _sb_fold(tiles, upper, r_ref[...], acc_ref[...])
                r_ref[...] = r
                acc_ref[...] = acc

    os_ref[0] = _sb_heads_to_lanes(acc_ref[...], ts)


def _decode_attention(cq, sq, new_rows, caches, page_table, rel_bias, lam_p, head_g, lam_init):
    b, ts, _ = cq.shape
    n_pages = page_table.shape[1]
    psz = caches[2].shape[2]
    past = n_pages * psz
    npd = math.gcd(n_pages, DEC_DIFF_PAGES)
    nps = math.gcd(n_pages, DEC_PAGES)
    nrest = n_pages - nps
    nkn = LANES // H_C
    assert _DEC_ROWS == 2 * H_C * ts == H_D * ts and ts <= nkn
    ck, cv, sk, sv = new_rows
    ckf = ck.reshape(b, ts * H_C, LANES)
    cvf = cv.reshape(b, ts * H_C, LANES)
    dk, dv, skt, svt = caches
    pt = page_table.reshape(-1)

    def table(base):
        base = jnp.moveaxis(base, 0, -3)
        own = jnp.arange(H_C)[:, None, None, None] == jnp.arange(H_C)[None, None, None, :]
        tab = jnp.where(own, base[..., None], NEG_INF)
        tab = jnp.broadcast_to(tab[..., :, None, :, :, :], tab.shape[:-3] + (2,) + tab.shape[-3:])
        return tab.reshape(tab.shape[:-5] + (_DEC_ROWS, tab.shape[-2] * H_C))

    tq = jnp.arange(ts, dtype=jnp.int32)
    kpos = (jnp.arange(n_pages - 1, -1, -1, dtype=jnp.int32)[:, None] * psz
            + jnp.arange(psz, dtype=jnp.int32)[None, :])
    dist = past + tq[None, :, None] - kpos[:, None, :]
    bias = table(_bias_of_dist(rel_bias, dist))
    knew = jnp.arange(nkn, dtype=jnp.int32)
    dnew = jnp.where(knew[None, :] < ts, tq[:, None] - knew[None, :], -1)
    bnew = table(_bias_of_dist(rel_bias, dnew))

    page = lambda idx: pl.BlockSpec((1, HALF, LANES), idx)
    const = lambda shape: pl.BlockSpec(shape, lambda *_: (0,) * len(shape))

    row = pl.BlockSpec((1, ts, HALF), lambda bi, g, pt: (bi, 0, 0))
    rowf = pl.BlockSpec((1, ts * H_C, LANES), lambda bi, g, pt: (bi, 0, 0))
    hbm = pl.BlockSpec(memory_space=pl.ANY)
    oc = pl.pallas_call(
        functools.partial(_dec_diff_kernel, ts=ts, lam_init=lam_init, npg=npd, n_pages=n_pages),
        grid_spec=pltpu.PrefetchScalarGridSpec(
            num_scalar_prefetch=1,
            grid=(b, n_pages // npd),
            in_specs=[row, rowf, rowf, const((n_pages, _DEC_ROWS, psz * H_C)), const((_DEC_ROWS, LANES)),
                      const((4, DH_C)), const((H_C, LANES)), hbm, hbm],
            out_specs=row,
            scratch_shapes=[pltpu.VMEM((_DEC_ROWS, LANES), BF16), pltpu.VMEM((_DEC_ROWS, 1), F32),
                            pltpu.VMEM((_DEC_ROWS, 1), F32), pltpu.VMEM((_DEC_ROWS, LANES), F32),
                            pltpu.VMEM((2, npd, psz * H_C, LANES), F32), pltpu.VMEM((2, npd, psz * H_C, LANES), F32),
                            pltpu.SemaphoreType.DMA((2, 2 * npd))],
        ),
        out_shape=jax.ShapeDtypeStruct((b, ts, HALF), F32),
        compiler_params=_cp("arbitrary", "arbitrary"),
    )(pt, cq, ckf, cvf, bias, bnew, lam_p, head_g.reshape(H_C, LANES), dk, dv)

    bb = math.gcd(b, DEC_SB_SEQS)
    rows = lambda n, *s: pl.BlockSpec((n,) + s, lambda bi, *_: (bi,) + (0,) * len(s))
    sspecs, sargs = [], []
    for s in range(bb):
        for p in range(nps):
            for c in (skt, svt):
                sspecs.append(page(lambda bi, pt, s=s, p=p: (pt[(bi * bb + s) * n_pages + n_pages - 1 - p], 0, 0)))
                sargs.append(c)
    os_first, acc, r, alive = pl.pallas_call(
        functools.partial(_dec_sb_first_kernel, ts=ts, npg=nps, bb=bb),
        grid_spec=pltpu.PrefetchScalarGridSpec(
            num_scalar_prefetch=1,
            grid=(b // bb,),
            in_specs=[rows(bb, ts, HALF)] * 3 + sspecs,
            out_specs=[rows(bb, ts, HALF), rows(bb, _DEC_ROWS, HALF), rows(bb, _DEC_ROWS, LANES),
                       rows(bb, SUBLANES, LANES)],
        ),
        out_shape=[jax.ShapeDtypeStruct((b, ts, HALF), F32), jax.ShapeDtypeStruct((b, _DEC_ROWS, HALF), F32),
                   jax.ShapeDtypeStruct((b, _DEC_ROWS, LANES), F32),
                   jax.ShapeDtypeStruct((b, SUBLANES, LANES), jnp.int32)],
        compiler_params=_cp("parallel"),
    )(pt, sq, sk, sv, *sargs)
    if nrest == 0:
        return oc, os_first

    alive = alive[:, 0, 0]

    def rest_pages():
        rspecs, rargs = [], []
        for p in range(nrest):
            for c in (skt, svt):
                rspecs.append(page(lambda bi, pt, al, p=p: (
                    jnp.where(al[bi] == 1, pt[bi * n_pages + n_pages - 1 - nps - p], pt[0]), 0, 0)))
                rargs.append(c)
        return pl.pallas_call(
            functools.partial(_dec_sb_rest_kernel, ts=ts, npg=nps, nrest=nrest),
            grid_spec=pltpu.PrefetchScalarGridSpec(
                num_scalar_prefetch=2,
                grid=(b,),
                in_specs=[rows(1, ts, HALF), rows(1, _DEC_ROWS, HALF), rows(1, _DEC_ROWS, LANES)] + rspecs,
                out_specs=rows(1, ts, HALF),
                scratch_shapes=[pltpu.VMEM((_DEC_ROWS, HALF), F32), pltpu.VMEM((_DEC_ROWS, 1), F32)],
            ),
            out_shape=jax.ShapeDtypeStruct((b, ts, HALF), F32),
            compiler_params=_cp("arbitrary"),
        )(pt, alive, sq, acc, r, *rargs)

    os_ = lax.cond(jnp.any(alive == 1), rest_pages, lambda: os_first)
    return oc, os_


def _trunk(x, p, even_states, odd_past, page_table, W):
    b, t, d = x.shape
    m = b * t
    prompt = odd_past is None
    act = BF16 if prompt else F32
    h = x.reshape(m, d)
    depth = p.shape[0]
    new_even, new_odd = [], []
    for l in range(depth):
        j = l // 2
        h = _ffn_half(h, W['ffn_norm1'][l], W['ffn1_wi'][l], W['ffn1_wo'][l])
        if l % 2 == 0:
            buf, c0, n0, m0 = even_states[j]
            u, q, k, v, o, gt = _inproj_even(h, W['mix_norm'][l], W['ev_w_in'][j], W['ev_w_gt'][j], act)
            a_out, buf1 = _conv_module(u.reshape(b, t, HALF), buf, W['ev_conv_w'][j], W['ev_conv_b'][j],
                                       W['ev_ln_g'][j], W['ev_ln_b'][j], act)
            gt = jnp.moveaxis(gt.reshape(2 * H_B, b, t), 1, 0)
            if t < MLSTM_CHUNK:
                padv = jnp.where(jnp.arange(2 * H_B) < H_B, NEG_INF, -NEG_INF).astype(F32)
                gt = jnp.concatenate(
                    [gt, jnp.broadcast_to(padv[None, :, None], (b, 2 * H_B, MLSTM_CHUNK - t))], axis=2)
            r3 = lambda a: a.reshape(b, t, HALF)
            b_out, c1, n1, m1 = _mlstm(r3(q), r3(k), r3(v), r3(o), gt, W['ev_gate_b'][j], c0, n0, m0, act)
            new_even.append((buf1, c1, n1, m1))
            mix_a, mix_b, w_out = a_out.reshape(m, HALF), b_out.reshape(m, HALF), W['ev_w_out'][j]
        else:
            lam_init = 0.8 - 0.6 * math.exp(-0.3 * l)
            cq, ck, cv, sq, sk, sv = _inproj_odd(h, W['mix_norm'][l], W['od_w_in'][j], W['od_w_kvt'][j], act, b, t,
                                                 time_minor=prompt)
            heads = lambda a, nh: a.reshape(b, t, nh, HALF // nh)
            if prompt:
                sb_rows = lambda a: jnp.transpose(a.reshape(b, H_D, DH_D, t), (0, 3, 1, 2))
            else:
                sb_rows = lambda a: heads(a, H_D)
            new_odd.append((heads(ck, H_C), heads(cv, H_C), sb_rows(sk), sb_rows(sv)))
            if prompt:
                oc = _diff_attention(cq, ck, cv, b, t, W['rel_bias'], W['od_lambda'][j], W['od_head_g'][j], lam_init)
                os_ = _sb_attention(sq, sk, sv, b, t)
            else:
                r3 = lambda a: a.reshape(b, t, HALF)
                oc, os_ = _decode_attention(r3(cq), r3(sq), [r3(a) for a in (ck, cv, sk, sv)], odd_past[j],
                                            page_table[j], W['rel_bias'], W['od_lambda'][j], W['od_head_g'][j],
                                            lam_init)
            mix_a, mix_b, w_out = oc.reshape(m, HALF), os_.reshape(m, HALF), W['od_w_out'][j]
        h = _layer_tail(h, mix_a, mix_b, w_out, W['ffn_norm2'][l], W['ffn2_wi'][l], W['ffn2_wo'][l],
                        W['ple_norm'][l], W['ple_wg'][l], p[l].reshape(m, -1), W['ple_wp'][l], W['final_norm'],
                        final=(l == depth - 1))
    return h.reshape(b, t, d), new_even, new_odd


def kernel(x_prompt, x_sample, p_prompt, p_sample, state_conv, state_mlstm_C, state_mlstm_n, state_mlstm_m, cache_diff_k, cache_diff_v, cache_sb_k, cache_sb_v, page_table, ffn_norm1, ffn1_wi, ffn1_wo, mix_norm, ffn_norm2, ffn2_wi, ffn2_wo, ple_norm, ple_wg, ple_wp, ev_w_in, ev_conv_w, ev_conv_b, ev_ln_g, ev_ln_b, ev_gate_b, ev_w_out, od_w_in, od_lambda, od_head_g, od_w_out, rel_bias, final_norm):
    bf = lambda a: a.astype(BF16)
    n_even, n_odd = ev_w_in.shape[0], od_w_in.shape[0]
    W = dict(ffn_norm1=ffn_norm1, ffn1_wi=bf(ffn1_wi), ffn1_wo=bf(ffn1_wo), mix_norm=mix_norm,
             ffn_norm2=ffn_norm2, ffn2_wi=bf(ffn2_wi), ffn2_wo=bf(ffn2_wo),
             ple_norm=ple_norm, ple_wg=bf(ple_wg), ple_wp=bf(ple_wp),
             ev_w_in=bf(ev_w_in[:, :, :6 * HALF]), ev_w_gt=bf(jnp.swapaxes(ev_w_in[:, :, 6 * HALF:], 1, 2)),
             ev_conv_w=ev_conv_w, ev_conv_b=ev_conv_b, ev_ln_g=ev_ln_g, ev_ln_b=ev_ln_b,
             ev_gate_b=ev_gate_b, ev_w_out=bf(ev_w_out),
             od_w_in=bf(od_w_in), od_w_kvt=bf(jnp.swapaxes(od_w_in[:, :, 4 * HALF:], 1, 2)), od_lambda=od_lambda, od_head_g=od_head_g, od_w_out=bf(od_w_out),
             rel_bias=rel_bias, final_norm=final_norm)
    bp, tp = x_prompt.shape[0], x_prompt.shape[1]
    bs, ts = x_sample.shape[0], x_sample.shape[1]
    even_p = [(jnp.zeros((bp, CONV_W - 1, HALF), F32), jnp.zeros((bp, H_B, DH_B, DH_B), F32),
               jnp.zeros((bp, H_B, DH_B), F32), jnp.zeros((bp, H_B), F32)) for _ in range(n_even)]
    y_prompt, ev_p, od_p = _trunk(x_prompt, p_prompt, even_p, None, None, W)
    even_s = [(state_conv[j], state_mlstm_C[j], state_mlstm_n[j], state_mlstm_m[j]) for j in range(n_even)]
    n_pool, psz = cache_diff_k.shape[1], cache_diff_k.shape[2]
    pool_d = lambda c: c.reshape(n_odd * n_pool, psz * H_C, 2 * DH_C)
    pool_s = lambda c: jnp.transpose(c, (0, 1, 3, 4, 2)).reshape(n_odd * n_pool, H_D * DH_D, psz)
    caches = (pool_d(cache_diff_k), pool_d(cache_diff_v), pool_s(cache_sb_k), pool_s(cache_sb_v))
    tables = [page_table + j * n_pool for j in range(n_odd)]
    y_sample, ev_s, od_s = _trunk(x_sample, p_sample, even_s, [caches] * n_odd, tables, W)
    ev = lambda states, i: jnp.stack([s[i] for s in states])
    return (y_prompt, y_sample,
            ev(ev_p, 0), ev(ev_s, 0), ev(ev_p, 1), ev(ev_s, 1),
            ev(ev_p, 2), ev(ev_s, 2), ev(ev_p, 3), ev(ev_s, 3),
            ev(od_p, 0), ev(od_s, 0), ev(od_p, 1), ev(od_s, 1),
            ev(od_p, 2), ev(od_s, 2), ev(od_p, 3), ev(od_s, 3))
```

```python
import functools
import math

import jax
import jax.numpy as jnp
from jax import lax
from jax.experimental import pallas as pl
from jax.experimental.pallas import tpu as pltpu

F32 = jnp.float32
BF16 = jnp.bfloat16

LANES = 128
SUBLANES = 8
VMEM_LIMIT_BYTES = 56 * 1024 * 1024

D_MODEL = 1024
D_FF = 2816
HALF = 512
CONV_W = 31
H_B, DH_B = 4, 128
H_C, DH_C = 4, 64
H_D, DH_D = 8, 64
N_BUCKETS = 32
MAX_DIST = 128
MLSTM_CHUNK = 128
NEG_INF = -1e30
SB_DEAD = -104.0

ROW_TILE = 512
FF_CHUNK = 256
DIFF_BLK = 512
SB_BLK = 256
CONV_ROWS = 256
CONV_SUB = 32
CONV_SEQS = 16
MLSTM_SEQS = 4
DEC_PAGES = 4
DEC_DIFF_PAGES = 8
DEC_SB_SEQS = 2
DEC_DIFF_SEQS = 2


def _cp(*sem):
    return pltpu.CompilerParams(dimension_semantics=sem, vmem_limit_bytes=VMEM_LIMIT_BYTES)


def _rms(x, g, eps=1e-6):
    return x * lax.rsqrt(jnp.mean(x * x, axis=-1, keepdims=True) + eps) * g


def _dot(a, b):
    return jnp.dot(a, b, preferred_element_type=F32)


def _dot_nt(a, b):
    return lax.dot_general(a, b, (((1,), (1,)), ((), ())), preferred_element_type=F32)


def _dot_tn(a, b):
    return lax.dot_general(a, b, (((0,), (0,)), ((), ())), preferred_element_type=F32)


def _log_sigmoid_neg(z):
    return -(jnp.maximum(z, 0.0) + jnp.log1p(jnp.exp(-jnp.abs(z))))


def _split_bf16(x):
    hi = x.astype(BF16)
    lo = (x - hi.astype(F32)).astype(BF16)
    return hi, lo


def _swiglu_half(x, g_ref, wi_ref, wo_ref, act_ref):
    hn = _rms(x, g_ref[...]).astype(BF16)
    for c in range(D_FF // FF_CHUNK):
        lo = c * FF_CHUNK
        gate = _dot(hn, wi_ref[:, lo:lo + FF_CHUNK])
        up = _dot(hn, wi_ref[:, D_FF + lo:D_FF + lo + FF_CHUNK])
        act_ref[:, lo:lo + FF_CHUNK] = (gate * jax.nn.sigmoid(gate) * up).astype(BF16)
    return x + 0.5 * _dot(act_ref[...], wo_ref[...])


def _ffn_kernel(x_ref, g_ref, wi_ref, wo_ref, o_ref, act_ref):
    o_ref[...] = _swiglu_half(x_ref[...], g_ref, wi_ref, wo_ref, act_ref)


def _ffn_half(h, g, wi, wo):
    m, d = h.shape
    tm = min(ROW_TILE, m)
    resident = lambda shape: pl.BlockSpec(shape, lambda i: (0, 0), pipeline_mode=pl.Buffered(1))
    return pl.pallas_call(
        _ffn_kernel,
        grid=(m // tm,),
        in_specs=[
            pl.BlockSpec((tm, d), lambda i: (i, 0)),
            pl.BlockSpec((1, d), lambda i: (0, 0)),
            resident((d, 2 * D_FF)),
            resident((D_FF, d)),
        ],
        out_specs=pl.BlockSpec((tm, d), lambda i: (i, 0)),
        out_shape=jax.ShapeDtypeStruct((m, d), F32),
        scratch_shapes=[pltpu.VMEM((tm, D_FF), BF16)],
        compiler_params=_cp("parallel"),
    )(h, g.reshape(1, d), wi, wo)


def _layer_tail_kernel(x_ref, a_ref, b_ref, wa_ref, wb_ref, g_ref, wi_ref, wo_ref, pg_ref, wg_ref, p_ref, wp_ref,
                       fg_ref, o_ref, act_ref, *, final):
    h = (x_ref[...] + _dot(a_ref[...].astype(BF16), wa_ref[...]) + _dot(b_ref[...].astype(BF16), wb_ref[...]))
    h = _swiglu_half(h, g_ref, wi_ref, wo_ref, act_ref)
    gate = jax.nn.sigmoid(_dot(_rms(h, pg_ref[...]).astype(BF16), wg_ref[...]))
    h = h + gate * _dot(p_ref[...].astype(BF16), wp_ref[...])
    if final:
        h = _rms(h, fg_ref[...])
    o_ref[...] = h


def _layer_tail(h, a, b, w_out, g, wi, wo, pg, wg, p, wp, fg, final):
    m, d = h.shape
    tm = min(ROW_TILE, m)
    pd = p.shape[1]
    row = lambda i: (i, 0)
    resident = lambda shape, idx=(0, 0): pl.BlockSpec(shape, lambda i: idx, pipeline_mode=pl.Buffered(1))
    vec = lambda: pl.BlockSpec((1, d), lambda i: (0, 0))
    return pl.pallas_call(
        functools.partial(_layer_tail_kernel, final=final),
        grid=(m // tm,),
        in_specs=[
            pl.BlockSpec((tm, d), row), pl.BlockSpec((tm, HALF), row), pl.BlockSpec((tm, HALF), row),
            resident((HALF, d)), resident((HALF, d), (1, 0)),
            vec(), resident((d, 2 * D_FF)), resident((D_FF, d)),
            vec(), resident((d, d)), pl.BlockSpec((tm, pd), row), resident((pd, d)), vec(),
        ],
        out_specs=pl.BlockSpec((tm, d), row),
        out_shape=jax.ShapeDtypeStruct((m, d), F32),
        scratch_shapes=[pltpu.VMEM((tm, D_FF), BF16)],
        compiler_params=_cp("parallel"),
    )(h, a, b, w_out, w_out, g.reshape(1, d), wi, wo, pg.reshape(1, d), wg, p, wp, fg.reshape(1, d))


def _inproj_even_kernel(x_ref, g_ref, w_ref, wgt_ref, u_ref, q_ref, k_ref, v_ref, o_ref, gt_ref):
    hn = _rms(x_ref[...], g_ref[...]).astype(BF16)

    def col(c):
        return _dot(hn, w_ref[:, c * HALF:(c + 1) * HALF])

    u_ref[...] = col(0) * jax.nn.sigmoid(col(1))
    q_ref[...] = col(2).astype(q_ref.dtype)
    k_ref[...] = (col(3) * (DH_B ** -0.5)).astype(k_ref.dtype)
    v_ref[...] = col(4).astype(v_ref.dtype)
    o_ref[...] = col(5)
    gt_ref[...] = _dot_nt(wgt_ref[...], hn)


def _inproj_even(h, g, w, wgt, qkv_dtype):
    m, d = h.shape
    tm = min(ROW_TILE, m)
    row = lambda i: (i, 0)
    out = lambda dt: jax.ShapeDtypeStruct((m, HALF), dt)
    return pl.pallas_call(
        _inproj_even_kernel,
        grid=(m // tm,),
        in_specs=[
            pl.BlockSpec((tm, d), row),
            pl.BlockSpec((1, d), lambda i: (0, 0)),
            pl.BlockSpec((d, 6 * HALF), lambda i: (0, 0)),
            pl.BlockSpec((2 * H_B, d), lambda i: (0, 0)),
        ],
        out_specs=[pl.BlockSpec((tm, HALF), row)] * 5 + [pl.BlockSpec((2 * H_B, tm), lambda i: (0, i))],
        out_shape=[out(F32), out(qkv_dtype), out(qkv_dtype), out(qkv_dtype), out(F32),
                   jax.ShapeDtypeStruct((2 * H_B, m), F32)],
        compiler_params=_cp("parallel"),
    )(h, g.reshape(1, d), w, wgt)


def _inproj_odd_kernel(x_ref, g_ref, w_ref, wkvt_ref, cq_ref, ck_ref, cv_ref, sq_ref, sk_ref, sv_ref, *, time_minor):
    hn = _rms(x_ref[...], g_ref[...]).astype(BF16)

    def col(c):
        return _dot(hn, w_ref[:, c * HALF:(c + 1) * HALF])

    cq_ref[...] = (col(0) * (DH_C ** -0.5)).astype(cq_ref.dtype)
    ck_ref[...] = col(1)
    cv_ref[...] = col(2)
    sq_ref[...] = (col(3) * (DH_D ** -0.5)).astype(sq_ref.dtype)
    if time_minor:
        sk_ref[0] = _dot_nt(wkvt_ref[0:HALF, :], hn)
        sv_ref[0] = _dot_nt(wkvt_ref[HALF:2 * HALF, :], hn)
    else:
        sk_ref[...] = col(4)
        sv_ref[...] = col(5)


def _inproj_odd(h, g, w, wkvt, q_dtype, b, t, time_minor):
    m, d = h.shape
    tm = min(ROW_TILE, m)
    row = lambda i: (i, 0)
    out = lambda dt: jax.ShapeDtypeStruct((m, HALF), dt)
    rows = pl.BlockSpec((tm, HALF), row)
    if time_minor:
        assert t % tm == 0
        nt = t // tm
        kv_spec = pl.BlockSpec((1, HALF, tm), lambda i: (i // nt, 0, i % nt))
        kv_shape = jax.ShapeDtypeStruct((b, HALF, t), F32)
    else:
        kv_spec, kv_shape = rows, out(F32)
    return pl.pallas_call(
        functools.partial(_inproj_odd_kernel, time_minor=time_minor),
        grid=(m // tm,),
        in_specs=[
            pl.BlockSpec((tm, d), row),
            pl.BlockSpec((1, d), lambda i: (0, 0)),
            pl.BlockSpec((d, 6 * HALF), lambda i: (0, 0)),
            pl.BlockSpec((2 * HALF, d), lambda i: (0, 0)),
        ],
        out_specs=[rows] * 4 + [kv_spec] * 2,
        out_shape=[out(q_dtype), out(F32), out(F32), out(q_dtype), kv_shape, kv_shape],
        compiler_params=_cp("parallel"),
    )(h, g.reshape(1, d), w, wkvt)


_HIST = 32


def _conv_kernel(u_ref, buf_ref, taps_ref, cb_ref, lg_ref, lb_ref, a_ref, st_ref, win_ref, sh_ref, *, tt, sub, bb):
    t = pl.program_id(1)
    pad = _HIST - (CONV_W - 1)
    span = tt + _HIST - SUBLANES

    for s in range(bb):
        @pl.when(t == 0)
        def _(s=s):
            win_ref[s, 0:SUBLANES, :] = jnp.zeros((SUBLANES, HALF), F32)
            win_ref[s, pad:_HIST, :] = buf_ref[s]

        win_ref[s, _HIST:_HIST + tt, :] = u_ref[s]
        for r in range(1, SUBLANES):
            sh_ref[s, r - 1] = win_ref[s, pl.ds(r, span), :]
        for rb in range(tt // sub):
            acc = jnp.zeros((sub, HALF), F32)
            for w in range(CONV_W):
                a8, r = divmod(pad + w, SUBLANES)
                lo = rb * sub + a8 * SUBLANES
                src = win_ref[s, lo:lo + sub, :] if r == 0 else sh_ref[s, r - 1, lo:lo + sub, :]
                acc = acc + src * taps_ref[w:w + 1, :]
            c = acc + cb_ref[...]
            mu = jnp.mean(c, axis=-1, keepdims=True)
            var = jnp.mean(jnp.square(c - mu), axis=-1, keepdims=True)
            cn = (c - mu) * lax.rsqrt(var + 1e-5) * lg_ref[...] + lb_ref[...]
            a_ref[s, rb * sub:(rb + 1) * sub, :] = (cn * jax.nn.sigmoid(cn)).astype(a_ref.dtype)

        @pl.when(t == pl.num_programs(1) - 1)
        def _(s=s):
            st_ref[s] = win_ref[s, tt + pad:tt + _HIST, :]

        win_ref[s, 0:_HIST, :] = win_ref[s, tt:tt + _HIST, :]


def _conv_module(u, buf, taps, cb, lg, lb, out_dtype):
    b, t, _ = u.shape
    tt = min(CONV_ROWS, t)
    sub = min(CONV_SUB, tt)
    bb = 1 if t > tt else math.gcd(b, CONV_SEQS)
    vec = lambda: pl.BlockSpec((1, HALF), lambda i, j: (0, 0))
    return pl.pallas_call(
        functools.partial(_conv_kernel, tt=tt, sub=sub, bb=bb),
        grid=(b // bb, t // tt),
        in_specs=[
            pl.BlockSpec((bb, tt, HALF), lambda i, j: (i, j, 0)),
            pl.BlockSpec((bb, CONV_W - 1, HALF), lambda i, j: (i, 0, 0)),
            pl.BlockSpec((CONV_W, HALF), lambda i, j: (0, 0)),
            vec(), vec(), vec(),
        ],
        out_specs=[
            pl.BlockSpec((bb, tt, HALF), lambda i, j: (i, j, 0)),
            pl.BlockSpec((bb, CONV_W - 1, HALF), lambda i, j: (i, 0, 0)),
        ],
        out_shape=[jax.ShapeDtypeStruct((b, t, HALF), out_dtype),
                   jax.ShapeDtypeStruct((b, CONV_W - 1, HALF), F32)],
        scratch_shapes=[pltpu.VMEM((bb, _HIST + tt, HALF), F32),
                        pltpu.VMEM((bb, SUBLANES - 1, tt + _HIST - SUBLANES, HALF), F32)],
        compiler_params=_cp("parallel", "arbitrary"),
    )(u, buf, taps, cb.reshape(1, HALF), lg.reshape(1, HALF), lb.reshape(1, HALF))


def _scan_lanes(x, op, fill):
    lane = lax.broadcasted_iota(jnp.int32, x.shape, 1)
    sh = 1
    while sh < x.shape[1]:
        x = op(x, jnp.where(lane >= sh, pltpu.roll(x, sh, 1), fill))
        sh *= 2
    return x


def _mlstm_kernel(q_ref, k_ref, v_ref, o_ref, g_ref, gb_ref, c0_ref, n0_ref, m0_ref,
                  h_ref, c_ref, n_ref, m_ref, cx_ref, *, lr, bb):
    L = MLSTM_CHUNK
    ng = 2 * H_B
    nr = bb * ng

    @pl.when(pl.program_id(1) == 0)
    def _():
        m_ref[...] = m0_ref[...]
        for s in range(bb):
            for h in range(H_B):
                cx_ref[s, h, :, 0:DH_B] = c0_ref[s, h]
                cx_ref[s, h, :, DH_B:2 * DH_B] = jnp.transpose(jnp.broadcast_to(n0_ref[s, h:h + 1, :], (DH_B, DH_B)))

    def rows(ref, s):
        x = ref[s]
        if lr < L:
            x = jnp.concatenate([x.astype(F32), jnp.zeros((L - lr, HALF), F32)], axis=0)
        return x.astype(BF16)

    g = g_ref[...].reshape(nr, L) + gb_ref[...]
    row = lax.broadcasted_iota(jnp.int32, g.shape, 0)
    is_li = row % ng < H_B
    bcum = _scan_lanes(jnp.where(is_li, 0.0, _log_sigmoid_neg(-g)), jnp.add, 0.0)
    b = pltpu.roll(bcum, nr - H_B, 0)
    a = jnp.where(is_li, g - b, 0.0)
    m_old = m_ref[...].reshape(nr, LANES)
    mx = jnp.maximum(m_old, _scan_lanes(a, jnp.maximum, -3e38))
    b_last = b[:, L - 1:L]
    m_new = b_last + mx[:, L - 1:L]
    w_inter = jnp.exp(m_old - mx)
    inv_floor = jnp.exp(-(b + mx))
    w_key = jnp.exp(a + b_last - m_new)
    decay = jnp.exp(b_last + m_old - m_new)
    m_ref[...] = jnp.where(is_li, jnp.broadcast_to(m_new, (nr, LANES)), 0.0).reshape(bb, ng, LANES)
    packed = jnp.concatenate([mx, w_inter, inv_floor, w_key] + [jnp.zeros((L - 4 * nr, L), F32)] * (4 * nr < L), axis=0)
    cols = jnp.transpose(packed)
    tpos = lax.broadcasted_iota(jnp.int32, (L, L), 0)
    spos = lax.broadcasted_iota(jnp.int32, (L, L), 1)
    causal = spos <= tpos
    ones = jnp.ones((L, DH_B), BF16)
    for s in range(bb):
        q, k, v = rows(q_ref, s), rows(k_ref, s), rows(v_ref, s)
        outs = []
        for h in range(H_B):
            sl = slice(h * DH_B, (h + 1) * DH_B)
            qh, kh = q[:, sl], k[:, sl]
            v1 = jnp.concatenate([v[:, sl], ones], axis=1)
            i = s * ng + h
            col = lambda vec: cols[:, vec * nr + i:vec * nr + i + 1]
            gate = jnp.where(causal, jnp.exp(a[i:i + 1, :] - col(0)), 0.0)
            sc = _dot_nt(qh, kh) * gate
            cx = cx_ref[s, h]
            mix = col(1) * _dot(qh, cx.astype(BF16)) + _dot(sc.astype(BF16), v1)
            outs.append(mix[:, 0:DH_B] / jnp.maximum(jnp.abs(mix[:, DH_B:2 * DH_B]), col(2)))
            kw = (kh.astype(F32) * col(3)).astype(BF16)
            cx_ref[s, h] = decay[i:i + 1, 0:1] * cx + _dot_tn(kw, v1)
        hs = jnp.concatenate(outs, axis=1)
        h_ref[s] = (jax.nn.sigmoid(o_ref[s]) * hs[0:lr]).astype(h_ref.dtype)

    @pl.when(pl.program_id(1) == pl.num_programs(1) - 1)
    def _():
        for s in range(bb):
            for h in range(H_B):
                c_ref[s, h] = cx_ref[s, h, :, 0:DH_B]
                n_ref[s, h:h + 1, :] = jnp.transpose(cx_ref[s, h, :, DH_B:2 * DH_B])[0:1, :]


def _mlstm(q, k, v, o, gt, gate_b, c0, n0, m0, out_dtype):
    b, t, _ = q.shape
    L = MLSTM_CHUNK
    lr = min(L, t)
    nc = t // lr
    m0b = jnp.broadcast_to(jnp.pad(m0, ((0, 0), (0, SUBLANES - H_B)))[:, :, None], (b, SUBLANES, LANES))
    bb = math.gcd(b, MLSTM_SEQS)
    gb = jnp.broadcast_to(jnp.tile(gate_b, bb).reshape(bb * 2 * H_B, 1), (bb * 2 * H_B, L))
    blk = pl.BlockSpec((bb, lr, HALF), lambda i, j: (i, j, 0))
    st = lambda *s: pl.BlockSpec((bb,) + s, lambda i, j: (i,) + (0,) * len(s))
    h, c1, n1, m1 = pl.pallas_call(
        functools.partial(_mlstm_kernel, lr=lr, bb=bb),
        grid=(b // bb, nc),
        in_specs=[blk, blk, blk, blk,
                  pl.BlockSpec((bb, 2 * H_B, L), lambda i, j: (i, 0, j)),
                  pl.BlockSpec((bb * 2 * H_B, L), lambda i, j: (0, 0)),
                  st(H_B, DH_B, DH_B), st(H_B, DH_B), st(SUBLANES, LANES)],
        out_specs=[blk, st(H_B, DH_B, DH_B), st(H_B, DH_B), st(SUBLANES, LANES)],
        out_shape=[jax.ShapeDtypeStruct((b, t, HALF), out_dtype),
                   jax.ShapeDtypeStruct((b, H_B, DH_B, DH_B), F32),
                   jax.ShapeDtypeStruct((b, H_B, DH_B), F32),
                   jax.ShapeDtypeStruct((b, SUBLANES, LANES), F32)],
        scratch_shapes=[pltpu.VMEM((bb, H_B, DH_B, 2 * DH_B), F32)],
        compiler_params=_cp("parallel", "arbitrary"),
    )(q, k, v, o, gt, gb, c0, n0, m0b)
    return h, c1, n1, m1[:, :H_B, 0]


def _t5_bucket(dist):
    n = jnp.maximum(dist, 0)
    exact = N_BUCKETS // 2
    nf = jnp.maximum(n, 1).astype(F32)
    large = exact + (jnp.log(nf / exact) / math.log(MAX_DIST / exact) * (N_BUCKETS - exact)).astype(jnp.int32)
    return jnp.where(n < exact, n, jnp.minimum(large, N_BUCKETS - 1))


def _bias_of_dist(rel_bias, dist):
    onehot = jax.nn.one_hot(_t5_bucket(dist), N_BUCKETS, dtype=F32)
    b = jnp.einsum('...k,kh->h...', onehot, rel_bias.astype(F32), precision=lax.Precision.HIGHEST)
    return jnp.where(dist >= 0, b, NEG_INF)


def _lambda(lp_ref, lam_init):
    lp = lp_ref[...]
    s1 = jnp.sum(lp[0:1] * lp[1:2], axis=-1, keepdims=True)
    s2 = jnp.sum(lp[2:3] * lp[3:4], axis=-1, keepdims=True)
    return jnp.exp(s1) - jnp.exp(s2) + lam_init


def _head_norm(x, hg, lam_init):
    return x * lax.rsqrt(jnp.mean(x * x, axis=-1, keepdims=True) + 1e-6) * hg * (1.0 - lam_init)


def _diff_attn_kernel(q_ref, k_ref, v_ref, bd_ref, bp_ref, lp_ref, hg_ref, o_ref, kb_ref, vt_ref, *, lam_init):
    blk = DIFF_BLK
    i = pl.program_id(2)

    @pl.when(i == 0)
    def _():
        kb_ref[...] = k_ref[...].astype(BF16)
        for c in range(vt_ref.shape[0]):
            vt_ref[c] = jnp.transpose(v_ref[c * blk:(c + 1) * blk, :]).astype(BF16)

    q = q_ref[...]
    lane = lax.broadcasted_iota(jnp.int32, q.shape, 1)
    zero = jnp.zeros_like(q)
    q2 = jnp.concatenate([jnp.where(lane < DH_C, q, zero), jnp.where(lane >= DH_C, q, zero)], axis=0)

    def scores(j):
        off = pl.multiple_of(j * blk, blk)
        return _dot_nt(kb_ref[pl.ds(off, blk), :], q2)

    def update(carry, s, vt):
        m, l, acc = carry
        mn = jnp.maximum(m, jnp.max(s, axis=0, keepdims=True))
        p = jnp.exp(s - mn)
        al = jnp.exp(m - mn)
        return mn, al * l + jnp.sum(p, axis=0, keepdims=True), al * acc + _dot(vt, p.astype(BF16))

    s = scores(i) + bd_ref[0]
    m = jnp.max(s, axis=0, keepdims=True)
    p = jnp.exp(s - m)
    carry = (m, jnp.sum(p, axis=0, keepdims=True), _dot(vt_ref[i], p.astype(BF16)))

    def prev_step(c):
        return update(c, scores(i - 1) + bp_ref[0], vt_ref[i - 1])

    carry = lax.cond(i >= 1, prev_step, lambda c: c, carry)

    def far_step(j, c):
        return update(c, scores(j), vt_ref[j])

    m, l, acc = lax.fori_loop(0, jnp.maximum(i - 1, 0), far_step, carry)
    o = acc / l
    o = jnp.transpose(o[:, 0:blk] - _lambda(lp_ref, lam_init) * o[:, blk:2 * blk])
    o_ref[...] = _head_norm(o, hg_ref[0], lam_init).astype(o_ref.dtype)


def _diff_attention(cq, ck, cv, b, t, rel_bias, lam_p, head_g, lam_init):
    blk = DIFF_BLK
    nq = t // blk
    r = jnp.arange(blk, dtype=jnp.int32)
    d0 = r[None, :] - r[:, None]
    assert blk + 1 >= MAX_DIST
    far = _bias_of_dist(rel_bias, jnp.full((1, 1), 2 * blk, jnp.int32))
    rel = lambda d: jnp.tile(jnp.where(d >= 0, _bias_of_dist(rel_bias, d) - far, NEG_INF), (1, 1, 2))
    bd = rel(d0)
    bp = rel(d0 + blk)
    per_head = lambda *s: pl.BlockSpec((1,) + s, lambda bi, h, i: (h, 0, 0))
    return pl.pallas_call(
        functools.partial(_diff_attn_kernel, lam_init=lam_init),
        grid=(b, H_C, nq),
        in_specs=[
            pl.BlockSpec((blk, LANES), lambda bi, h, i: (bi * nq + i, h)),
            pl.BlockSpec((t, LANES), lambda bi, h, i: (bi, h)),
            pl.BlockSpec((t, LANES), lambda bi, h, i: (bi, h)),
            per_head(blk, 2 * blk), per_head(blk, 2 * blk),
            pl.BlockSpec((4, DH_C), lambda bi, h, i: (0, 0)),
            per_head(1, LANES),
        ],
        out_specs=pl.BlockSpec((blk, LANES), lambda bi, h, i: (bi * nq + i, h)),
        out_shape=jax.ShapeDtypeStruct((b * t, HALF), BF16),
        scratch_shapes=[pltpu.VMEM((t, LANES), BF16), pltpu.VMEM((nq, LANES, blk), BF16)],
        compiler_params=_cp("parallel", "parallel", "arbitrary"),
    )(cq, ck, cv, bd, bp, lam_p, head_g.reshape(H_C, 1, LANES))


def _sb_tile(q2, kt, vt, upper, r, mask):
    z = _dot(q2, kt)
    lk = _log_sigmoid_neg(z)
    if mask is not None:
        lk = jnp.where(mask, lk, 0.0)
    hi, lo = _split_bf16(lk)
    after = _dot(hi, upper) + _dot(lo, upper) + r
    w = jnp.exp(lk + z + after)
    if mask is not None:
        w = jnp.where(mask, w, 0.0)
    return _dot_nt(w.astype(BF16), vt), r + jnp.sum(lk, axis=-1, keepdims=True)


def _strict_upper(n):
    j = lax.broadcasted_iota(jnp.int32, (n, n), 0)
    s = lax.broadcasted_iota(jnp.int32, (n, n), 1)
    return jnp.where(j > s, 1.0, 0.0).astype(BF16)


def _sb_attn_kernel(q_ref, k_ref, v_ref, o_ref, kb_ref, vb_ref):
    blk = SB_BLK
    i = pl.program_id(2)

    @pl.when(i == 0)
    def _():
        for c in range(kb_ref.shape[0]):
            kb_ref[c] = k_ref[0, :, c * blk:(c + 1) * blk].astype(BF16)
            vb_ref[c] = v_ref[0, :, c * blk:(c + 1) * blk].astype(BF16)

    q = q_ref[...]
    lane = lax.broadcasted_iota(jnp.int32, q.shape, 1)
    zero = jnp.zeros_like(q)
    q2 = jnp.concatenate([jnp.where(lane < DH_D, q, zero), jnp.where(lane >= DH_D, q, zero)], axis=0)
    upper = _strict_upper(blk)
    tpos = lax.broadcasted_iota(jnp.int32, (2 * blk, blk), 0) % blk
    spos = lax.broadcasted_iota(jnp.int32, (2 * blk, blk), 1)

    def tile(j, r, mask):
        return _sb_tile(q2, kb_ref[j], vb_ref[j], upper, r, mask)

    acc, r = tile(i, jnp.zeros((2 * blk, 1), F32), spos < tpos)
    pv, r = tile(jnp.maximum(i - 1, 0), r, jnp.broadcast_to(i >= 1, spos.shape))
    acc = acc + pv

    def cond(c):
        j, _, r = c
        return jnp.logical_and(j >= 0, jnp.max(r) > SB_DEAD)

    def body(c):
        j, acc, r = c
        pv, r = tile(j, r, None)
        return j - 1, acc + pv, r

    _, acc, _ = lax.while_loop(cond, body, (i - 2, acc, r))
    o_ref[...] = jnp.where(lane < DH_D, acc[0:blk], acc[blk:2 * blk]).astype(o_ref.dtype)


def _sb_attention(sq, sk, sv, b, t):
    blk = SB_BLK
    nq = t // blk
    return pl.pallas_call(
        _sb_attn_kernel,
        grid=(b, HALF // LANES, nq),
        in_specs=[
            pl.BlockSpec((blk, LANES), lambda bi, h, i: (bi * nq + i, h)),
            pl.BlockSpec((1, LANES, t), lambda bi, h, i: (bi, h, 0)),
            pl.BlockSpec((1, LANES, t), lambda bi, h, i: (bi, h, 0)),
        ],
        out_specs=pl.BlockSpec((blk, LANES), lambda bi, h, i: (bi * nq + i, h)),
        out_shape=jax.ShapeDtypeStruct((b * t, HALF), BF16),
        scratch_shapes=[pltpu.VMEM((nq, LANES, blk), BF16), pltpu.VMEM((nq, LANES, blk), BF16)],
        compiler_params=_cp("parallel", "parallel", "arbitrary"),
    )(sq, sk, sv)


_DEC_ROWS = 64


def _pad_rows(x, n):
    return jnp.concatenate([x, jnp.zeros((n - x.shape[0], x.shape[1]), F32)], axis=0).astype(BF16)


def _dec_diff_kernel(pt_ref, cq_ref, ckf_ref, cvf_ref, bias_ref, bnew_ref, lp_ref, hg_ref, dk_hbm, dv_hbm,
                     oc_ref, qa_ref, m_ref, l_ref, acc_ref, kbuf_ref, vbuf_ref, sem_ref,
                     *, ts, lam_init, npg, n_pages, nsq):
    g = pl.program_id(1)
    n_groups = pl.num_programs(1)
    step = pl.program_id(0) * n_groups + g
    slot = lax.rem(step, 2)

    def page_copies(st, sl):
        blk, grp = lax.div(st, n_groups), lax.rem(st, n_groups)
        out = []
        for q in range(nsq):
            for p in range(npg):
                page = pt_ref[(blk * nsq + q) * n_pages + n_pages - 1 - (grp * npg + p)]
                i = q * npg + p
                out.append(pltpu.make_async_copy(dk_hbm.at[page], kbuf_ref.at[sl, i], sem_ref.at[sl, 2 * i]))
                out.append(pltpu.make_async_copy(dv_hbm.at[page], vbuf_ref.at[sl, i], sem_ref.at[sl, 2 * i + 1]))
        return out

    def start(st, sl):
        for i, c in enumerate(page_copies(st, sl)):
            c.start(priority=i % 2)

    @pl.when(step == 0)
    def _():
        start(step, slot)

    @pl.when(step + 1 < pl.num_programs(0) * n_groups)
    def _():
        start(step + 1, 1 - slot)

    def update(q, tiles):
        m = m_ref[q]
        mn = m
        for s, _ in tiles:
            mn = jnp.maximum(mn, jnp.max(s, axis=-1, keepdims=True))
        al = jnp.exp(m - mn)
        l = al * l_ref[q]
        acc = al * acc_ref[q]
        for s, v in tiles:
            p = jnp.exp(s - mn)
            l = l + jnp.sum(p, axis=-1, keepdims=True)
            acc = acc + _dot(p.astype(BF16), v)
        m_ref[q] = mn
        l_ref[q] = l
        acc_ref[q] = acc

    @pl.when(g == 0)
    def _():
        half = lax.broadcasted_iota(jnp.int32, (ts, LANES), 1) >= DH_C
        m_ref[...] = jnp.full_like(m_ref, NEG_INF)
        l_ref[...] = jnp.zeros_like(l_ref)
        acc_ref[...] = jnp.zeros_like(acc_ref)
        for q in range(nsq):
            cq = cq_ref[q]
            qa_ref[q] = jnp.concatenate(
                [jnp.where(half if c % 2 else jnp.logical_not(half), cq[:, (c // 2) * LANES:(c // 2 + 1) * LANES], 0.0)
                 for c in range(2 * H_C)], axis=0).astype(BF16)
            update(q, [(_dot_nt(qa_ref[q], _pad_rows(ckf_ref[q], LANES)) + bnew_ref[...],
                        _pad_rows(cvf_ref[q], LANES))])

    for c in page_copies(step, slot):
        c.wait()
    for q in range(nsq):
        qa = qa_ref[q]
        update(q, [(_dot_nt(qa, kbuf_ref[slot, q * npg + p].astype(BF16)) + bias_ref[g * npg + p],
                    vbuf_ref[slot, q * npg + p].astype(BF16)) for p in range(npg)])

    @pl.when(g == pl.num_programs(1) - 1)
    def _():
        lam = _lambda(lp_ref, lam_init)
        for q in range(nsq):
            o = acc_ref[q] / l_ref[q]
            for h in range(H_C):
                r0 = h * 2 * ts
                oh = o[r0:r0 + ts] - lam * o[r0 + ts:r0 + 2 * ts]
                oc_ref[q, :, h * LANES:(h + 1) * LANES] = _head_norm(oh, hg_ref[h:h + 1, :], lam_init)


def _sb_queries(sq):
    lane = lax.broadcasted_iota(jnp.int32, sq.shape, 1)
    return jnp.concatenate(
        [jnp.where((lane >= c * DH_D) & (lane < (c + 1) * DH_D), sq, 0.0) for c in range(H_D)], axis=0).astype(BF16)


def _sb_fold(tiles, upper, r, acc):
    for z, pv, mask in tiles:
        lk = _log_sigmoid_neg(z)
        if mask is not None:
            lk = jnp.where(mask, lk, 0.0)
        hi, lo = _split_bf16(lk)
        w = jnp.exp(lk + z + _dot(hi, upper) + _dot(lo, upper) + r)
        if mask is not None:
            w = jnp.where(mask, w, 0.0)
        acc = acc + pv(w.astype(BF16))
        r = r + jnp.sum(lk, axis=-1, keepdims=True)
    return r, acc


def _sb_page_tiles(qs, pages):
    return [(_dot(qs, pages[2 * p][0].astype(BF16)), functools.partial(_dot_nt, b=pages[2 * p + 1][0].astype(BF16)),
             None) for p in range(len(pages) // 2)]


def _sb_heads_to_lanes(acc, ts):
    lane = lax.broadcasted_iota(jnp.int32, (ts, LANES), 1)
    outs = []
    for pr in range(H_D // 2):
        sl = slice(pr * LANES, (pr + 1) * LANES)
        r0 = pr * 2 * ts
        outs.append(jnp.where(lane < DH_D, acc[r0:r0 + ts, sl], acc[r0 + ts:r0 + 2 * ts, sl]))
    return jnp.concatenate(outs, axis=1)


def _dec_sb_first_kernel(pt_ref, sq_ref, sk_ref, sv_ref, *rest, ts, npg, bb):
    pages = rest[:2 * npg * bb]
    os_ref, acc_ref, r_ref, alive_ref = rest[2 * npg * bb:]
    psz = pages[0].shape[2]
    upper = _strict_upper(psz)
    tq = lax.broadcasted_iota(jnp.int32, (_DEC_ROWS, psz), 0) % ts
    kpos = lax.broadcasted_iota(jnp.int32, (_DEC_ROWS, psz), 1)
    for s in range(bb):
        qs = _sb_queries(sq_ref[s])
        sv_new = _pad_rows(sv_ref[s], psz)
        tiles = [(_dot_nt(qs, _pad_rows(sk_ref[s], psz)), lambda w, sv_new=sv_new: _dot(w, sv_new), kpos < tq)]
        r, acc = _sb_fold(tiles + _sb_page_tiles(qs, pages[2 * npg * s:2 * npg * (s + 1)]), upper,
                          jnp.zeros((_DEC_ROWS, 1), F32), jnp.zeros((_DEC_ROWS, HALF), F32))
        os_ref[s] = _sb_heads_to_lanes(acc, ts)
        acc_ref[s] = acc
        r_ref[s] = jnp.broadcast_to(r, (_DEC_ROWS, LANES))
        alive = jnp.max(r, axis=0, keepdims=True) > SB_DEAD
        alive_ref[s] = jnp.broadcast_to(jnp.where(alive, 1, 0), (SUBLANES, LANES)).astype(jnp.int32)


def _dec_sb_rest_kernel(pt_ref, al_ref, sq_ref, acc_in_ref, r_in_ref, *rest, ts, npg, nrest):
    pages = rest[:2 * nrest]
    os_ref, acc_ref, r_ref = rest[2 * nrest:]
    acc_ref[...] = acc_in_ref[0]
    r_ref[...] = r_in_ref[0][:, 0:1]

    @pl.when(al_ref[pl.program_id(0)] == 1)
    def _():
        qs = _sb_queries(sq_ref[0])
        upper = _strict_upper(pages[0].shape[2])
        for grp in range(nrest // npg):
            @pl.when(jnp.max(r_ref[...]) > SB_DEAD)
            def _(grp=grp):
                tiles = _sb_page_tiles(qs, pages[2 * npg * grp:2 * npg * (grp + 1)])
                r, acc = _sb_fold(tiles, upper, r_ref[...], acc_ref[...])
                r_ref[...] = r
                acc_ref[...] = acc

    os_ref[0] = _sb_heads_to_lanes(acc_ref[...], ts)


def _decode_attention(cq, sq, new_rows, caches, page_table, rel_bias, lam_p, head_g, lam_init):
    b, ts, _ = cq.shape
    n_pages = page_table.shape[1]
    psz = caches[2].shape[2]
    past = n_pages * psz
    npd = math.gcd(n_pages, DEC_DIFF_PAGES)
    nps = math.gcd(n_pages, DEC_PAGES)
    nrest = n_pages - nps
    nkn = LANES // H_C
    assert _DEC_ROWS == 2 * H_C * ts == H_D * ts and ts <= nkn
    ck, cv, sk, sv = new_rows
    ckf = ck.reshape(b, ts * H_C, LANES)
    cvf = cv.reshape(b, ts * H_C, LANES)
    dk, dv, skt, svt = caches
    pt = page_table.reshape(-1)

    def table(base):
        base = jnp.moveaxis(base, 0, -3)
        own = jnp.arange(H_C)[:, None, None, None] == jnp.arange(H_C)[None, None, None, :]
        tab = jnp.where(own, base[..., None], NEG_INF)
        tab = jnp.broadcast_to(tab[..., :, None, :, :, :], tab.shape[:-3] + (2,) + tab.shape[-3:])
        return tab.reshape(tab.shape[:-5] + (_DEC_ROWS, tab.shape[-2] * H_C))

    tq = jnp.arange(ts, dtype=jnp.int32)
    kpos = (jnp.arange(n_pages - 1, -1, -1, dtype=jnp.int32)[:, None] * psz
            + jnp.arange(psz, dtype=jnp.int32)[None, :])
    dist = past + tq[None, :, None] - kpos[:, None, :]
    bias = table(_bias_of_dist(rel_bias, dist))
    knew = jnp.arange(nkn, dtype=jnp.int32)
    dnew = jnp.where(knew[None, :] < ts, tq[:, None] - knew[None, :], -1)
    bnew = table(_bias_of_dist(rel_bias, dnew))

    page = lambda idx: pl.BlockSpec((1, HALF, LANES), idx)
    const = lambda shape: pl.BlockSpec(shape, lambda *_: (0,) * len(shape))

    nsq = math.gcd(b, DEC_DIFF_SEQS)
    row = pl.BlockSpec((nsq, ts, HALF), lambda bi, g, pt: (bi, 0, 0))
    rowf = pl.BlockSpec((nsq, ts * H_C, LANES), lambda bi, g, pt: (bi, 0, 0))
    hbm = pl.BlockSpec(memory_space=pl.ANY)
    pages = pltpu.VMEM((2, nsq * npd, psz * H_C, LANES), F32)
    oc = pl.pallas_call(
        functools.partial(_dec_diff_kernel, ts=ts, lam_init=lam_init, npg=npd, n_pages=n_pages, nsq=nsq),
        grid_spec=pltpu.PrefetchScalarGridSpec(
            num_scalar_prefetch=1,
            grid=(b // nsq, n_pages // npd),
            in_specs=[row, rowf, rowf, const((n_pages, _DEC_ROWS, psz * H_C)), const((_DEC_ROWS, LANES)),
                      const((4, DH_C)), const((H_C, LANES)), hbm, hbm],
            out_specs=row,
            scratch_shapes=[pltpu.VMEM((nsq, _DEC_ROWS, LANES), BF16), pltpu.VMEM((nsq, _DEC_ROWS, 1), F32),
                            pltpu.VMEM((nsq, _DEC_ROWS, 1), F32), pltpu.VMEM((nsq, _DEC_ROWS, LANES), F32),
                            pages, pages, pltpu.SemaphoreType.DMA((2, 2 * nsq * npd))],
        ),
        out_shape=jax.ShapeDtypeStruct((b, ts, HALF), F32),
        compiler_params=_cp("arbitrary", "arbitrary"),
    )(pt, cq, ckf, cvf, bias, bnew, lam_p, head_g.reshape(H_C, LANES), dk, dv)

    bb = math.gcd(b, DEC_SB_SEQS)
    rows = lambda n, *s: pl.BlockSpec((n,) + s, lambda bi, *_: (bi,) + (0,) * len(s))
    sspecs, sargs = [], []
    for s in range(bb):
        for p in range(nps):
            for c in (skt, svt):
                sspecs.append(page(lambda bi, pt, s=s, p=p: (pt[(bi * bb + s) * n_pages + n_pages - 1 - p], 0, 0)))
                sargs.append(c)
    os_first, acc, r, alive = pl.pallas_call(
        functools.partial(_dec_sb_first_kernel, ts=ts, npg=nps, bb=bb),
        grid_spec=pltpu.PrefetchScalarGridSpec(
            num_scalar_prefetch=1,
            grid=(b // bb,),
            in_specs=[rows(bb, ts, HALF)] * 3 + sspecs,
            out_specs=[rows(bb, ts, HALF), rows(bb, _DEC_ROWS, HALF), rows(bb, _DEC_ROWS, LANES),
                       rows(bb, SUBLANES, LANES)],
        ),
        out_shape=[jax.ShapeDtypeStruct((b, ts, HALF), F32), jax.ShapeDtypeStruct((b, _DEC_ROWS, HALF), F32),
                   jax.ShapeDtypeStruct((b, _DEC_ROWS, LANES), F32),
                   jax.ShapeDtypeStruct((b, SUBLANES, LANES), jnp.int32)],
        compiler_params=_cp("parallel"),
    )(pt, sq, sk, sv, *sargs)
    if nrest == 0:
        return oc, os_first

    alive = alive[:, 0, 0]

    def rest_pages():
        rspecs, rargs = [], []
        for p in range(nrest):
            for c in (skt, svt):
                rspecs.append(page(lambda bi, pt, al, p=p: (
                    jnp.where(al[bi] == 1, pt[bi * n_pages + n_pages - 1 - nps - p], pt[0]), 0, 0)))
                rargs.append(c)
        return pl.pallas_call(
            functools.partial(_dec_sb_rest_kernel, ts=ts, npg=nps, nrest=nrest),
            grid_spec=pltpu.PrefetchScalarGridSpec(
                num_scalar_prefetch=2,
                grid=(b,),
                in_specs=[rows(1, ts, HALF), rows(1, _DEC_ROWS, HALF), rows(1, _DEC_ROWS, LANES)] + rspecs,
                out_specs=rows(1, ts, HALF),
                scratch_shapes=[pltpu.VMEM((_DEC_ROWS, HALF), F32), pltpu.VMEM((_DEC_ROWS, 1), F32)],
            ),
            out_shape=jax.ShapeDtypeStruct((b, ts, HALF), F32),
            compiler_params=_cp("arbitrary"),
        )(pt, alive, sq, acc, r, *rargs)

    os_ = lax.cond(jnp.any(alive == 1), rest_pages, lambda: os_first)
    return oc, os_


def _trunk(x, p, even_states, odd_past, page_table, W):
    b, t, d = x.shape
    m = b * t
    prompt = odd_past is None
    act = BF16 if prompt else F32
    h = x.reshape(m, d)
    depth = p.shape[0]
    new_even, new_odd = [], []
    for l in range(depth):
        j = l // 2
        h = _ffn_half(h, W['ffn_norm1'][l], W['ffn1_wi'][l], W['ffn1_wo'][l])
        if l % 2 == 0:
            buf, c0, n0, m0 = even_states[j]
            u, q, k, v, o, gt = _inproj_even(h, W['mix_norm'][l], W['ev_w_in'][j], W['ev_w_gt'][j], act)
            a_out, buf1 = _conv_module(u.reshape(b, t, HALF), buf, W['ev_conv_w'][j], W['ev_conv_b'][j],
                                       W['ev_ln_g'][j], W['ev_ln_b'][j], act)
            gt = jnp.moveaxis(gt.reshape(2 * H_B, b, t), 1, 0)
            if t < MLSTM_CHUNK:
                padv = jnp.where(jnp.arange(2 * H_B) < H_B, NEG_INF, -NEG_INF).astype(F32)
                gt = jnp.concatenate(
                    [gt, jnp.broadcast_to(padv[None, :, None], (b, 2 * H_B, MLSTM_CHUNK - t))], axis=2)
            r3 = lambda a: a.reshape(b, t, HALF)
            b_out, c1, n1, m1 = _mlstm(r3(q), r3(k), r3(v), r3(o), gt, W['ev_gate_b'][j], c0, n0, m0, act)
            new_even.append((buf1, c1, n1, m1))
            mix_a, mix_b, w_out = a_out.reshape(m, HALF), b_out.reshape(m, HALF), W['ev_w_out'][j]
        else:
            lam_init = 0.8 - 0.6 * math.exp(-0.3 * l)
            cq, ck, cv, sq, sk, sv = _inproj_odd(h, W['mix_norm'][l], W['od_w_in'][j], W['od_w_kvt'][j], act, b, t,
                                                 time_minor=prompt)
            heads = lambda a, nh: a.reshape(b, t, nh, HALF // nh)
            if prompt:
                sb_rows = lambda a: jnp.transpose(a.reshape(b, H_D, DH_D, t), (0, 3, 1, 2))
            else:
                sb_rows = lambda a: heads(a, H_D)
            new_odd.append((heads(ck, H_C), heads(cv, H_C), sb_rows(sk), sb_rows(sv)))
            if prompt:
                oc = _diff_attention(cq, ck, cv, b, t, W['rel_bias'], W['od_lambda'][j], W['od_head_g'][j], lam_init)
                os_ = _sb_attention(sq, sk, sv, b, t)
            else:
                r3 = lambda a: a.reshape(b, t, HALF)
                oc, os_ = _decode_attention(r3(cq), r3(sq), [r3(a) for a in (ck, cv, sk, sv)], odd_past[j],
                                            page_table[j], W['rel_bias'], W['od_lambda'][j], W['od_head_g'][j],
                                            lam_init)
            mix_a, mix_b, w_out = oc.reshape(m, HALF), os_.reshape(m, HALF), W['od_w_out'][j]
        h = _layer_tail(h, mix_a, mix_b, w_out, W['ffn_norm2'][l], W['ffn2_wi'][l], W['ffn2_wo'][l],
                        W['ple_norm'][l], W['ple_wg'][l], p[l].reshape(m, -1), W['ple_wp'][l], W['final_norm'],
                        final=(l == depth - 1))
    return h.reshape(b, t, d), new_even, new_odd


def kernel(x_prompt, x_sample, p_prompt, p_sample, state_conv, state_mlstm_C, state_mlstm_n, state_mlstm_m, cache_diff_k, cache_diff_v, cache_sb_k, cache_sb_v, page_table, ffn_norm1, ffn1_wi, ffn1_wo, mix_norm, ffn_norm2, ffn2_wi, ffn2_wo, ple_norm, ple_wg, ple_wp, ev_w_in, ev_conv_w, ev_conv_b, ev_ln_g, ev_ln_b, ev_gate_b, ev_w_out, od_w_in, od_lambda, od_head_g, od_w_out, rel_bias, final_norm):
    bf = lambda a: a.astype(BF16)
    n_even, n_odd = ev_w_in.shape[0], od_w_in.shape[0]
    W = dict(ffn_norm1=ffn_norm1, ffn1_wi=bf(ffn1_wi), ffn1_wo=bf(ffn1_wo), mix_norm=mix_norm,
             ffn_norm2=ffn_norm2, ffn2_wi=bf(ffn2_wi), ffn2_wo=bf(ffn2_wo),
             ple_norm=ple_norm, ple_wg=bf(ple_wg), ple_wp=bf(ple_wp),
             ev_w_in=bf(ev_w_in[:, :, :6 * HALF]), ev_w_gt=bf(jnp.swapaxes(ev_w_in[:, :, 6 * HALF:], 1, 2)),
             ev_conv_w=ev_conv_w, ev_conv_b=ev_conv_b, ev_ln_g=ev_ln_g, ev_ln_b=ev_ln_b,
             ev_gate_b=ev_gate_b, ev_w_out=bf(ev_w_out),
             od_w_in=bf(od_w_in), od_w_kvt=bf(jnp.swapaxes(od_w_in[:, :, 4 * HALF:], 1, 2)), od_lambda=od_lambda, od_head_g=od_head_g, od_w_out=bf(od_w_out),
             rel_bias=rel_bias, final_norm=final_norm)
    bp, tp = x_prompt.shape[0], x_prompt.shape[1]
    bs, ts = x_sample.shape[0], x_sample.shape[1]
    even_p = [(jnp.zeros((bp, CONV_W - 1, HALF), F32), jnp.zeros((bp, H_B, DH_B, DH_B), F32),
               jnp.zeros((bp, H_B, DH_B), F32), jnp.zeros((bp, H_B), F32)) for _ in range(n_even)]
    y_prompt, ev_p, od_p = _trunk(x_prompt, p_prompt, even_p, None, None, W)
    even_s = [(state_conv[j], state_mlstm_C[j], state_mlstm_n[j], state_mlstm_m[j]) for j in range(n_even)]
    n_pool, psz = cache_diff_k.shape[1], cache_diff_k.shape[2]
    pool_d = lambda c: c.reshape(n_odd * n_pool, psz * H_C, 2 * DH_C)
    pool_s = lambda c: jnp.transpose(c, (0, 1, 3, 4, 2)).reshape(n_odd * n_pool, H_D * DH_D, psz)
    caches = (pool_d(cache_diff_k), pool_d(cache_diff_v), pool_s(cache_sb_k), pool_s(cache_sb_v))
    tables = [page_table + j * n_pool for j in range(n_odd)]
    y_sample, ev_s, od_s = _trunk(x_sample, p_sample, even_s, [caches] * n_odd, tables, W)
    ev = lambda states, i: jnp.stack([s[i] for s in states])
    return (y_prompt, y_sample,
            ev(ev_p, 0), ev(ev_s, 0), ev(ev_p, 1), ev(ev_s, 1),
            ev(ev_p, 2), ev(ev_s, 2), ev(ev_p, 3), ev(ev_s, 3),
            ev(od_p, 0), ev(od_s, 0), ev(od_p, 1), ev(od_s, 1),
            ev(od_p, 2), ev(od_s, 2), ev(od_p, 3), ev(od_s, 3))
```

```python
import functools
import math

import jax
import jax.numpy as jnp
from jax import lax
from jax.experimental import pallas as pl
from jax.experimental.pallas import tpu as pltpu

F32 = jnp.float32
BF16 = jnp.bfloat16

LANES = 128
SUBLANES = 8
VMEM_LIMIT_BYTES = 56 * 1024 * 1024

D_MODEL = 1024
D_FF = 2816
HALF = 512
CONV_W = 31
H_B, DH_B = 4, 128
H_C, DH_C = 4, 64
H_D, DH_D = 8, 64
N_BUCKETS = 32
MAX_DIST = 128
MLSTM_CHUNK = 128
NEG_INF = -1e30
SB_DEAD = -104.0

ROW_TILE = 512
FF_CHUNK = 256
DIFF_BLK = 512
SB_BLK = 256
CONV_ROWS = 256
CONV_SUB = 32
CONV_SEQS = 16
MLSTM_SEQS = 4
DEC_PAGES = 4
DEC_DIFF_PAGES = 8
DEC_SB_SEQS = 2
DEC_DIFF_SEQS = 2


def _cp(*sem):
    return pltpu.CompilerParams(dimension_semantics=sem, vmem_limit_bytes=VMEM_LIMIT_BYTES)


def _rms(x, g, eps=1e-6):
    return x * lax.rsqrt(jnp.mean(x * x, axis=-1, keepdims=True) + eps) * g


def _dot(a, b):
    return jnp.dot(a, b, preferred_element_type=F32)


def _dot_nt(a, b):
    return lax.dot_general(a, b, (((1,), (1,)), ((), ())), preferred_element_type=F32)


def _dot_tn(a, b):
    return lax.dot_general(a, b, (((0,), (0,)), ((), ())), preferred_element_type=F32)


def _log_sigmoid_neg(z):
    return -(jnp.maximum(z, 0.0) + jnp.log1p(jnp.exp(-jnp.abs(z))))


def _split_bf16(x):
    hi = x.astype(BF16)
    lo = (x - hi.astype(F32)).astype(BF16)
    return hi, lo


def _swiglu_half(x, g_ref, wi_ref, wo_ref, act_ref):
    hn = _rms(x, g_ref[...]).astype(BF16)
    for c in range(D_FF // FF_CHUNK):
        lo = c * FF_CHUNK
        gate = _dot(hn, wi_ref[:, lo:lo + FF_CHUNK])
        up = _dot(hn, wi_ref[:, D_FF + lo:D_FF + lo + FF_CHUNK])
        act_ref[:, lo:lo + FF_CHUNK] = (gate * jax.nn.sigmoid(gate) * up).astype(BF16)
    return x + 0.5 * _dot(act_ref[...], wo_ref[...])


def _ffn_kernel(x_ref, g_ref, wi_ref, wo_ref, o_ref, act_ref):
    o_ref[...] = _swiglu_half(x_ref[...], g_ref, wi_ref, wo_ref, act_ref)


def _ffn_half(h, g, wi, wo):
    m, d = h.shape
    tm = min(ROW_TILE, m)
    resident = lambda shape: pl.BlockSpec(shape, lambda i: (0, 0), pipeline_mode=pl.Buffered(1))
    return pl.pallas_call(
        _ffn_kernel,
        grid=(m // tm,),
        in_specs=[
            pl.BlockSpec((tm, d), lambda i: (i, 0)),
            pl.BlockSpec((1, d), lambda i: (0, 0)),
            resident((d, 2 * D_FF)),
            resident((D_FF, d)),
        ],
        out_specs=pl.BlockSpec((tm, d), lambda i: (i, 0)),
        out_shape=jax.ShapeDtypeStruct((m, d), F32),
        scratch_shapes=[pltpu.VMEM((tm, D_FF), BF16)],
        compiler_params=_cp("parallel"),
    )(h, g.reshape(1, d), wi, wo)


def _layer_tail_kernel(x_ref, a_ref, b_ref, wa_ref, wb_ref, g_ref, wi_ref, wo_ref, pg_ref, wg_ref, p_ref, wp_ref,
                       fg_ref, o_ref, act_ref, *, final):
    h = (x_ref[...] + _dot(a_ref[...].astype(BF16), wa_ref[...]) + _dot(b_ref[...].astype(BF16), wb_ref[...]))
    h = _swiglu_half(h, g_ref, wi_ref, wo_ref, act_ref)
    gate = jax.nn.sigmoid(_dot(_rms(h, pg_ref[...]).astype(BF16), wg_ref[...]))
    h = h + gate * _dot(p_ref[...].astype(BF16), wp_ref[...])
    if final:
        h = _rms(h, fg_ref[...])
    o_ref[...] = h


def _layer_tail(h, a, b, w_out, g, wi, wo, pg, wg, p, wp, fg, final):
    m, d = h.shape
    tm = min(ROW_TILE, m)
    pd = p.shape[1]
    row = lambda i: (i, 0)
    resident = lambda shape, idx=(0, 0): pl.BlockSpec(shape, lambda i: idx, pipeline_mode=pl.Buffered(1))
    vec = lambda: pl.BlockSpec((1, d), lambda i: (0, 0))
    return pl.pallas_call(
        functools.partial(_layer_tail_kernel, final=final),
        grid=(m // tm,),
        in_specs=[
            pl.BlockSpec((tm, d), row), pl.BlockSpec((tm, HALF), row), pl.BlockSpec((tm, HALF), row),
            resident((HALF, d)), resident((HALF, d), (1, 0)),
            vec(), resident((d, 2 * D_FF)), resident((D_FF, d)),
            vec(), resident((d, d)), pl.BlockSpec((tm, pd), row), resident((pd, d)), vec(),
        ],
        out_specs=pl.BlockSpec((tm, d), row),
        out_shape=jax.ShapeDtypeStruct((m, d), F32),
        scratch_shapes=[pltpu.VMEM((tm, D_FF), BF16)],
        compiler_params=_cp("parallel"),
    )(h, a, b, w_out, w_out, g.reshape(1, d), wi, wo, pg.reshape(1, d), wg, p, wp, fg.reshape(1, d))


def _inproj_even_kernel(x_ref, g_ref, w_ref, wgt_ref, u_ref, q_ref, k_ref, v_ref, o_ref, gt_ref):
    hn = _rms(x_ref[...], g_ref[...]).astype(BF16)

    def col(c):
        return _dot(hn, w_ref[:, c * HALF:(c + 1) * HALF])

    u_ref[...] = col(0) * jax.nn.sigmoid(col(1))
    q_ref[...] = col(2).astype(q_ref.dtype)
    k_ref[...] = (col(3) * (DH_B ** -0.5)).astype(k_ref.dtype)
    v_ref[...] = col(4).astype(v_ref.dtype)
    o_ref[...] = col(5)
    gt_ref[...] = _dot_nt(wgt_ref[...], hn)


def _inproj_even(h, g, w, wgt, qkv_dtype):
    m, d = h.shape
    tm = min(ROW_TILE, m)
    row = lambda i: (i, 0)
    out = lambda dt: jax.ShapeDtypeStruct((m, HALF), dt)
    return pl.pallas_call(
        _inproj_even_kernel,
        grid=(m // tm,),
        in_specs=[
            pl.BlockSpec((tm, d), row),
            pl.BlockSpec((1, d), lambda i: (0, 0)),
            pl.BlockSpec((d, 6 * HALF), lambda i: (0, 0)),
            pl.BlockSpec((2 * H_B, d), lambda i: (0, 0)),
        ],
        out_specs=[pl.BlockSpec((tm, HALF), row)] * 5 + [pl.BlockSpec((2 * H_B, tm), lambda i: (0, i))],
        out_shape=[out(F32), out(qkv_dtype), out(qkv_dtype), out(qkv_dtype), out(F32),
                   jax.ShapeDtypeStruct((2 * H_B, m), F32)],
        compiler_params=_cp("parallel"),
    )(h, g.reshape(1, d), w, wgt)


def _inproj_odd_kernel(x_ref, g_ref, w_ref, wkvt_ref, cq_ref, ck_ref, cv_ref, sq_ref, sk_ref, sv_ref, *, time_minor):
    hn = _rms(x_ref[...], g_ref[...]).astype(BF16)

    def col(c):
        return _dot(hn, w_ref[:, c * HALF:(c + 1) * HALF])

    cq_ref[...] = (col(0) * (DH_C ** -0.5)).astype(cq_ref.dtype)
    ck_ref[...] = col(1)
    cv_ref[...] = col(2)
    sq_ref[...] = (col(3) * (DH_D ** -0.5)).astype(sq_ref.dtype)
    if time_minor:
        sk_ref[0] = _dot_nt(wkvt_ref[0:HALF, :], hn)
        sv_ref[0] = _dot_nt(wkvt_ref[HALF:2 * HALF, :], hn)
    else:
        sk_ref[...] = col(4)
        sv_ref[...] = col(5)


def _inproj_odd(h, g, w, wkvt, q_dtype, b, t, time_minor):
    m, d = h.shape
    tm = min(ROW_TILE, m)
    row = lambda i: (i, 0)
    out = lambda dt: jax.ShapeDtypeStruct((m, HALF), dt)
    rows = pl.BlockSpec((tm, HALF), row)
    if time_minor:
        assert t % tm == 0
        nt = t // tm
        kv_spec = pl.BlockSpec((1, HALF, tm), lambda i: (i // nt, 0, i % nt))
        kv_shape = jax.ShapeDtypeStruct((b, HALF, t), F32)
    else:
        kv_spec, kv_shape = rows, out(F32)
    return pl.pallas_call(
        functools.partial(_inproj_odd_kernel, time_minor=time_minor),
        grid=(m // tm,),
        in_specs=[
            pl.BlockSpec((tm, d), row),
            pl.BlockSpec((1, d), lambda i: (0, 0)),
            pl.BlockSpec((d, 6 * HALF), lambda i: (0, 0)),
            pl.BlockSpec((2 * HALF, d), lambda i: (0, 0)),
        ],
        out_specs=[rows] * 4 + [kv_spec] * 2,
        out_shape=[out(q_dtype), out(F32), out(F32), out(q_dtype), kv_shape, kv_shape],
        compiler_params=_cp("parallel"),
    )(h, g.reshape(1, d), w, wkvt)


_HIST = 32


def _conv_kernel(u_ref, buf_ref, taps_ref, cb_ref, lg_ref, lb_ref, a_ref, st_ref, win_ref, sh_ref, *, tt, sub, bb):
    t = pl.program_id(1)
    pad = _HIST - (CONV_W - 1)
    span = tt + _HIST - SUBLANES

    for s in range(bb):
        @pl.when(t == 0)
        def _(s=s):
            win_ref[s, 0:SUBLANES, :] = jnp.zeros((SUBLANES, HALF), F32)
            win_ref[s, pad:_HIST, :] = buf_ref[s]

        win_ref[s, _HIST:_HIST + tt, :] = u_ref[s]
        for r in range(1, SUBLANES):
            sh_ref[s, r - 1] = win_ref[s, pl.ds(r, span), :]
        for rb in range(tt // sub):
            acc = jnp.zeros((sub, HALF), F32)
            for w in range(CONV_W):
                a8, r = divmod(pad + w, SUBLANES)
                lo = rb * sub + a8 * SUBLANES
                src = win_ref[s, lo:lo + sub, :] if r == 0 else sh_ref[s, r - 1, lo:lo + sub, :]
                acc = acc + src * taps_ref[w:w + 1, :]
            c = acc + cb_ref[...]
            mu = jnp.mean(c, axis=-1, keepdims=True)
            var = jnp.mean(jnp.square(c - mu), axis=-1, keepdims=True)
            cn = (c - mu) * lax.rsqrt(var + 1e-5) * lg_ref[...] + lb_ref[...]
            a_ref[s, rb * sub:(rb + 1) * sub, :] = (cn * jax.nn.sigmoid(cn)).astype(a_ref.dtype)

        @pl.when(t == pl.num_programs(1) - 1)
        def _(s=s):
            st_ref[s] = win_ref[s, tt + pad:tt + _HIST, :]

        win_ref[s, 0:_HIST, :] = win_ref[s, tt:tt + _HIST, :]


def _conv_module(u, buf, taps, cb, lg, lb, out_dtype):
    b, t, _ = u.shape
    tt = min(CONV_ROWS, t)
    sub = min(CONV_SUB, tt)
    bb = 1 if t > tt else math.gcd(b, CONV_SEQS)
    vec = lambda: pl.BlockSpec((1, HALF), lambda i, j: (0, 0))
    return pl.pallas_call(
        functools.partial(_conv_kernel, tt=tt, sub=sub, bb=bb),
        grid=(b // bb, t // tt),
        in_specs=[
            pl.BlockSpec((bb, tt, HALF), lambda i, j: (i, j, 0)),
            pl.BlockSpec((bb, CONV_W - 1, HALF), lambda i, j: (i, 0, 0)),
            pl.BlockSpec((CONV_W, HALF), lambda i, j: (0, 0)),
            vec(), vec(), vec(),
        ],
        out_specs=[
            pl.BlockSpec((bb, tt, HALF), lambda i, j: (i, j, 0)),
            pl.BlockSpec((bb, CONV_W - 1, HALF), lambda i, j: (i, 0, 0)),
        ],
        out_shape=[jax.ShapeDtypeStruct((b, t, HALF), out_dtype),
                   jax.ShapeDtypeStruct((b, CONV_W - 1, HALF), F32)],
        scratch_shapes=[pltpu.VMEM((bb, _HIST + tt, HALF), F32),
                        pltpu.VMEM((bb, SUBLANES - 1, tt + _HIST - SUBLANES, HALF), F32)],
        compiler_params=_cp("parallel", "arbitrary"),
    )(u, buf, taps, cb.reshape(1, HALF), lg.reshape(1, HALF), lb.reshape(1, HALF))


def _scan_lanes(x, op, fill):
    lane = lax.broadcasted_iota(jnp.int32, x.shape, 1)
    sh = 1
    while sh < x.shape[1]:
        x = op(x, jnp.where(lane >= sh, pltpu.roll(x, sh, 1), fill))
        sh *= 2
    return x


def _mlstm_kernel(q_ref, k_ref, v_ref, o_ref, g_ref, gb_ref, c0_ref, n0_ref, m0_ref,
                  h_ref, c_ref, n_ref, m_ref, cx_ref, *, lr, bb):
    L = MLSTM_CHUNK
    ng = 2 * H_B
    nr = bb * ng

    @pl.when(pl.program_id(1) == 0)
    def _():
        m_ref[...] = m0_ref[...]
        for s in range(bb):
            for h in range(H_B):
                cx_ref[s, h, :, 0:DH_B] = c0_ref[s, h]
                cx_ref[s, h, :, DH_B:2 * DH_B] = jnp.transpose(jnp.broadcast_to(n0_ref[s, h:h + 1, :], (DH_B, DH_B)))

    def rows(ref, s):
        x = ref[s]
        if lr < L:
            x = jnp.concatenate([x.astype(F32), jnp.zeros((L - lr, HALF), F32)], axis=0)
        return x.astype(BF16)

    g = g_ref[...].reshape(nr, L) + gb_ref[...]
    row = lax.broadcasted_iota(jnp.int32, g.shape, 0)
    is_li = row % ng < H_B
    bcum = _scan_lanes(jnp.where(is_li, 0.0, _log_sigmoid_neg(-g)), jnp.add, 0.0)
    b = pltpu.roll(bcum, nr - H_B, 0)
    a = jnp.where(is_li, g - b, 0.0)
    m_old = m_ref[...].reshape(nr, LANES)
    mx = jnp.maximum(m_old, _scan_lanes(a, jnp.maximum, -3e38))
    b_last = b[:, L - 1:L]
    m_new = b_last + mx[:, L - 1:L]
    w_inter = jnp.exp(m_old - mx)
    inv_floor = jnp.exp(-(b + mx))
    w_key = jnp.exp(a + b_last - m_new)
    decay = jnp.exp(b_last + m_old - m_new)
    m_ref[...] = jnp.where(is_li, jnp.broadcast_to(m_new, (nr, LANES)), 0.0).reshape(bb, ng, LANES)
    packed = jnp.concatenate([mx, w_inter, inv_floor, w_key] + [jnp.zeros((L - 4 * nr, L), F32)] * (4 * nr < L), axis=0)
    cols = jnp.transpose(packed)
    tpos = lax.broadcasted_iota(jnp.int32, (L, L), 0)
    spos = lax.broadcasted_iota(jnp.int32, (L, L), 1)
    causal = spos <= tpos
    ones = jnp.ones((L, DH_B), BF16)
    for s in range(bb):
        q, k, v = rows(q_ref, s), rows(k_ref, s), rows(v_ref, s)
        outs = []
        for h in range(H_B):
            sl = slice(h * DH_B, (h + 1) * DH_B)
            qh, kh = q[:, sl], k[:, sl]
            v1 = jnp.concatenate([v[:, sl], ones], axis=1)
            i = s * ng + h
            col = lambda vec: cols[:, vec * nr + i:vec * nr + i + 1]
            gate = jnp.where(causal, jnp.exp(a[i:i + 1, :] - col(0)), 0.0)
            sc = _dot_nt(qh, kh) * gate
            cx = cx_ref[s, h]
            mix = col(1) * _dot(qh, cx.astype(BF16)) + _dot(sc.astype(BF16), v1)
            outs.append(mix[:, 0:DH_B] / jnp.maximum(jnp.abs(mix[:, DH_B:2 * DH_B]), col(2)))
            kw = (kh.astype(F32) * col(3)).astype(BF16)
            cx_ref[s, h] = decay[i:i + 1, 0:1] * cx + _dot_tn(kw, v1)
        hs = jnp.concatenate(outs, axis=1)
        h_ref[s] = (jax.nn.sigmoid(o_ref[s]) * hs[0:lr]).astype(h_ref.dtype)

    @pl.when(pl.program_id(1) == pl.num_programs(1) - 1)
    def _():
        for s in range(bb):
            for h in range(H_B):
                c_ref[s, h] = cx_ref[s, h, :, 0:DH_B]
                n_ref[s, h:h + 1, :] = jnp.transpose(cx_ref[s, h, :, DH_B:2 * DH_B])[0:1, :]


def _mlstm(q, k, v, o, gt, gate_b, c0, n0, m0, out_dtype):
    b, t, _ = q.shape
    L = MLSTM_CHUNK
    lr = min(L, t)
    nc = t // lr
    m0b = jnp.broadcast_to(jnp.pad(m0, ((0, 0), (0, SUBLANES - H_B)))[:, :, None], (b, SUBLANES, LANES))
    bb = math.gcd(b, MLSTM_SEQS)
    gb = jnp.broadcast_to(jnp.tile(gate_b, bb).reshape(bb * 2 * H_B, 1), (bb * 2 * H_B, L))
    blk = pl.BlockSpec((bb, lr, HALF), lambda i, j: (i, j, 0))
    st = lambda *s: pl.BlockSpec((bb,) + s, lambda i, j: (i,) + (0,) * len(s))
    h, c1, n1, m1 = pl.pallas_call(
        functools.partial(_mlstm_kernel, lr=lr, bb=bb),
        grid=(b // bb, nc),
        in_specs=[blk, blk, blk, blk,
                  pl.BlockSpec((bb, 2 * H_B, L), lambda i, j: (i, 0, j)),
                  pl.BlockSpec((bb * 2 * H_B, L), lambda i, j: (0, 0)),
                  st(H_B, DH_B, DH_B), st(H_B, DH_B), st(SUBLANES, LANES)],
        out_specs=[blk, st(H_B, DH_B, DH_B), st(H_B, DH_B), st(SUBLANES, LANES)],
        out_shape=[jax.ShapeDtypeStruct((b, t, HALF), out_dtype),
                   jax.ShapeDtypeStruct((b, H_B, DH_B, DH_B), F32),
                   jax.ShapeDtypeStruct((b, H_B, DH_B), F32),
                   jax.ShapeDtypeStruct((b, SUBLANES, LANES), F32)],
        scratch_shapes=[pltpu.VMEM((bb, H_B, DH_B, 2 * DH_B), F32)],
        compiler_params=_cp("parallel", "arbitrary"),
    )(q, k, v, o, gt, gb, c0, n0, m0b)
    return h, c1, n1, m1[:, :H_B, 0]


def _t5_bucket(dist):
    n = jnp.maximum(dist, 0)
    exact = N_BUCKETS // 2
    nf = jnp.maximum(n, 1).astype(F32)
    large = exact + (jnp.log(nf / exact) / math.log(MAX_DIST / exact) * (N_BUCKETS - exact)).astype(jnp.int32)
    return jnp.where(n < exact, n, jnp.minimum(large, N_BUCKETS - 1))


def _bias_of_dist(rel_bias, dist):
    onehot = jax.nn.one_hot(_t5_bucket(dist), N_BUCKETS, dtype=F32)
    b = jnp.einsum('...k,kh->h...', onehot, rel_bias.astype(F32), precision=lax.Precision.HIGHEST)
    return jnp.where(dist >= 0, b, NEG_INF)


def _lambda(lp_ref, lam_init):
    lp = lp_ref[...]
    s1 = jnp.sum(lp[0:1] * lp[1:2], axis=-1, keepdims=True)
    s2 = jnp.sum(lp[2:3] * lp[3:4], axis=-1, keepdims=True)
    return jnp.exp(s1) - jnp.exp(s2) + lam_init


def _head_norm(x, hg, lam_init):
    return x * lax.rsqrt(jnp.mean(x * x, axis=-1, keepdims=True) + 1e-6) * hg * (1.0 - lam_init)


def _diff_attn_kernel(q_ref, k_ref, v_ref, bd_ref, bp_ref, lp_ref, hg_ref, o_ref, kb_ref, vt_ref, *, lam_init):
    blk = DIFF_BLK
    i = pl.program_id(2)

    @pl.when(i == 0)
    def _():
        kb_ref[...] = k_ref[...].astype(BF16)
        for c in range(vt_ref.shape[0]):
            vt_ref[c] = jnp.transpose(v_ref[c * blk:(c + 1) * blk, :]).astype(BF16)

    q = q_ref[...]
    lane = lax.broadcasted_iota(jnp.int32, q.shape, 1)
    zero = jnp.zeros_like(q)
    q2 = jnp.concatenate([jnp.where(lane < DH_C, q, zero), jnp.where(lane >= DH_C, q, zero)], axis=0)

    def scores(j):
        off = pl.multiple_of(j * blk, blk)
        return _dot_nt(kb_ref[pl.ds(off, blk), :], q2)

    def update(carry, s, vt):
        m, l, acc = carry
        mn = jnp.maximum(m, jnp.max(s, axis=0, keepdims=True))
        p = jnp.exp(s - mn)
        al = jnp.exp(m - mn)
        return mn, al * l + jnp.sum(p, axis=0, keepdims=True), al * acc + _dot(vt, p.astype(BF16))

    s = scores(i) + bd_ref[0]
    m = jnp.max(s, axis=0, keepdims=True)
    p = jnp.exp(s - m)
    carry = (m, jnp.sum(p, axis=0, keepdims=True), _dot(vt_ref[i], p.astype(BF16)))

    def prev_step(c):
        return update(c, scores(i - 1) + bp_ref[0], vt_ref[i - 1])

    carry = lax.cond(i >= 1, prev_step, lambda c: c, carry)

    n_far = jnp.maximum(i - 1, 0)
    odd = lax.rem(n_far, 2)
    carry = lax.cond(odd == 1, lambda c: update(c, scores(0), vt_ref[0]), lambda c: c, carry)

    def far_pair(jj, c):
        m, l, acc = c
        j = odd + 2 * jj
        s0, s1 = scores(j), scores(j + 1)
        mn = jnp.maximum(m, jnp.maximum(jnp.max(s0, axis=0, keepdims=True), jnp.max(s1, axis=0, keepdims=True)))
        p0, p1 = jnp.exp(s0 - mn), jnp.exp(s1 - mn)
        al = jnp.exp(m - mn)
        l = al * l + jnp.sum(p0, axis=0, keepdims=True) + jnp.sum(p1, axis=0, keepdims=True)
        acc = al * acc + _dot(vt_ref[j], p0.astype(BF16)) + _dot(vt_ref[j + 1], p1.astype(BF16))
        return mn, l, acc

    m, l, acc = lax.fori_loop(0, n_far // 2, far_pair, carry)
    o = acc / l
    o = jnp.transpose(o[:, 0:blk] - _lambda(lp_ref, lam_init) * o[:, blk:2 * blk])
    o_ref[...] = _head_norm(o, hg_ref[0], lam_init).astype(o_ref.dtype)


def _diff_attention(cq, ck, cv, b, t, rel_bias, lam_p, head_g, lam_init):
    blk = DIFF_BLK
    nq = t // blk
    r = jnp.arange(blk, dtype=jnp.int32)
    d0 = r[None, :] - r[:, None]
    assert blk + 1 >= MAX_DIST
    far = _bias_of_dist(rel_bias, jnp.full((1, 1), 2 * blk, jnp.int32))
    rel = lambda d: jnp.tile(jnp.where(d >= 0, _bias_of_dist(rel_bias, d) - far, NEG_INF), (1, 1, 2))
    bd = rel(d0)
    bp = rel(d0 + blk)
    per_head = lambda *s: pl.BlockSpec((1,) + s, lambda bi, h, i: (h, 0, 0))
    return pl.pallas_call(
        functools.partial(_diff_attn_kernel, lam_init=lam_init),
        grid=(b, H_C, nq),
        in_specs=[
            pl.BlockSpec((blk, LANES), lambda bi, h, i: (bi * nq + i, h)),
            pl.BlockSpec((t, LANES), lambda bi, h, i: (bi, h)),
            pl.BlockSpec((t, LANES), lambda bi, h, i: (bi, h)),
            per_head(blk, 2 * blk), per_head(blk, 2 * blk),
            pl.BlockSpec((4, DH_C), lambda bi, h, i: (0, 0)),
            per_head(1, LANES),
        ],
        out_specs=pl.BlockSpec((blk, LANES), lambda bi, h, i: (bi * nq + i, h)),
        out_shape=jax.ShapeDtypeStruct((b * t, HALF), BF16),
        scratch_shapes=[pltpu.VMEM((t, LANES), BF16), pltpu.VMEM((nq, LANES, blk), BF16)],
        compiler_params=_cp("parallel", "parallel", "arbitrary"),
    )(cq, ck, cv, bd, bp, lam_p, head_g.reshape(H_C, 1, LANES))


def _sb_tile(q2, kt, vt, upper, r, mask):
    z = _dot(q2, kt)
    lk = _log_sigmoid_neg(z)
    if mask is not None:
        lk = jnp.where(mask, lk, 0.0)
    hi, lo = _split_bf16(lk)
    after = _dot(hi, upper) + _dot(lo, upper) + r
    w = jnp.exp(lk + z + after)
    if mask is not None:
        w = jnp.where(mask, w, 0.0)
    return _dot_nt(w.astype(BF16), vt), r + jnp.sum(lk, axis=-1, keepdims=True)


def _strict_upper(n):
    j = lax.broadcasted_iota(jnp.int32, (n, n), 0)
    s = lax.broadcasted_iota(jnp.int32, (n, n), 1)
    return jnp.where(j > s, 1.0, 0.0).astype(BF16)


def _sb_attn_kernel(q_ref, k_ref, v_ref, o_ref, kb_ref, vb_ref):
    blk = SB_BLK
    i = pl.program_id(2)

    @pl.when(i == 0)
    def _():
        for c in range(kb_ref.shape[0]):
            kb_ref[c] = k_ref[0, :, c * blk:(c + 1) * blk].astype(BF16)
            vb_ref[c] = v_ref[0, :, c * blk:(c + 1) * blk].astype(BF16)

    q = q_ref[...]
    lane = lax.broadcasted_iota(jnp.int32, q.shape, 1)
    zero = jnp.zeros_like(q)
    q2 = jnp.concatenate([jnp.where(lane < DH_D, q, zero), jnp.where(lane >= DH_D, q, zero)], axis=0)
    upper = _strict_upper(blk)
    tpos = lax.broadcasted_iota(jnp.int32, (2 * blk, blk), 0) % blk
    spos = lax.broadcasted_iota(jnp.int32, (2 * blk, blk), 1)

    def tile(j, r, mask):
        return _sb_tile(q2, kb_ref[j], vb_ref[j], upper, r, mask)

    acc, r = tile(i, jnp.zeros((2 * blk, 1), F32), spos < tpos)
    pv, r = tile(jnp.maximum(i - 1, 0), r, jnp.broadcast_to(i >= 1, spos.shape))
    acc = acc + pv

    def cond(c):
        j, _, r = c
        return jnp.logical_and(j >= 0, jnp.max(r) > SB_DEAD)

    def body(c):
        j, acc, r = c
        pv, r = tile(j, r, None)
        return j - 1, acc + pv, r

    _, acc, _ = lax.while_loop(cond, body, (i - 2, acc, r))
    o_ref[...] = jnp.where(lane < DH_D, acc[0:blk], acc[blk:2 * blk]).astype(o_ref.dtype)


def _sb_attention(sq, sk, sv, b, t):
    blk = SB_BLK
    nq = t // blk
    return pl.pallas_call(
        _sb_attn_kernel,
        grid=(b, HALF // LANES, nq),
        in_specs=[
            pl.BlockSpec((blk, LANES), lambda bi, h, i: (bi * nq + i, h)),
            pl.BlockSpec((1, LANES, t), lambda bi, h, i: (bi, h, 0)),
            pl.BlockSpec((1, LANES, t), lambda bi, h, i: (bi, h, 0)),
        ],
        out_specs=pl.BlockSpec((blk, LANES), lambda bi, h, i: (bi * nq + i, h)),
        out_shape=jax.ShapeDtypeStruct((b * t, HALF), BF16),
        scratch_shapes=[pltpu.VMEM((nq, LANES, blk), BF16), pltpu.VMEM((nq, LANES, blk), BF16)],
        compiler_params=_cp("parallel", "parallel", "arbitrary"),
    )(sq, sk, sv)


_DEC_ROWS = 64


def _pad_rows(x, n):
    return jnp.concatenate([x, jnp.zeros((n - x.shape[0], x.shape[1]), F32)], axis=0).astype(BF16)


def _dec_diff_kernel(pt_ref, cq_ref, ckf_ref, cvf_ref, bias_ref, bnew_ref, lp_ref, hg_ref, dk_hbm, dv_hbm,
                     oc_ref, qa_ref, m_ref, l_ref, acc_ref, kbuf_ref, vbuf_ref, sem_ref,
                     *, ts, lam_init, npg, n_pages, nsq):
    g = pl.program_id(1)
    n_groups = pl.num_programs(1)
    step = pl.program_id(0) * n_groups + g
    slot = lax.rem(step, 2)

    def page_copies(st, sl):
        blk, grp = lax.div(st, n_groups), lax.rem(st, n_groups)
        out = []
        for q in range(nsq):
            for p in range(npg):
                page = pt_ref[(blk * nsq + q) * n_pages + n_pages - 1 - (grp * npg + p)]
                i = q * npg + p
                out.append(pltpu.make_async_copy(dk_hbm.at[page], kbuf_ref.at[sl, i], sem_ref.at[sl, 2 * i]))
                out.append(pltpu.make_async_copy(dv_hbm.at[page], vbuf_ref.at[sl, i], sem_ref.at[sl, 2 * i + 1]))
        return out

    def start(st, sl):
        for i, c in enumerate(page_copies(st, sl)):
            c.start(priority=i % 2)

    @pl.when(step == 0)
    def _():
        start(step, slot)

    @pl.when(step + 1 < pl.num_programs(0) * n_groups)
    def _():
        start(step + 1, 1 - slot)

    def update(q, tiles):
        m = m_ref[q]
        mn = m
        for s, _ in tiles:
            mn = jnp.maximum(mn, jnp.max(s, axis=-1, keepdims=True))
        al = jnp.exp(m - mn)
        l = al * l_ref[q]
        acc = al * acc_ref[q]
        for s, v in tiles:
            p = jnp.exp(s - mn)
            l = l + jnp.sum(p, axis=-1, keepdims=True)
            acc = acc + _dot(p.astype(BF16), v)
        m_ref[q] = mn
        l_ref[q] = l
        acc_ref[q] = acc

    @pl.when(g == 0)
    def _():
        half = lax.broadcasted_iota(jnp.int32, (ts, LANES), 1) >= DH_C
        m_ref[...] = jnp.full_like(m_ref, NEG_INF)
        l_ref[...] = jnp.zeros_like(l_ref)
        acc_ref[...] = jnp.zeros_like(acc_ref)
        for q in range(nsq):
            cq = cq_ref[q]
            qa_ref[q] = jnp.concatenate(
                [jnp.where(half if c % 2 else jnp.logical_not(half), cq[:, (c // 2) * LANES:(c // 2 + 1) * LANES], 0.0)
                 for c in range(2 * H_C)], axis=0).astype(BF16)
            update(q, [(_dot_nt(qa_ref[q], _pad_rows(ckf_ref[q], LANES)) + bnew_ref[...],
                        _pad_rows(cvf_ref[q], LANES))])

    for c in page_copies(step, slot):
        c.wait()
    for q in range(nsq):
        qa = qa_ref[q]
        update(q, [(_dot_nt(qa, kbuf_ref[slot, q * npg + p].astype(BF16)) + bias_ref[g * npg + p],
                    vbuf_ref[slot, q * npg + p].astype(BF16)) for p in range(npg)])

    @pl.when(g == pl.num_programs(1) - 1)
    def _():
        lam = _lambda(lp_ref, lam_init)
        for q in range(nsq):
            o = acc_ref[q] / l_ref[q]
            for h in range(H_C):
                r0 = h * 2 * ts
                oh = o[r0:r0 + ts] - lam * o[r0 + ts:r0 + 2 * ts]
                oc_ref[q, :, h * LANES:(h + 1) * LANES] = _head_norm(oh, hg_ref[h:h + 1, :], lam_init)


def _sb_queries(sq):
    lane = lax.broadcasted_iota(jnp.int32, sq.shape, 1)
    return jnp.concatenate(
        [jnp.where((lane >= c * DH_D) & (lane < (c + 1) * DH_D), sq, 0.0) for c in range(H_D)], axis=0).astype(BF16)


def _sb_fold(tiles, upper, r, acc):
    for z, pv, mask in tiles:
        lk = _log_sigmoid_neg(z)
        if mask is not None:
            lk = jnp.where(mask, lk, 0.0)
        hi, lo = _split_bf16(lk)
        w = jnp.exp(lk + z + _dot(hi, upper) + _dot(lo, upper) + r)
        if mask is not None:
            w = jnp.where(mask, w, 0.0)
        acc = acc + pv(w.astype(BF16))
        r = r + jnp.sum(lk, axis=-1, keepdims=True)
    return r, acc


def _sb_page_tiles(qs, pages):
    return [(_dot(qs, pages[2 * p][0].astype(BF16)), functools.partial(_dot_nt, b=pages[2 * p + 1][0].astype(BF16)),
             None) for p in range(len(pages) // 2)]


def _sb_heads_to_lanes(acc, ts):
    lane = lax.broadcasted_iota(jnp.int32, (ts, LANES), 1)
    outs = []
    for pr in range(H_D // 2):
        sl = slice(pr * LANES, (pr + 1) * LANES)
        r0 = pr * 2 * ts
        outs.append(jnp.where(lane < DH_D, acc[r0:r0 + ts, sl], acc[r0 + ts:r0 + 2 * ts, sl]))
    return jnp.concatenate(outs, axis=1)


def _dec_sb_first_kernel(pt_ref, sq_ref, sk_ref, sv_ref, *rest, ts, npg, bb):
    pages = rest[:2 * npg * bb]
    os_ref, acc_ref, r_ref, alive_ref = rest[2 * npg * bb:]
    psz = pages[0].shape[2]
    upper = _strict_upper(psz)
    tq = lax.broadcasted_iota(jnp.int32, (_DEC_ROWS, psz), 0) % ts
    kpos = lax.broadcasted_iota(jnp.int32, (_DEC_ROWS, psz), 1)
    for s in range(bb):
        qs = _sb_queries(sq_ref[s])
        sv_new = _pad_rows(sv_ref[s], psz)
        tiles = [(_dot_nt(qs, _pad_rows(sk_ref[s], psz)), lambda w, sv_new=sv_new: _dot(w, sv_new), kpos < tq)]
        r, acc = _sb_fold(tiles + _sb_page_tiles(qs, pages[2 * npg * s:2 * npg * (s + 1)]), upper,
                          jnp.zeros((_DEC_ROWS, 1), F32), jnp.zeros((_DEC_ROWS, HALF), F32))
        os_ref[s] = _sb_heads_to_lanes(acc, ts)
        acc_ref[s] = acc
        r_ref[s] = jnp.broadcast_to(r, (_DEC_ROWS, LANES))
        alive = jnp.max(r, axis=0, keepdims=True) > SB_DEAD
        alive_ref[s] = jnp.broadcast_to(jnp.where(alive, 1, 0), (SUBLANES, LANES)).astype(jnp.int32)


def _dec_sb_rest_kernel(pt_ref, al_ref, sq_ref, acc_in_ref, r_in_ref, *rest, ts, npg, nrest):
    pages = rest[:2 * nrest]
    os_ref, acc_ref, r_ref = rest[2 * nrest:]
    acc_ref[...] = acc_in_ref[0]
    r_ref[...] = r_in_ref[0][:, 0:1]

    @pl.when(al_ref[pl.program_id(0)] == 1)
    def _():
        qs = _sb_queries(sq_ref[0])
        upper = _strict_upper(pages[0].shape[2])
        for grp in range(nrest // npg):
            @pl.when(jnp.max(r_ref[...]) > SB_DEAD)
            def _(grp=grp):
                tiles = _sb_page_tiles(qs, pages[2 * npg * grp:2 * npg * (grp + 1)])
                r, acc = _sb_fold(tiles, upper, r_ref[...], acc_ref[...])
                r_ref[...] = r
                acc_ref[...] = acc

    os_ref[0] = _sb_heads_to_lanes(acc_ref[...], ts)


def _decode_attention(cq, sq, new_rows, caches, page_table, rel_bias, lam_p, head_g, lam_init):
    b, ts, _ = cq.shape
    n_pages = page_table.shape[1]
    psz = caches[2].shape[2]
    past = n_pages * psz
    npd = math.gcd(n_pages, DEC_DIFF_PAGES)
    nps = math.gcd(n_pages, DEC_PAGES)
    nrest = n_pages - nps
    nkn = LANES // H_C
    assert _DEC_ROWS == 2 * H_C * ts == H_D * ts and ts <= nkn
    ck, cv, sk, sv = new_rows
    ckf = ck.reshape(b, ts * H_C, LANES)
    cvf = cv.reshape(b, ts * H_C, LANES)
    dk, dv, skt, svt = caches
    pt = page_table.reshape(-1)

    def table(base):
        base = jnp.moveaxis(base, 0, -3)
        own = jnp.arange(H_C)[:, None, None, None] == jnp.arange(H_C)[None, None, None, :]
        tab = jnp.where(own, base[..., None], NEG_INF)
        tab = jnp.broadcast_to(tab[..., :, None, :, :, :], tab.shape[:-3] + (2,) + tab.shape[-3:])
        return tab.reshape(tab.shape[:-5] + (_DEC_ROWS, tab.shape[-2] * H_C))

    tq = jnp.arange(ts, dtype=jnp.int32)
    kpos = (jnp.arange(n_pages - 1, -1, -1, dtype=jnp.int32)[:, None] * psz
            + jnp.arange(psz, dtype=jnp.int32)[None, :])
    dist = past + tq[None, :, None] - kpos[:, None, :]
    bias = table(_bias_of_dist(rel_bias, dist))
    knew = jnp.arange(nkn, dtype=jnp.int32)
    dnew = jnp.where(knew[None, :] < ts, tq[:, None] - knew[None, :], -1)
    bnew = table(_bias_of_dist(rel_bias, dnew))

    page = lambda idx: pl.BlockSpec((1, HALF, LANES), idx)
    const = lambda shape: pl.BlockSpec(shape, lambda *_: (0,) * len(shape))

    nsq = math.gcd(b, DEC_DIFF_SEQS)
    row = pl.BlockSpec((nsq, ts, HALF), lambda bi, g, pt: (bi, 0, 0))
    rowf = pl.BlockSpec((nsq, ts * H_C, LANES), lambda bi, g, pt: (bi, 0, 0))
    hbm = pl.BlockSpec(memory_space=pl.ANY)
    pages = pltpu.VMEM((2, nsq * npd, psz * H_C, LANES), F32)
    oc = pl.pallas_call(
        functools.partial(_dec_diff_kernel, ts=ts, lam_init=lam_init, npg=npd, n_pages=n_pages, nsq=nsq),
        grid_spec=pltpu.PrefetchScalarGridSpec(
            num_scalar_prefetch=1,
            grid=(b // nsq, n_pages // npd),
            in_specs=[row, rowf, rowf, const((n_pages, _DEC_ROWS, psz * H_C)), const((_DEC_ROWS, LANES)),
                      const((4, DH_C)), const((H_C, LANES)), hbm, hbm],
            out_specs=row,
            scratch_shapes=[pltpu.VMEM((nsq, _DEC_ROWS, LANES), BF16), pltpu.VMEM((nsq, _DEC_ROWS, 1), F32),
                            pltpu.VMEM((nsq, _DEC_ROWS, 1), F32), pltpu.VMEM((nsq, _DEC_ROWS, LANES), F32),
                            pages, pages, pltpu.SemaphoreType.DMA((2, 2 * nsq * npd))],
        ),
        out_shape=jax.ShapeDtypeStruct((b, ts, HALF), F32),
        compiler_params=_cp("arbitrary", "arbitrary"),
    )(pt, cq, ckf, cvf, bias, bnew, lam_p, head_g.reshape(H_C, LANES), dk, dv)

    bb = math.gcd(b, DEC_SB_SEQS)
    rows = lambda n, *s: pl.BlockSpec((n,) + s, lambda bi, *_: (bi,) + (0,) * len(s))
    sspecs, sargs = [], []
    for s in range(bb):
        for p in range(nps):
            for c in (skt, svt):
                sspecs.append(page(lambda bi, pt, s=s, p=p: (pt[(bi * bb + s) * n_pages + n_pages - 1 - p], 0, 0)))
                sargs.append(c)
    os_first, acc, r, alive = pl.pallas_call(
        functools.partial(_dec_sb_first_kernel, ts=ts, npg=nps, bb=bb),
        grid_spec=pltpu.PrefetchScalarGridSpec(
            num_scalar_prefetch=1,
            grid=(b // bb,),
            in_specs=[rows(bb, ts, HALF)] * 3 + sspecs,
            out_specs=[rows(bb, ts, HALF), rows(bb, _DEC_ROWS, HALF), rows(bb, _DEC_ROWS, LANES),
                       rows(bb, SUBLANES, LANES)],
        ),
        out_shape=[jax.ShapeDtypeStruct((b, ts, HALF), F32), jax.ShapeDtypeStruct((b, _DEC_ROWS, HALF), F32),
                   jax.ShapeDtypeStruct((b, _DEC_ROWS, LANES), F32),
                   jax.ShapeDtypeStruct((b, SUBLANES, LANES), jnp.int32)],
        compiler_params=_cp("parallel"),
    )(pt, sq, sk, sv, *sargs)
    if nrest == 0:
        return oc, os_first

    alive = alive[:, 0, 0]

    def rest_pages():
        rspecs, rargs = [], []
        for p in range(nrest):
            for c in (skt, svt):
                rspecs.append(page(lambda bi, pt, al, p=p: (
                    jnp.where(al[bi] == 1, pt[bi * n_pages + n_pages - 1 - nps - p], pt[0]), 0, 0)))
                rargs.append(c)
        return pl.pallas_call(
            functools.partial(_dec_sb_rest_kernel, ts=ts, npg=nps, nrest=nrest),
            grid_spec=pltpu.PrefetchScalarGridSpec(
                num_scalar_prefetch=2,
                grid=(b,),
                in_specs=[rows(1, ts, HALF), rows(1, _DEC_ROWS, HALF), rows(1, _DEC_ROWS, LANES)] + rspecs,
                out_specs=rows(1, ts, HALF),
                scratch_shapes=[pltpu.VMEM((_DEC_ROWS, HALF), F32), pltpu.VMEM((_DEC_ROWS, 1), F32)],
            ),
            out_shape=jax.ShapeDtypeStruct((b, ts, HALF), F32),
            compiler_params=_cp("arbitrary"),
        )(pt, alive, sq, acc, r, *rargs)

    os_ = lax.cond(jnp.any(alive == 1), rest_pages, lambda: os_first)
    return oc, os_


def _trunk(x, p, even_states, odd_past, page_table, W):
    b, t, d = x.shape
    m = b * t
    prompt = odd_past is None
    act = BF16 if prompt else F32
    h = x.reshape(m, d)
    depth = p.shape[0]
    new_even, new_odd = [], []
    for l in range(depth):
        j = l // 2
        h = _ffn_half(h, W['ffn_norm1'][l], W['ffn1_wi'][l], W['ffn1_wo'][l])
        if l % 2 == 0:
            buf, c0, n0, m0 = even_states[j]
            u, q, k, v, o, gt = _inproj_even(h, W['mix_norm'][l], W['ev_w_in'][j], W['ev_w_gt'][j], act)
            a_out, buf1 = _conv_module(u.reshape(b, t, HALF), buf, W['ev_conv_w'][j], W['ev_conv_b'][j],
                                       W['ev_ln_g'][j], W['ev_ln_b'][j], act)
            gt = jnp.moveaxis(gt.reshape(2 * H_B, b, t), 1, 0)
            if t < MLSTM_CHUNK:
                padv = jnp.where(jnp.arange(2 * H_B) < H_B, NEG_INF, -NEG_INF).astype(F32)
                gt = jnp.concatenate(
                    [gt, jnp.broadcast_to(padv[None, :, None], (b, 2 * H_B, MLSTM_CHUNK - t))], axis=2)
            r3 = lambda a: a.reshape(b, t, HALF)
            b_out, c1, n1, m1 = _mlstm(r3(q), r3(k), r3(v), r3(o), gt, W['ev_gate_b'][j], c0, n0, m0, act)
            new_even.append((buf1, c1, n1, m1))
            mix_a, mix_b, w_out = a_out.reshape(m, HALF), b_out.reshape(m, HALF), W['ev_w_out'][j]
        else:
            lam_init = 0.8 - 0.6 * math.exp(-0.3 * l)
            cq, ck, cv, sq, sk, sv = _inproj_odd(h, W['mix_norm'][l], W['od_w_in'][j], W['od_w_kvt'][j], act, b, t,
                                                 time_minor=prompt)
            heads = lambda a, nh: a.reshape(b, t, nh, HALF // nh)
            if prompt:
                sb_rows = lambda a: jnp.transpose(a.reshape(b, H_D, DH_D, t), (0, 3, 1, 2))
            else:
                sb_rows = lambda a: heads(a, H_D)
            new_odd.append((heads(ck, H_C), heads(cv, H_C), sb_rows(sk), sb_rows(sv)))
            if prompt:
                oc = _diff_attention(cq, ck, cv, b, t, W['rel_bias'], W['od_lambda'][j], W['od_head_g'][j], lam_init)
                os_ = _sb_attention(sq, sk, sv, b, t)
            else:
                r3 = lambda a: a.reshape(b, t, HALF)
                oc, os_ = _decode_attention(r3(cq), r3(sq), [r3(a) for a in (ck, cv, sk, sv)], odd_past[j],
                                            page_table[j], W['rel_bias'], W['od_lambda'][j], W['od_head_g'][j],
                                            lam_init)
            mix_a, mix_b, w_out = oc.reshape(m, HALF), os_.reshape(m, HALF), W['od_w_out'][j]
        h = _layer_tail(h, mix_a, mix_b, w_out, W['ffn_norm2'][l], W['ffn2_wi'][l], W['ffn2_wo'][l],
                        W['ple_norm'][l], W['ple_wg'][l], p[l].reshape(m, -1), W['ple_wp'][l], W['final_norm'],
                        final=(l == depth - 1))
    return h.reshape(b, t, d), new_even, new_odd


def kernel(x_prompt, x_sample, p_prompt, p_sample, state_conv, state_mlstm_C, state_mlstm_n, state_mlstm_m, cache_diff_k, cache_diff_v, cache_sb_k, cache_sb_v, page_table, ffn_norm1, ffn1_wi, ffn1_wo, mix_norm, ffn_norm2, ffn2_wi, ffn2_wo, ple_norm, ple_wg, ple_wp, ev_w_in, ev_conv_w, ev_conv_b, ev_ln_g, ev_ln_b, ev_gate_b, ev_w_out, od_w_in, od_lambda, od_head_g, od_w_out, rel_bias, final_norm):
    bf = lambda a: a.astype(BF16)
    n_even, n_odd = ev_w_in.shape[0], od_w_in.shape[0]
    W = dict(ffn_norm1=ffn_norm1, ffn1_wi=bf(ffn1_wi), ffn1_wo=bf(ffn1_wo), mix_norm=mix_norm,
             ffn_norm2=ffn_norm2, ffn2_wi=bf(ffn2_wi), ffn2_wo=bf(ffn2_wo),
             ple_norm=ple_norm, ple_wg=bf(ple_wg), ple_wp=bf(ple_wp),
             ev_w_in=bf(ev_w_in[:, :, :6 * HALF]), ev_w_gt=bf(jnp.swapaxes(ev_w_in[:, :, 6 * HALF:], 1, 2)),
             ev_conv_w=ev_conv_w, ev_conv_b=ev_conv_b, ev_ln_g=ev_ln_g, ev_ln_b=ev_ln_b,
             ev_gate_b=ev_gate_b, ev_w_out=bf(ev_w_out),
             od_w_in=bf(od_w_in), od_w_kvt=bf(jnp.swapaxes(od_w_in[:, :, 4 * HALF:], 1, 2)), od_lambda=od_lambda, od_head_g=od_head_g, od_w_out=bf(od_w_out),
             rel_bias=rel_bias, final_norm=final_norm)
    bp, tp = x_prompt.shape[0], x_prompt.shape[1]
    bs, ts = x_sample.shape[0], x_sample.shape[1]
    even_p = [(jnp.zeros((bp, CONV_W - 1, HALF), F32), jnp.zeros((bp, H_B, DH_B, DH_B), F32),
               jnp.zeros((bp, H_B, DH_B), F32), jnp.zeros((bp, H_B), F32)) for _ in range(n_even)]
    y_prompt, ev_p, od_p = _trunk(x_prompt, p_prompt, even_p, None, None, W)
    even_s = [(state_conv[j], state_mlstm_C[j], state_mlstm_n[j], state_mlstm_m[j]) for j in range(n_even)]
    n_pool, psz = cache_diff_k.shape[1], cache_diff_k.shape[2]
    pool_d = lambda c: c.reshape(n_odd * n_pool, psz * H_C, 2 * DH_C)
    pool_s = lambda c: jnp.transpose(c, (0, 1, 3, 4, 2)).reshape(n_odd * n_pool, H_D * DH_D, psz)
    caches = (pool_d(cache_diff_k), pool_d(cache_diff_v), pool_s(cache_sb_k), pool_s(cache_sb_v))
    tables = [page_table + j * n_pool for j in range(n_odd)]
    y_sample, ev_s, od_s = _trunk(x_sample, p_sample, even_s, [caches] * n_odd, tables, W)
    ev = lambda states, i: jnp.stack([s[i] for s in states])
    return (y_prompt, y_sample,
            ev(ev_p, 0), ev(ev_s, 0), ev(ev_p, 1), ev(ev_s, 1),
            ev(ev_p, 2), ev(ev_s, 2), ev(ev_p, 3), ev(ev_s, 3),
            ev(od_p, 0), ev(od_s, 0), ev(od_p, 1), ev(od_s, 1),
            ev(od_p, 2), ev(od_s, 2), ev(od_p, 3), ev(od_s, 3))
```

```python
import functools
import math

import jax
import jax.numpy as jnp
from jax import lax
from jax.experimental import pallas as pl
from jax.experimental.pallas import tpu as pltpu

F32 = jnp.float32
BF16 = jnp.bfloat16

LANES = 128
SUBLANES = 8
VMEM_LIMIT_BYTES = 56 * 1024 * 1024

D_MODEL = 1024
D_FF = 2816
HALF = 512
CONV_W = 31
H_B, DH_B = 4, 128
H_C, DH_C = 4, 64
H_D, DH_D = 8, 64
N_BUCKETS = 32
MAX_DIST = 128
MLSTM_CHUNK = 128
NEG_INF = -1e30
SB_DEAD = -104.0

ROW_TILE = 512
FF_CHUNK = 256
DIFF_BLK = 512
SB_BLK = 256
CONV_ROWS = 256
CONV_SUB = 32
CONV_SEQS = 16
MLSTM_SEQS = 4
DEC_PAGES = 4
DEC_DIFF_PAGES = 8
DEC_SB_SEQS = 2
DEC_DIFF_SEQS = 2


def _cp(*sem):
    return pltpu.CompilerParams(dimension_semantics=sem, vmem_limit_bytes=VMEM_LIMIT_BYTES)


def _rms(x, g, eps=1e-6):
    return x * lax.rsqrt(jnp.mean(x * x, axis=-1, keepdims=True) + eps) * g


def _dot(a, b):
    return jnp.dot(a, b, preferred_element_type=F32)


def _dot_nt(a, b):
    return lax.dot_general(a, b, (((1,), (1,)), ((), ())), preferred_element_type=F32)


def _dot_tn(a, b):
    return lax.dot_general(a, b, (((0,), (0,)), ((), ())), preferred_element_type=F32)


def _log_sigmoid_neg(z):
    return -(jnp.maximum(z, 0.0) + jnp.log1p(jnp.exp(-jnp.abs(z))))


def _split_bf16(x):
    hi = x.astype(BF16)
    lo = (x - hi.astype(F32)).astype(BF16)
    return hi, lo


def _swiglu_half(x, g_ref, wi_ref, wo_ref, act_ref):
    hn = _rms(x, g_ref[...]).astype(BF16)
    for c in range(D_FF // FF_CHUNK):
        lo = c * FF_CHUNK
        gate = _dot(hn, wi_ref[:, lo:lo + FF_CHUNK])
        up = _dot(hn, wi_ref[:, D_FF + lo:D_FF + lo + FF_CHUNK])
        act_ref[:, lo:lo + FF_CHUNK] = (gate * jax.nn.sigmoid(gate) * up).astype(BF16)
    return x + 0.5 * _dot(act_ref[...], wo_ref[...])


def _ffn_kernel(x_ref, g_ref, wi_ref, wo_ref, o_ref, act_ref):
    o_ref[...] = _swiglu_half(x_ref[...], g_ref, wi_ref, wo_ref, act_ref)


def _ffn_half(h, g, wi, wo):
    m, d = h.shape
    tm = min(ROW_TILE, m)
    resident = lambda shape: pl.BlockSpec(shape, lambda i: (0, 0), pipeline_mode=pl.Buffered(1))
    return pl.pallas_call(
        _ffn_kernel,
        grid=(m // tm,),
        in_specs=[
            pl.BlockSpec((tm, d), lambda i: (i, 0)),
            pl.BlockSpec((1, d), lambda i: (0, 0)),
            resident((d, 2 * D_FF)),
            resident((D_FF, d)),
        ],
        out_specs=pl.BlockSpec((tm, d), lambda i: (i, 0)),
        out_shape=jax.ShapeDtypeStruct((m, d), F32),
        scratch_shapes=[pltpu.VMEM((tm, D_FF), BF16)],
        compiler_params=_cp("parallel"),
    )(h, g.reshape(1, d), wi, wo)


def _layer_tail_kernel(x_ref, a_ref, b_ref, wa_ref, wb_ref, g_ref, wi_ref, wo_ref, pg_ref, wg_ref, p_ref, wp_ref,
                       fg_ref, o_ref, act_ref, *, final):
    h = (x_ref[...] + _dot(a_ref[...].astype(BF16), wa_ref[...]) + _dot(b_ref[...].astype(BF16), wb_ref[...]))
    h = _swiglu_half(h, g_ref, wi_ref, wo_ref, act_ref)
    gate = jax.nn.sigmoid(_dot(_rms(h, pg_ref[...]).astype(BF16), wg_ref[...]))
    h = h + gate * _dot(p_ref[...].astype(BF16), wp_ref[...])
    if final:
        h = _rms(h, fg_ref[...])
    o_ref[...] = h


def _layer_tail(h, a, b, w_out, g, wi, wo, pg, wg, p, wp, fg, final):
    m, d = h.shape
    tm = min(ROW_TILE, m)
    pd = p.shape[1]
    row = lambda i: (i, 0)
    resident = lambda shape, idx=(0, 0): pl.BlockSpec(shape, lambda i: idx, pipeline_mode=pl.Buffered(1))
    vec = lambda: pl.BlockSpec((1, d), lambda i: (0, 0))
    return pl.pallas_call(
        functools.partial(_layer_tail_kernel, final=final),
        grid=(m // tm,),
        in_specs=[
            pl.BlockSpec((tm, d), row), pl.BlockSpec((tm, HALF), row), pl.BlockSpec((tm, HALF), row),
            resident((HALF, d)), resident((HALF, d), (1, 0)),
            vec(), resident((d, 2 * D_FF)), resident((D_FF, d)),
            vec(), resident((d, d)), pl.BlockSpec((tm, pd), row), resident((pd, d)), vec(),
        ],
        out_specs=pl.BlockSpec((tm, d), row),
        out_shape=jax.ShapeDtypeStruct((m, d), F32),
        scratch_shapes=[pltpu.VMEM((tm, D_FF), BF16)],
        compiler_params=_cp("parallel"),
    )(h, a, b, w_out, w_out, g.reshape(1, d), wi, wo, pg.reshape(1, d), wg, p, wp, fg.reshape(1, d))


def _inproj_even_kernel(x_ref, g_ref, w_ref, wgt_ref, u_ref, q_ref, k_ref, v_ref, o_ref, gt_ref):
    hn = _rms(x_ref[...], g_ref[...]).astype(BF16)

    def col(c):
        return _dot(hn, w_ref[:, c * HALF:(c + 1) * HALF])

    u_ref[...] = col(0) * jax.nn.sigmoid(col(1))
    q_ref[...] = col(2).astype(q_ref.dtype)
    k_ref[...] = (col(3) * (DH_B ** -0.5)).astype(k_ref.dtype)
    v_ref[...] = col(4).astype(v_ref.dtype)
    o_ref[...] = col(5)
    gt_ref[...] = _dot_nt(wgt_ref[...], hn)


def _inproj_even(h, g, w, wgt, qkv_dtype):
    m, d = h.shape
    tm = min(ROW_TILE, m)
    row = lambda i: (i, 0)
    out = lambda dt: jax.ShapeDtypeStruct((m, HALF), dt)
    return pl.pallas_call(
        _inproj_even_kernel,
        grid=(m // tm,),
        in_specs=[
            pl.BlockSpec((tm, d), row),
            pl.BlockSpec((1, d), lambda i: (0, 0)),
            pl.BlockSpec((d, 6 * HALF), lambda i: (0, 0)),
            pl.BlockSpec((2 * H_B, d), lambda i: (0, 0)),
        ],
        out_specs=[pl.BlockSpec((tm, HALF), row)] * 5 + [pl.BlockSpec((2 * H_B, tm), lambda i: (0, i))],
        out_shape=[out(F32), out(qkv_dtype), out(qkv_dtype), out(qkv_dtype), out(F32),
                   jax.ShapeDtypeStruct((2 * H_B, m), F32)],
        compiler_params=_cp("parallel"),
    )(h, g.reshape(1, d), w, wgt)


def _inproj_odd_kernel(x_ref, g_ref, w_ref, wkvt_ref, cq_ref, ck_ref, cv_ref, sq_ref, sk_ref, sv_ref, *, time_minor):
    hn = _rms(x_ref[...], g_ref[...]).astype(BF16)

    def col(c):
        return _dot(hn, w_ref[:, c * HALF:(c + 1) * HALF])

    cq_ref[...] = (col(0) * (DH_C ** -0.5)).astype(cq_ref.dtype)
    ck_ref[...] = col(1)
    cv_ref[...] = col(2)
    sq_ref[...] = (col(3) * (DH_D ** -0.5)).astype(sq_ref.dtype)
    if time_minor:
        sk_ref[0] = _dot_nt(wkvt_ref[0:HALF, :], hn)
        sv_ref[0] = _dot_nt(wkvt_ref[HALF:2 * HALF, :], hn)
    else:
        sk_ref[...] = col(4)
        sv_ref[...] = col(5)


def _inproj_odd(h, g, w, wkvt, q_dtype, b, t, time_minor):
    m, d = h.shape
    tm = min(ROW_TILE, m)
    row = lambda i: (i, 0)
    out = lambda dt: jax.ShapeDtypeStruct((m, HALF), dt)
    rows = pl.BlockSpec((tm, HALF), row)
    if time_minor:
        assert t % tm == 0
        nt = t // tm
        kv_spec = pl.BlockSpec((1, HALF, tm), lambda i: (i // nt, 0, i % nt))
        kv_shape = jax.ShapeDtypeStruct((b, HALF, t), F32)
    else:
        kv_spec, kv_shape = rows, out(F32)
    return pl.pallas_call(
        functools.partial(_inproj_odd_kernel, time_minor=time_minor),
        grid=(m // tm,),
        in_specs=[
            pl.BlockSpec((tm, d), row),
            pl.BlockSpec((1, d), lambda i: (0, 0)),
            pl.BlockSpec((d, 6 * HALF), lambda i: (0, 0)),
            pl.BlockSpec((2 * HALF, d), lambda i: (0, 0)),
        ],
        out_specs=[rows] * 4 + [kv_spec] * 2,
        out_shape=[out(q_dtype), out(F32), out(F32), out(q_dtype), kv_shape, kv_shape],
        compiler_params=_cp("parallel"),
    )(h, g.reshape(1, d), w, wkvt)


_HIST = 32


def _conv_kernel(u_ref, buf_ref, taps_ref, cb_ref, lg_ref, lb_ref, a_ref, st_ref, win_ref, sh_ref, *, tt, sub, bb):
    t = pl.program_id(1)
    pad = _HIST - (CONV_W - 1)
    span = tt + _HIST - SUBLANES

    for s in range(bb):
        @pl.when(t == 0)
        def _(s=s):
            win_ref[s, 0:SUBLANES, :] = jnp.zeros((SUBLANES, HALF), F32)
            win_ref[s, pad:_HIST, :] = buf_ref[s]

        win_ref[s, _HIST:_HIST + tt, :] = u_ref[s]
        for r in range(1, SUBLANES):
            sh_ref[s, r - 1] = win_ref[s, pl.ds(r, span), :]
        for rb in range(tt // sub):
            acc = jnp.zeros((sub, HALF), F32)
            for w in range(CONV_W):
                a8, r = divmod(pad + w, SUBLANES)
                lo = rb * sub + a8 * SUBLANES
                src = win_ref[s, lo:lo + sub, :] if r == 0 else sh_ref[s, r - 1, lo:lo + sub, :]
                acc = acc + src * taps_ref[w:w + 1, :]
            c = acc + cb_ref[...]
            mu = jnp.mean(c, axis=-1, keepdims=True)
            var = jnp.mean(jnp.square(c - mu), axis=-1, keepdims=True)
            cn = (c - mu) * lax.rsqrt(var + 1e-5) * lg_ref[...] + lb_ref[...]
            a_ref[s, rb * sub:(rb + 1) * sub, :] = (cn * jax.nn.sigmoid(cn)).astype(a_ref.dtype)

        @pl.when(t == pl.num_programs(1) - 1)
        def _(s=s):
            st_ref[s] = win_ref[s, tt + pad:tt + _HIST, :]

        win_ref[s, 0:_HIST, :] = win_ref[s, tt:tt + _HIST, :]


def _conv_module(u, buf, taps, cb, lg, lb, out_dtype):
    b, t, _ = u.shape
    tt = min(CONV_ROWS, t)
    sub = min(CONV_SUB, tt)
    bb = 1 if t > tt else math.gcd(b, CONV_SEQS)
    vec = lambda: pl.BlockSpec((1, HALF), lambda i, j: (0, 0))
    return pl.pallas_call(
        functools.partial(_conv_kernel, tt=tt, sub=sub, bb=bb),
        grid=(b // bb, t // tt),
        in_specs=[
            pl.BlockSpec((bb, tt, HALF), lambda i, j: (i, j, 0)),
            pl.BlockSpec((bb, CONV_W - 1, HALF), lambda i, j: (i, 0, 0)),
            pl.BlockSpec((CONV_W, HALF), lambda i, j: (0, 0)),
            vec(), vec(), vec(),
        ],
        out_specs=[
            pl.BlockSpec((bb, tt, HALF), lambda i, j: (i, j, 0)),
            pl.BlockSpec((bb, CONV_W - 1, HALF), lambda i, j: (i, 0, 0)),
        ],
        out_shape=[jax.ShapeDtypeStruct((b, t, HALF), out_dtype),
                   jax.ShapeDtypeStruct((b, CONV_W - 1, HALF), F32)],
        scratch_shapes=[pltpu.VMEM((bb, _HIST + tt, HALF), F32),
                        pltpu.VMEM((bb, SUBLANES - 1, tt + _HIST - SUBLANES, HALF), F32)],
        compiler_params=_cp("parallel", "arbitrary"),
    )(u, buf, taps, cb.reshape(1, HALF), lg.reshape(1, HALF), lb.reshape(1, HALF))


def _scan_lanes(x, op, fill):
    lane = lax.broadcasted_iota(jnp.int32, x.shape, 1)
    sh = 1
    while sh < x.shape[1]:
        x = op(x, jnp.where(lane >= sh, pltpu.roll(x, sh, 1), fill))
        sh *= 2
    return x


def _mlstm_kernel(q_ref, k_ref, v_ref, o_ref, g_ref, gb_ref, c0_ref, n0_ref, m0_ref,
                  h_ref, c_ref, n_ref, m_ref, cx_ref, *, lr, bb):
    L = MLSTM_CHUNK
    ng = 2 * H_B
    nr = bb * ng

    @pl.when(pl.program_id(1) == 0)
    def _():
        m_ref[...] = m0_ref[...]
        for s in range(bb):
            for h in range(H_B):
                cx_ref[s, h, :, 0:DH_B] = c0_ref[s, h]
                cx_ref[s, h, :, DH_B:2 * DH_B] = jnp.transpose(jnp.broadcast_to(n0_ref[s, h:h + 1, :], (DH_B, DH_B)))

    def rows(ref, s):
        x = ref[s]
        if lr < L:
            x = jnp.concatenate([x.astype(F32), jnp.zeros((L - lr, HALF), F32)], axis=0)
        return x.astype(BF16)

    g = g_ref[...].reshape(nr, L) + gb_ref[...]
    row = lax.broadcasted_iota(jnp.int32, g.shape, 0)
    is_li = row % ng < H_B
    bcum = _scan_lanes(jnp.where(is_li, 0.0, _log_sigmoid_neg(-g)), jnp.add, 0.0)
    b = pltpu.roll(bcum, nr - H_B, 0)
    a = jnp.where(is_li, g - b, 0.0)
    m_old = m_ref[...].reshape(nr, LANES)
    mx = jnp.maximum(m_old, _scan_lanes(a, jnp.maximum, -3e38))
    b_last = b[:, L - 1:L]
    m_new = b_last + mx[:, L - 1:L]
    w_inter = jnp.exp(m_old - mx)
    inv_floor = jnp.exp(-(b + mx))
    w_key = jnp.exp(a + b_last - m_new)
    decay = jnp.exp(b_last + m_old - m_new)
    m_ref[...] = jnp.where(is_li, jnp.broadcast_to(m_new, (nr, LANES)), 0.0).reshape(bb, ng, LANES)
    packed = jnp.concatenate([mx, w_inter, inv_floor, w_key] + [jnp.zeros((L - 4 * nr, L), F32)] * (4 * nr < L), axis=0)
    cols = jnp.transpose(packed)
    tpos = lax.broadcasted_iota(jnp.int32, (L, L), 0)
    spos = lax.broadcasted_iota(jnp.int32, (L, L), 1)
    causal = spos <= tpos
    ones = jnp.ones((L, DH_B), BF16)
    for s in range(bb):
        q, k, v = rows(q_ref, s), rows(k_ref, s), rows(v_ref, s)
        outs = []
        for h in range(H_B):
            sl = slice(h * DH_B, (h + 1) * DH_B)
            qh, kh = q[:, sl], k[:, sl]
            v1 = jnp.concatenate([v[:, sl], ones], axis=1)
            i = s * ng + h
            col = lambda vec: cols[:, vec * nr + i:vec * nr + i + 1]
            gate = jnp.where(causal, jnp.exp(a[i:i + 1, :] - col(0)), 0.0)
            sc = _dot_nt(qh, kh) * gate
            cx = cx_ref[s, h]
            mix = col(1) * _dot(qh, cx.astype(BF16)) + _dot(sc.astype(BF16), v1)
            outs.append(mix[:, 0:DH_B] / jnp.maximum(jnp.abs(mix[:, DH_B:2 * DH_B]), col(2)))
            kw = (kh.astype(F32) * col(3)).astype(BF16)
            cx_ref[s, h] = decay[i:i + 1, 0:1] * cx + _dot_tn(kw, v1)
        hs = jnp.concatenate(outs, axis=1)
        h_ref[s] = (jax.nn.sigmoid(o_ref[s]) * hs[0:lr]).astype(h_ref.dtype)

    @pl.when(pl.program_id(1) == pl.num_programs(1) - 1)
    def _():
        for s in range(bb):
            for h in range(H_B):
                c_ref[s, h] = cx_ref[s, h, :, 0:DH_B]
                n_ref[s, h:h + 1, :] = jnp.transpose(cx_ref[s, h, :, DH_B:2 * DH_B])[0:1, :]


def _mlstm(q, k, v, o, gt, gate_b, c0, n0, m0, out_dtype):
    b, t, _ = q.shape
    L = MLSTM_CHUNK
    lr = min(L, t)
    nc = t // lr
    m0b = jnp.broadcast_to(jnp.pad(m0, ((0, 0), (0, SUBLANES - H_B)))[:, :, None], (b, SUBLANES, LANES))
    bb = math.gcd(b, MLSTM_SEQS)
    gb = jnp.broadcast_to(jnp.tile(gate_b, bb).reshape(bb * 2 * H_B, 1), (bb * 2 * H_B, L))
    blk = pl.BlockSpec((bb, lr, HALF), lambda i, j: (i, j, 0))
    st = lambda *s: pl.BlockSpec((bb,) + s, lambda i, j: (i,) + (0,) * len(s))
    h, c1, n1, m1 = pl.pallas_call(
        functools.partial(_mlstm_kernel, lr=lr, bb=bb),
        grid=(b // bb, nc),
        in_specs=[blk, blk, blk, blk,
                  pl.BlockSpec((bb, 2 * H_B, L), lambda i, j: (i, 0, j)),
                  pl.BlockSpec((bb * 2 * H_B, L), lambda i, j: (0, 0)),
                  st(H_B, DH_B, DH_B), st(H_B, DH_B), st(SUBLANES, LANES)],
        out_specs=[blk, st(H_B, DH_B, DH_B), st(H_B, DH_B), st(SUBLANES, LANES)],
        out_shape=[jax.ShapeDtypeStruct((b, t, HALF), out_dtype),
                   jax.ShapeDtypeStruct((b, H_B, DH_B, DH_B), F32),
                   jax.ShapeDtypeStruct((b, H_B, DH_B), F32),
                   jax.ShapeDtypeStruct((b, SUBLANES, LANES), F32)],
        scratch_shapes=[pltpu.VMEM((bb, H_B, DH_B, 2 * DH_B), F32)],
        compiler_params=_cp("parallel", "arbitrary"),
    )(q, k, v, o, gt, gb, c0, n0, m0b)
    return h, c1, n1, m1[:, :H_B, 0]


def _t5_bucket(dist):
    n = jnp.maximum(dist, 0)
    exact = N_BUCKETS // 2
    nf = jnp.maximum(n, 1).astype(F32)
    large = exact + (jnp.log(nf / exact) / math.log(MAX_DIST / exact) * (N_BUCKETS - exact)).astype(jnp.int32)
    return jnp.where(n < exact, n, jnp.minimum(large, N_BUCKETS - 1))


def _bias_of_dist(rel_bias, dist):
    onehot = jax.nn.one_hot(_t5_bucket(dist), N_BUCKETS, dtype=F32)
    b = jnp.einsum('...k,kh->h...', onehot, rel_bias.astype(F32), precision=lax.Precision.HIGHEST)
    return jnp.where(dist >= 0, b, NEG_INF)


def _lambda(lp_ref, lam_init):
    lp = lp_ref[...]
    s1 = jnp.sum(lp[0:1] * lp[1:2], axis=-1, keepdims=True)
    s2 = jnp.sum(lp[2:3] * lp[3:4], axis=-1, keepdims=True)
    return jnp.exp(s1) - jnp.exp(s2) + lam_init


def _head_norm(x, hg, lam_init):
    return x * lax.rsqrt(jnp.mean(x * x, axis=-1, keepdims=True) + 1e-6) * hg * (1.0 - lam_init)


def _diff_attn_kernel(q_ref, k_ref, v_ref, bd_ref, bp_ref, lp_ref, hg_ref, o_ref, kb_ref, vt_ref, *, lam_init):
    blk = DIFF_BLK
    i = pl.program_id(2)

    @pl.when(i == 0)
    def _():
        kb_ref[...] = k_ref[...].astype(BF16)
        for c in range(vt_ref.shape[0]):
            vt_ref[c] = jnp.transpose(v_ref[c * blk:(c + 1) * blk, :]).astype(BF16)

    q = q_ref[...]
    lane = lax.broadcasted_iota(jnp.int32, q.shape, 1)
    zero = jnp.zeros_like(q)
    q2 = jnp.concatenate([jnp.where(lane < DH_C, q, zero), jnp.where(lane >= DH_C, q, zero)], axis=0)

    def scores(j):
        off = pl.multiple_of(j * blk, blk)
        return _dot_nt(kb_ref[pl.ds(off, blk), :], q2)

    def update(carry, s, vt):
        m, l, acc = carry
        mn = jnp.maximum(m, jnp.max(s, axis=0, keepdims=True))
        p = jnp.exp(s - mn)
        al = jnp.exp(m - mn)
        return mn, al * l + jnp.sum(p, axis=0, keepdims=True), al * acc + _dot(vt, p.astype(BF16))

    j_prev = jnp.maximum(i - 1, 0)
    s = scores(i) + bd_ref[0]
    s_prev = scores(j_prev) + bp_ref[0] + jnp.where(i >= 1, 0.0, NEG_INF)
    m = jnp.maximum(jnp.max(s, axis=0, keepdims=True), jnp.max(s_prev, axis=0, keepdims=True))
    p, p_prev = jnp.exp(s - m), jnp.exp(s_prev - m)
    carry = (m, jnp.sum(p, axis=0, keepdims=True) + jnp.sum(p_prev, axis=0, keepdims=True),
             _dot(vt_ref[i], p.astype(BF16)) + _dot(vt_ref[j_prev], p_prev.astype(BF16)))

    n_far = jnp.maximum(i - 1, 0)
    odd = lax.rem(n_far, 2)
    carry = lax.cond(odd == 1, lambda c: update(c, scores(0), vt_ref[0]), lambda c: c, carry)

    def far_pair(jj, c):
        m, l, acc = c
        j = odd + 2 * jj
        s0, s1 = scores(j), scores(j + 1)
        mn = jnp.maximum(m, jnp.maximum(jnp.max(s0, axis=0, keepdims=True), jnp.max(s1, axis=0, keepdims=True)))
        p0, p1 = jnp.exp(s0 - mn), jnp.exp(s1 - mn)
        al = jnp.exp(m - mn)
        l = al * l + jnp.sum(p0, axis=0, keepdims=True) + jnp.sum(p1, axis=0, keepdims=True)
        acc = al * acc + _dot(vt_ref[j], p0.astype(BF16)) + _dot(vt_ref[j + 1], p1.astype(BF16))
        return mn, l, acc

    m, l, acc = lax.fori_loop(0, n_far // 2, far_pair, carry)
    o = acc / l
    o = jnp.transpose(o[:, 0:blk] - _lambda(lp_ref, lam_init) * o[:, blk:2 * blk])
    o_ref[...] = _head_norm(o, hg_ref[0], lam_init).astype(o_ref.dtype)


def _diff_attention(cq, ck, cv, b, t, rel_bias, lam_p, head_g, lam_init):
    blk = DIFF_BLK
    nq = t // blk
    r = jnp.arange(blk, dtype=jnp.int32)
    d0 = r[None, :] - r[:, None]
    assert blk + 1 >= MAX_DIST
    far = _bias_of_dist(rel_bias, jnp.full((1, 1), 2 * blk, jnp.int32))
    rel = lambda d: jnp.tile(jnp.where(d >= 0, _bias_of_dist(rel_bias, d) - far, NEG_INF), (1, 1, 2))
    bd = rel(d0)
    bp = rel(d0 + blk)
    per_head = lambda *s: pl.BlockSpec((1,) + s, lambda bi, h, i: (h, 0, 0))
    return pl.pallas_call(
        functools.partial(_diff_attn_kernel, lam_init=lam_init),
        grid=(b, H_C, nq),
        in_specs=[
            pl.BlockSpec((blk, LANES), lambda bi, h, i: (bi * nq + i, h)),
            pl.BlockSpec((t, LANES), lambda bi, h, i: (bi, h)),
            pl.BlockSpec((t, LANES), lambda bi, h, i: (bi, h)),
            per_head(blk, 2 * blk), per_head(blk, 2 * blk),
            pl.BlockSpec((4, DH_C), lambda bi, h, i: (0, 0)),
            per_head(1, LANES),
        ],
        out_specs=pl.BlockSpec((blk, LANES), lambda bi, h, i: (bi * nq + i, h)),
        out_shape=jax.ShapeDtypeStruct((b * t, HALF), BF16),
        scratch_shapes=[pltpu.VMEM((t, LANES), BF16), pltpu.VMEM((nq, LANES, blk), BF16)],
        compiler_params=_cp("parallel", "parallel", "arbitrary"),
    )(cq, ck, cv, bd, bp, lam_p, head_g.reshape(H_C, 1, LANES))


def _sb_tile(q2, kt, vt, upper, r, mask):
    z = _dot(q2, kt)
    lk = _log_sigmoid_neg(z)
    if mask is not None:
        lk = jnp.where(mask, lk, 0.0)
    hi, lo = _split_bf16(lk)
    after = _dot(hi, upper) + _dot(lo, upper) + r
    w = jnp.exp(lk + z + after)
    if mask is not None:
        w = jnp.where(mask, w, 0.0)
    return _dot_nt(w.astype(BF16), vt), r + jnp.sum(lk, axis=-1, keepdims=True)


def _strict_upper(n):
    j = lax.broadcasted_iota(jnp.int32, (n, n), 0)
    s = lax.broadcasted_iota(jnp.int32, (n, n), 1)
    return jnp.where(j > s, 1.0, 0.0).astype(BF16)


def _sb_attn_kernel(q_ref, k_ref, v_ref, o_ref, kb_ref, vb_ref):
    blk = SB_BLK
    i = pl.program_id(2)

    @pl.when(i == 0)
    def _():
        for c in range(kb_ref.shape[0]):
            kb_ref[c] = k_ref[0, :, c * blk:(c + 1) * blk].astype(BF16)
            vb_ref[c] = v_ref[0, :, c * blk:(c + 1) * blk].astype(BF16)

    q = q_ref[...]
    lane = lax.broadcasted_iota(jnp.int32, q.shape, 1)
    zero = jnp.zeros_like(q)
    q2 = jnp.concatenate([jnp.where(lane < DH_D, q, zero), jnp.where(lane >= DH_D, q, zero)], axis=0)
    upper = _strict_upper(blk)
    tpos = lax.broadcasted_iota(jnp.int32, (2 * blk, blk), 0) % blk
    spos = lax.broadcasted_iota(jnp.int32, (2 * blk, blk), 1)

    def tile(j, r, mask):
        return _sb_tile(q2, kb_ref[j], vb_ref[j], upper, r, mask)

    acc, r = tile(i, jnp.zeros((2 * blk, 1), F32), spos < tpos)
    pv, r = tile(jnp.maximum(i - 1, 0), r, jnp.broadcast_to(i >= 1, spos.shape))
    acc = acc + pv

    def cond(c):
        j, _, r = c
        return jnp.logical_and(j >= 0, jnp.max(r) > SB_DEAD)

    def body(c):
        j, acc, r = c
        pv, r = tile(j, r, None)
        return j - 1, acc + pv, r

    _, acc, _ = lax.while_loop(cond, body, (i - 2, acc, r))
    o_ref[...] = jnp.where(lane < DH_D, acc[0:blk], acc[blk:2 * blk]).astype(o_ref.dtype)


def _sb_attention(sq, sk, sv, b, t):
    blk = SB_BLK
    nq = t // blk
    return pl.pallas_call(
        _sb_attn_kernel,
        grid=(b, HALF // LANES, nq),
        in_specs=[
            pl.BlockSpec((blk, LANES), lambda bi, h, i: (bi * nq + i, h)),
            pl.BlockSpec((1, LANES, t), lambda bi, h, i: (bi, h, 0)),
            pl.BlockSpec((1, LANES, t), lambda bi, h, i: (bi, h, 0)),
        ],
        out_specs=pl.BlockSpec((blk, LANES), lambda bi, h, i: (bi * nq + i, h)),
        out_shape=jax.ShapeDtypeStruct((b * t, HALF), BF16),
        scratch_shapes=[pltpu.VMEM((nq, LANES, blk), BF16), pltpu.VMEM((nq, LANES, blk), BF16)],
        compiler_params=_cp("parallel", "parallel", "arbitrary"),
    )(sq, sk, sv)


_DEC_ROWS = 64


def _pad_rows(x, n):
    return jnp.concatenate([x, jnp.zeros((n - x.shape[0], x.shape[1]), F32)], axis=0).astype(BF16)


def _dec_diff_kernel(pt_ref, cq_ref, ckf_ref, cvf_ref, bias_ref, bnew_ref, lp_ref, hg_ref, dk_hbm, dv_hbm,
                     oc_ref, qa_ref, m_ref, l_ref, acc_ref, kbuf_ref, vbuf_ref, sem_ref,
                     *, ts, lam_init, npg, n_pages, nsq):
    g = pl.program_id(1)
    n_groups = pl.num_programs(1)
    step = pl.program_id(0) * n_groups + g
    slot = lax.rem(step, 2)

    def page_copies(st, sl):
        blk, grp = lax.div(st, n_groups), lax.rem(st, n_groups)
        out = []
        for q in range(nsq):
            for p in range(npg):
                page = pt_ref[(blk * nsq + q) * n_pages + n_pages - 1 - (grp * npg + p)]
                i = q * npg + p
                out.append(pltpu.make_async_copy(dk_hbm.at[page], kbuf_ref.at[sl, i], sem_ref.at[sl, 2 * i]))
                out.append(pltpu.make_async_copy(dv_hbm.at[page], vbuf_ref.at[sl, i], sem_ref.at[sl, 2 * i + 1]))
        return out

    def start(st, sl):
        for i, c in enumerate(page_copies(st, sl)):
            c.start(priority=i % 2)

    @pl.when(step == 0)
    def _():
        start(step, slot)

    @pl.when(step + 1 < pl.num_programs(0) * n_groups)
    def _():
        start(step + 1, 1 - slot)

    def update(q, tiles):
        m = m_ref[q]
        mn = m
        for s, _ in tiles:
            mn = jnp.maximum(mn, jnp.max(s, axis=-1, keepdims=True))
        al = jnp.exp(m - mn)
        l = al * l_ref[q]
        acc = al * acc_ref[q]
        for s, v in tiles:
            p = jnp.exp(s - mn)
            l = l + jnp.sum(p, axis=-1, keepdims=True)
            acc = acc + _dot(p.astype(BF16), v)
        m_ref[q] = mn
        l_ref[q] = l
        acc_ref[q] = acc

    @pl.when(g == 0)
    def _():
        half = lax.broadcasted_iota(jnp.int32, (ts, LANES), 1) >= DH_C
        m_ref[...] = jnp.full_like(m_ref, NEG_INF)
        l_ref[...] = jnp.zeros_like(l_ref)
        acc_ref[...] = jnp.zeros_like(acc_ref)
        for q in range(nsq):
            cq = cq_ref[q]
            qa_ref[q] = jnp.concatenate(
                [jnp.where(half if c % 2 else jnp.logical_not(half), cq[:, (c // 2) * LANES:(c // 2 + 1) * LANES], 0.0)
                 for c in range(2 * H_C)], axis=0).astype(BF16)
            update(q, [(_dot_nt(qa_ref[q], _pad_rows(ckf_ref[q], LANES)) + bnew_ref[...],
                        _pad_rows(cvf_ref[q], LANES))])

    for c in page_copies(step, slot):
        c.wait()
    for q in range(nsq):
        qa = qa_ref[q]
        update(q, [(_dot_nt(qa, kbuf_ref[slot, q * npg + p].astype(BF16)) + bias_ref[g * npg + p],
                    vbuf_ref[slot, q * npg + p].astype(BF16)) for p in range(npg)])

    @pl.when(g == pl.num_programs(1) - 1)
    def _():
        lam = _lambda(lp_ref, lam_init)
        for q in range(nsq):
            o = acc_ref[q] / l_ref[q]
            for h in range(H_C):
                r0 = h * 2 * ts
                oh = o[r0:r0 + ts] - lam * o[r0 + ts:r0 + 2 * ts]
                oc_ref[q, :, h * LANES:(h + 1) * LANES] = _head_norm(oh, hg_ref[h:h + 1, :], lam_init)


def _sb_queries(sq):
    lane = lax.broadcasted_iota(jnp.int32, sq.shape, 1)
    return jnp.concatenate(
        [jnp.where((lane >= c * DH_D) & (lane < (c + 1) * DH_D), sq, 0.0) for c in range(H_D)], axis=0).astype(BF16)


def _sb_fold(tiles, upper, r, acc):
    for z, pv, mask in tiles:
        lk = _log_sigmoid_neg(z)
        if mask is not None:
            lk = jnp.where(mask, lk, 0.0)
        hi, lo = _split_bf16(lk)
        w = jnp.exp(lk + z + _dot(hi, upper) + _dot(lo, upper) + r)
        if mask is not None:
            w = jnp.where(mask, w, 0.0)
        acc = acc + pv(w.astype(BF16))
        r = r + jnp.sum(lk, axis=-1, keepdims=True)
    return r, acc


def _sb_page_tiles(qs, pages):
    return [(_dot(qs, pages[2 * p][0].astype(BF16)), functools.partial(_dot_nt, b=pages[2 * p + 1][0].astype(BF16)),
             None) for p in range(len(pages) // 2)]


def _sb_heads_to_lanes(acc, ts):
    lane = lax.broadcasted_iota(jnp.int32, (ts, LANES), 1)
    outs = []
    for pr in range(H_D // 2):
        sl = slice(pr * LANES, (pr + 1) * LANES)
        r0 = pr * 2 * ts
        outs.append(jnp.where(lane < DH_D, acc[r0:r0 + ts, sl], acc[r0 + ts:r0 + 2 * ts, sl]))
    return jnp.concatenate(outs, axis=1)


def _dec_sb_first_kernel(pt_ref, sq_ref, sk_ref, sv_ref, *rest, ts, npg, bb):
    pages = rest[:2 * npg * bb]
    os_ref, acc_ref, r_ref, alive_ref = rest[2 * npg * bb:]
    psz = pages[0].shape[2]
    upper = _strict_upper(psz)
    tq = lax.broadcasted_iota(jnp.int32, (_DEC_ROWS, psz), 0) % ts
    kpos = lax.broadcasted_iota(jnp.int32, (_DEC_ROWS, psz), 1)
    for s in range(bb):
        qs = _sb_queries(sq_ref[s])
        sv_new = _pad_rows(sv_ref[s], psz)
        tiles = [(_dot_nt(qs, _pad_rows(sk_ref[s], psz)), lambda w, sv_new=sv_new: _dot(w, sv_new), kpos < tq)]
        r, acc = _sb_fold(tiles + _sb_page_tiles(qs, pages[2 * npg * s:2 * npg * (s + 1)]), upper,
                          jnp.zeros((_DEC_ROWS, 1), F32), jnp.zeros((_DEC_ROWS, HALF), F32))
        os_ref[s] = _sb_heads_to_lanes(acc, ts)
        acc_ref[s] = acc
        r_ref[s] = jnp.broadcast_to(r, (_DEC_ROWS, LANES))
        alive = jnp.max(r, axis=0, keepdims=True) > SB_DEAD
        alive_ref[s] = jnp.broadcast_to(jnp.where(alive, 1, 0), (SUBLANES, LANES)).astype(jnp.int32)


def _dec_sb_rest_kernel(pt_ref, al_ref, sq_ref, acc_in_ref, r_in_ref, *rest, ts, npg, nrest):
    pages = rest[:2 * nrest]
    os_ref, acc_ref, r_ref = rest[2 * nrest:]
    acc_ref[...] = acc_in_ref[0]
    r_ref[...] = r_in_ref[0][:, 0:1]

    @pl.when(al_ref[pl.program_id(0)] == 1)
    def _():
        qs = _sb_queries(sq_ref[0])
        upper = _strict_upper(pages[0].shape[2])
        for grp in range(nrest // npg):
            @pl.when(jnp.max(r_ref[...]) > SB_DEAD)
            def _(grp=grp):
                tiles = _sb_page_tiles(qs, pages[2 * npg * grp:2 * npg * (grp + 1)])
                r, acc = _sb_fold(tiles, upper, r_ref[...], acc_ref[...])
                r_ref[...] = r
                acc_ref[...] = acc

    os_ref[0] = _sb_heads_to_lanes(acc_ref[...], ts)


def _decode_attention(cq, sq, new_rows, caches, page_table, rel_bias, lam_p, head_g, lam_init):
    b, ts, _ = cq.shape
    n_pages = page_table.shape[1]
    psz = caches[2].shape[2]
    past = n_pages * psz
    npd = math.gcd(n_pages, DEC_DIFF_PAGES)
    nps = math.gcd(n_pages, DEC_PAGES)
    nrest = n_pages - nps
    nkn = LANES // H_C
    assert _DEC_ROWS == 2 * H_C * ts == H_D * ts and ts <= nkn
    ck, cv, sk, sv = new_rows
    ckf = ck.reshape(b, ts * H_C, LANES)
    cvf = cv.reshape(b, ts * H_C, LANES)
    dk, dv, skt, svt = caches
    pt = page_table.reshape(-1)

    def table(base):
        base = jnp.moveaxis(base, 0, -3)
        own = jnp.arange(H_C)[:, None, None, None] == jnp.arange(H_C)[None, None, None, :]
        tab = jnp.where(own, base[..., None], NEG_INF)
        tab = jnp.broadcast_to(tab[..., :, None, :, :, :], tab.shape[:-3] + (2,) + tab.shape[-3:])
        return tab.reshape(tab.shape[:-5] + (_DEC_ROWS, tab.shape[-2] * H_C))

    tq = jnp.arange(ts, dtype=jnp.int32)
    kpos = (jnp.arange(n_pages - 1, -1, -1, dtype=jnp.int32)[:, None] * psz
            + jnp.arange(psz, dtype=jnp.int32)[None, :])
    dist = past + tq[None, :, None] - kpos[:, None, :]
    bias = table(_bias_of_dist(rel_bias, dist))
    knew = jnp.arange(nkn, dtype=jnp.int32)
    dnew = jnp.where(knew[None, :] < ts, tq[:, None] - knew[None, :], -1)
    bnew = table(_bias_of_dist(rel_bias, dnew))

    page = lambda idx: pl.BlockSpec((1, HALF, LANES), idx)
    const = lambda shape: pl.BlockSpec(shape, lambda *_: (0,) * len(shape))

    nsq = math.gcd(b, DEC_DIFF_SEQS)
    row = pl.BlockSpec((nsq, ts, HALF), lambda bi, g, pt: (bi, 0, 0))
    rowf = pl.BlockSpec((nsq, ts * H_C, LANES), lambda bi, g, pt: (bi, 0, 0))
    hbm = pl.BlockSpec(memory_space=pl.ANY)
    pages = pltpu.VMEM((2, nsq * npd, psz * H_C, LANES), F32)
    oc = pl.pallas_call(
        functools.partial(_dec_diff_kernel, ts=ts, lam_init=lam_init, npg=npd, n_pages=n_pages, nsq=nsq),
        grid_spec=pltpu.PrefetchScalarGridSpec(
            num_scalar_prefetch=1,
            grid=(b // nsq, n_pages // npd),
            in_specs=[row, rowf, rowf, const((n_pages, _DEC_ROWS, psz * H_C)), const((_DEC_ROWS, LANES)),
                      const((4, DH_C)), const((H_C, LANES)), hbm, hbm],
            out_specs=row,
            scratch_shapes=[pltpu.VMEM((nsq, _DEC_ROWS, LANES), BF16), pltpu.VMEM((nsq, _DEC_ROWS, 1), F32),
                            pltpu.VMEM((nsq, _DEC_ROWS, 1), F32), pltpu.VMEM((nsq, _DEC_ROWS, LANES), F32),
                            pages, pages, pltpu.SemaphoreType.DMA((2, 2 * nsq * npd))],
        ),
        out_shape=jax.ShapeDtypeStruct((b, ts, HALF), F32),
        compiler_params=_cp("arbitrary", "arbitrary"),
    )(pt, cq, ckf, cvf, bias, bnew, lam_p, head_g.reshape(H_C, LANES), dk, dv)

    bb = math.gcd(b, DEC_SB_SEQS)
    rows = lambda n, *s: pl.BlockSpec((n,) + s, lambda bi, *_: (bi,) + (0,) * len(s))
    sspecs, sargs = [], []
    for s in range(bb):
        for p in range(nps):
            for c in (skt, svt):
                sspecs.append(page(lambda bi, pt, s=s, p=p: (pt[(bi * bb + s) * n_pages + n_pages - 1 - p], 0, 0)))
                sargs.append(c)
    os_first, acc, r, alive = pl.pallas_call(
        functools.partial(_dec_sb_first_kernel, ts=ts, npg=nps, bb=bb),
        grid_spec=pltpu.PrefetchScalarGridSpec(
            num_scalar_prefetch=1,
            grid=(b // bb,),
            in_specs=[rows(bb, ts, HALF)] * 3 + sspecs,
            out_specs=[rows(bb, ts, HALF), rows(bb, _DEC_ROWS, HALF), rows(bb, _DEC_ROWS, LANES),
                       rows(bb, SUBLANES, LANES)],
        ),
        out_shape=[jax.ShapeDtypeStruct((b, ts, HALF), F32), jax.ShapeDtypeStruct((b, _DEC_ROWS, HALF), F32),
                   jax.ShapeDtypeStruct((b, _DEC_ROWS, LANES), F32),
                   jax.ShapeDtypeStruct((b, SUBLANES, LANES), jnp.int32)],
        compiler_params=_cp("parallel"),
    )(pt, sq, sk, sv, *sargs)
    if nrest == 0:
        return oc, os_first

    alive = alive[:, 0, 0]

    def rest_pages():
        rspecs, rargs = [], []
        for p in range(nrest):
            for c in (skt, svt):
                rspecs.append(page(lambda bi, pt, al, p=p: (
                    jnp.where(al[bi] == 1, pt[bi * n_pages + n_pages - 1 - nps - p], pt[0]), 0, 0)))
                rargs.append(c)
        return pl.pallas_call(
            functools.partial(_dec_sb_rest_kernel, ts=ts, npg=nps, nrest=nrest),
            grid_spec=pltpu.PrefetchScalarGridSpec(
                num_scalar_prefetch=2,
                grid=(b,),
                in_specs=[rows(1, ts, HALF), rows(1, _DEC_ROWS, HALF), rows(1, _DEC_ROWS, LANES)] + rspecs,
                out_specs=rows(1, ts, HALF),
                scratch_shapes=[pltpu.VMEM((_DEC_ROWS, HALF), F32), pltpu.VMEM((_DEC_ROWS, 1), F32)],
            ),
            out_shape=jax.ShapeDtypeStruct((b, ts, HALF), F32),
            compiler_params=_cp("arbitrary"),
        )(pt, alive, sq, acc, r, *rargs)

    os_ = lax.cond(jnp.any(alive == 1), rest_pages, lambda: os_first)
    return oc, os_


def _trunk(x, p, even_states, odd_past, page_table, W):
    b, t, d = x.shape
    m = b * t
    prompt = odd_past is None
    act = BF16 if prompt else F32
    h = x.reshape(m, d)
    depth = p.shape[0]
    new_even, new_odd = [], []
    for l in range(depth):
        j = l // 2
        h = _ffn_half(h, W['ffn_norm1'][l], W['ffn1_wi'][l], W['ffn1_wo'][l])
        if l % 2 == 0:
            buf, c0, n0, m0 = even_states[j]
            u, q, k, v, o, gt = _inproj_even(h, W['mix_norm'][l], W['ev_w_in'][j], W['ev_w_gt'][j], act)
            a_out, buf1 = _conv_module(u.reshape(b, t, HALF), buf, W['ev_conv_w'][j], W['ev_conv_b'][j],
                                       W['ev_ln_g'][j], W['ev_ln_b'][j], act)
            gt = jnp.moveaxis(gt.reshape(2 * H_B, b, t), 1, 0)
            if t < MLSTM_CHUNK:
                padv = jnp.where(jnp.arange(2 * H_B) < H_B, NEG_INF, -NEG_INF).astype(F32)
                gt = jnp.concatenate(
                    [gt, jnp.broadcast_to(padv[None, :, None], (b, 2 * H_B, MLSTM_CHUNK - t))], axis=2)
            r3 = lambda a: a.reshape(b, t, HALF)
            b_out, c1, n1, m1 = _mlstm(r3(q), r3(k), r3(v), r3(o), gt, W['ev_gate_b'][j], c0, n0, m0, act)
            new_even.append((buf1, c1, n1, m1))
            mix_a, mix_b, w_out = a_out.reshape(m, HALF), b_out.reshape(m, HALF), W['ev_w_out'][j]
        else:
            lam_init = 0.8 - 0.6 * math.exp(-0.3 * l)
            cq, ck, cv, sq, sk, sv = _inproj_odd(h, W['mix_norm'][l], W['od_w_in'][j], W['od_w_kvt'][j], act, b, t,
                                                 time_minor=prompt)
            heads = lambda a, nh: a.reshape(b, t, nh, HALF // nh)
            if prompt:
                sb_rows = lambda a: jnp.transpose(a.reshape(b, H_D, DH_D, t), (0, 3, 1, 2))
            else:
                sb_rows = lambda a: heads(a, H_D)
            new_odd.append((heads(ck, H_C), heads(cv, H_C), sb_rows(sk), sb_rows(sv)))
            if prompt:
                oc = _diff_attention(cq, ck, cv, b, t, W['rel_bias'], W['od_lambda'][j], W['od_head_g'][j], lam_init)
                os_ = _sb_attention(sq, sk, sv, b, t)
            else:
                r3 = lambda a: a.reshape(b, t, HALF)
                oc, os_ = _decode_attention(r3(cq), r3(sq), [r3(a) for a in (ck, cv, sk, sv)], odd_past[j],
                                            page_table[j], W['rel_bias'], W['od_lambda'][j], W['od_head_g'][j],
                                            lam_init)
            mix_a, mix_b, w_out = oc.reshape(m, HALF), os_.reshape(m, HALF), W['od_w_out'][j]
        h = _layer_tail(h, mix_a, mix_b, w_out, W['ffn_norm2'][l], W['ffn2_wi'][l], W['ffn2_wo'][l],
                        W['ple_norm'][l], W['ple_wg'][l], p[l].reshape(m, -1), W['ple_wp'][l], W['final_norm'],
                        final=(l == depth - 1))
    return h.reshape(b, t, d), new_even, new_odd


def kernel(x_prompt, x_sample, p_prompt, p_sample, state_conv, state_mlstm_C, state_mlstm_n, state_mlstm_m, cache_diff_k, cache_diff_v, cache_sb_k, cache_sb_v, page_table, ffn_norm1, ffn1_wi, ffn1_wo, mix_norm, ffn_norm2, ffn2_wi, ffn2_wo, ple_norm, ple_wg, ple_wp, ev_w_in, ev_conv_w, ev_conv_b, ev_ln_g, ev_ln_b, ev_gate_b, ev_w_out, od_w_in, od_lambda, od_head_g, od_w_out, rel_bias, final_norm):
    bf = lambda a: a.astype(BF16)
    n_even, n_odd = ev_w_in.shape[0], od_w_in.shape[0]
    W = dict(ffn_norm1=ffn_norm1, ffn1_wi=bf(ffn1_wi), ffn1_wo=bf(ffn1_wo), mix_norm=mix_norm,
             ffn_norm2=ffn_norm2, ffn2_wi=bf(ffn2_wi), ffn2_wo=bf(ffn2_wo),
             ple_norm=ple_norm, ple_wg=bf(ple_wg), ple_wp=bf(ple_wp),
             ev_w_in=bf(ev_w_in[:, :, :6 * HALF]), ev_w_gt=bf(jnp.swapaxes(ev_w_in[:, :, 6 * HALF:], 1, 2)),
             ev_conv_w=ev_conv_w, ev_conv_b=ev_conv_b, ev_ln_g=ev_ln_g, ev_ln_b=ev_ln_b,
             ev_gate_b=ev_gate_b, ev_w_out=bf(ev_w_out),
             od_w_in=bf(od_w_in), od_w_kvt=bf(jnp.swapaxes(od_w_in[:, :, 4 * HALF:], 1, 2)), od_lambda=od_lambda, od_head_g=od_head_g, od_w_out=bf(od_w_out),
             rel_bias=rel_bias, final_norm=final_norm)
    bp, tp = x_prompt.shape[0], x_prompt.shape[1]
    bs, ts = x_sample.shape[0], x_sample.shape[1]
    even_p = [(jnp.zeros((bp, CONV_W - 1, HALF), F32), jnp.zeros((bp, H_B, DH_B, DH_B), F32),
               jnp.zeros((bp, H_B, DH_B), F32), jnp.zeros((bp, H_B), F32)) for _ in range(n_even)]
    y_prompt, ev_p, od_p = _trunk(x_prompt, p_prompt, even_p, None, None, W)
    even_s = [(state_conv[j], state_mlstm_C[j], state_mlstm_n[j], state_mlstm_m[j]) for j in range(n_even)]
    n_pool, psz = cache_diff_k.shape[1], cache_diff_k.shape[2]
    pool_d = lambda c: c.reshape(n_odd * n_pool, psz * H_C, 2 * DH_C)
    pool_s = lambda c: jnp.transpose(c, (0, 1, 3, 4, 2)).reshape(n_odd * n_pool, H_D * DH_D, psz)
    caches = (pool_d(cache_diff_k), pool_d(cache_diff_v), pool_s(cache_sb_k), pool_s(cache_sb_v))
    tables = [page_table + j * n_pool for j in range(n_odd)]
    y_sample, ev_s, od_s = _trunk(x_sample, p_sample, even_s, [caches] * n_odd, tables, W)
    ev = lambda states, i: jnp.stack([s[i] for s in states])
    return (y_prompt, y_sample,
            ev(ev_p, 0), ev(ev_s, 0), ev(ev_p, 1), ev(ev_s, 1),
            ev(ev_p, 2), ev(ev_s, 2), ev(ev_p, 3), ev(ev_s, 3),
            ev(od_p, 0), ev(od_s, 0), ev(od_p, 1), ev(od_s, 1),
            ev(od_p, 2), ev(od_s, 2), ev(od_p, 3), ev(od_s, 3))
```
